```python
import jax, jax.numpy as jnp
from jax import lax
import numpy as np

D_MODEL = 4096
BATCH = 8
SEQ = 2048
DEPTH = 2
DEC_BATCH = 1
DEC_SEQ = 16384
PAST_LEN = 128

HEAD_DIM = 128
GLA_WIDTH = 3 * D_MODEL // 8
GLA_DV = 2 * HEAD_DIM
GLA_HEADS = GLA_WIDTH // GLA_DV
GLA_DK = HEAD_DIM
GLA_QK_WIDTH = GLA_HEADS * GLA_DK
GLA_CHUNK = 64
GLA_GATE_RANK = 16
GLA_GATE_TAU = 16.0
SGU_WIDTH = D_MODEL // 4
SGU_CHUNK = 128
SGU_GROUP_DIM = 128
SGU_GROUPS = SGU_WIDTH // SGU_GROUP_DIM
ATT_WIDTH = 3 * D_MODEL // 8
ATT_Q_HEADS = ATT_WIDTH // HEAD_DIM
ATT_KV_HEADS = 4
ATT_GROUP = ATT_Q_HEADS // ATT_KV_HEADS
ATT_KV_WIDTH = ATT_KV_HEADS * HEAD_DIM
WINDOW = 128
ROPE_THETA = 500000.0
ROPE_DIMS = HEAD_DIM // 4
N_BRANCH = 3
MIX_WIDTH = GLA_WIDTH + SGU_WIDTH + ATT_WIDTH
SPLIT_SIZES = (GLA_QK_WIDTH, GLA_QK_WIDTH, GLA_WIDTH, 2 * GLA_GATE_RANK, GLA_WIDTH,
               SGU_WIDTH, SGU_WIDTH, SGU_WIDTH,
               ATT_WIDTH, ATT_KV_WIDTH, ATT_KV_WIDTH, ATT_WIDTH,
               N_BRANCH * D_MODEL)
N_IN = sum(SPLIT_SIZES)
NORM_EPS = 1e-6
LN_EPS = 1e-5

kernel_name = "hybrid_gla_sgu_swa_gated_encoder"


def _rms_norm(x, gain):
    xf = x.astype(jnp.float32)
    y = xf * lax.rsqrt(jnp.mean(xf * xf, axis=-1, keepdims=True) + NORM_EPS)
    return (y * gain.astype(jnp.float32)).astype(x.dtype)


def _layer_norm(x, gain, bias):
    xf = x.astype(jnp.float32)
    mu = jnp.mean(xf, axis=-1, keepdims=True)
    xc = xf - mu
    y = xc * lax.rsqrt(jnp.mean(xc * xc, axis=-1, keepdims=True) + LN_EPS)
    return (y * gain.astype(jnp.float32) + bias.astype(jnp.float32)).astype(x.dtype)


def _partial_rope(x):
    S = x.shape[1]
    inv_freq = ROPE_THETA ** (-jnp.arange(0, ROPE_DIMS, 2, dtype=jnp.float32) / ROPE_DIMS)
    ang = jnp.arange(S, dtype=jnp.float32)[:, None] * inv_freq[None, :]
    cos = jnp.cos(ang)[None, :, None, :]
    sin = jnp.sin(ang)[None, :, None, :]
    xf = x.astype(jnp.float32)
    half = ROPE_DIMS // 2
    x1 = xf[..., :half]
    x2 = xf[..., half:ROPE_DIMS]
    out = jnp.concatenate([x1 * cos - x2 * sin, x2 * cos + x1 * sin, xf[..., ROPE_DIMS:]], axis=-1)
    return out.astype(x.dtype)


def _gla_chunked(q, k, v, log_a):
    B, S, H, DK = q.shape
    DV = v.shape[-1]
    C = GLA_CHUNK
    NC = S // C

    def to_chunks(t):
        return t.astype(jnp.float32).reshape(B, NC, C, H, t.shape[-1]).transpose(1, 0, 3, 2, 4)

    qc = to_chunks(q) * (DK ** -0.5)
    kc = to_chunks(k)
    vc = to_chunks(v)
    gc = to_chunks(log_a)
    lower = jnp.tril(jnp.ones((C, C), dtype=bool))

    def step(state, inp):
        qi, ki, vi, gi = inp
        b = jnp.cumsum(gi, axis=2)
        b_last = b[:, :, -1:, :]
        o_inter = jnp.einsum('bhtd,bhde->bhte', qi * jnp.exp(b), state)
        diff = b[:, :, :, None, :] - b[:, :, None, :, :]
        decay = jnp.exp(jnp.where(lower[:, :, None], diff, -jnp.inf))
        scores = jnp.einsum('bhtsd,bhsd->bhts', qi[:, :, :, None, :] * decay, ki)
        o_intra = jnp.einsum('bhts,bhse->bhte', scores, vi)
        new_state = (jnp.exp(b_last[:, :, 0, :, None]) * state
                     + jnp.einsum('bhsd,bhse->bhde', ki * jnp.exp(b_last - b), vi))
        return new_state, o_inter + o_intra

    state0 = jnp.zeros((B, H, DK, DV), jnp.float32)
    _, o = lax.scan(step, state0, (qc, kc, vc, gc))
    return o.transpose(1, 0, 3, 2, 4).reshape(B, S, H, DV)


def _window_attention(q, k, v, sink):
    B, S = q.shape[:2]
    nb = S // WINDOW
    qb = q.reshape(B, nb, WINDOW, ATT_KV_HEADS, ATT_GROUP, HEAD_DIM)
    pad = ((0, 0), (WINDOW, WINDOW), (0, 0), (0, 0))
    kp = jnp.pad(k, pad).reshape(B, nb + 2, WINDOW, ATT_KV_HEADS, HEAD_DIM)
    vp = jnp.pad(v, pad).reshape(B, nb + 2, WINDOW, ATT_KV_HEADS, HEAD_DIM)
    kb = jnp.concatenate([kp[:, :-2], kp[:, 1:-1], kp[:, 2:]], axis=2)
    vb = jnp.concatenate([vp[:, :-2], vp[:, 1:-1], vp[:, 2:]], axis=2)
    s = jnp.einsum('bnqhgd,bnkhd->bnhgqk', qb, kb,
                   preferred_element_type=jnp.float32) * (HEAD_DIM ** -0.5)
    blk = jnp.arange(nb)[:, None, None]
    qpos = blk * WINDOW + jnp.arange(WINDOW)[None, :, None]
    kpos = blk * WINDOW - WINDOW + jnp.arange(3 * WINDOW)[None, None, :]
    valid = (jnp.abs(kpos - qpos) <= WINDOW) & (kpos >= 0) & (kpos < S)
    s = jnp.where(valid[None, :, None, None], s, -jnp.inf)
    sink_logit = jnp.broadcast_to(
        sink.astype(jnp.float32).reshape(1, 1, ATT_KV_HEADS, ATT_GROUP, 1, 1), s.shape[:-1] + (1,))
    p = jax.nn.softmax(jnp.concatenate([s, sink_logit], axis=-1), axis=-1)[..., :-1]
    o = jnp.einsum('bnhgqk,bnkhd->bnqhgd', p.astype(v.dtype), vb)
    return o.reshape(B, S, ATT_Q_HEADS * HEAD_DIM)


def _layer(x, norm_gain, w_in, gla_gate_up, gla_gate_bias, gla_norm_gain,
           sgu_ln_gain, sgu_ln_bias, sgu_w, sgu_b, q_norm_gain, k_norm_gain, sink,
           gate_bias, w_br, w_out):
    B, S, _ = x.shape
    h = _rms_norm(x, norm_gain) @ w_in
    split_idx = [int(i) for i in np.cumsum(SPLIT_SIZES)[:-1]]
    (a_q, a_k, a_v, a_lr, a_gate, b_u, b_v, b_gate,
     c_q, c_k, c_v, c_gate, g_merge) = jnp.split(h, split_idx, axis=-1)

    qa = a_q.reshape(B, S, GLA_HEADS, GLA_DK)
    ka = a_k.reshape(B, S, GLA_HEADS, GLA_DK)
    va = a_v.reshape(B, S, GLA_HEADS, GLA_DV)
    lr = a_lr.reshape(B, S, 2, GLA_GATE_RANK)
    z = jnp.einsum('bsjr,jrk->bsjk', lr, gla_gate_up) + gla_gate_bias
    log_a = (jax.nn.log_sigmoid(z.astype(jnp.float32)) / GLA_GATE_TAU).reshape(B, S, 2, GLA_HEADS, GLA_DK)
    fwd = _gla_chunked(qa, ka, va, log_a[:, :, 0])
    rev = lambda t: jnp.flip(t, axis=1)
    bwd = rev(_gla_chunked(rev(qa), rev(ka), rev(va), rev(log_a[:, :, 1])))
    o_a = _rms_norm(fwd + bwd, gla_norm_gain).astype(x.dtype).reshape(B, S, GLA_WIDTH) * jax.nn.silu(a_gate)

    u = jax.nn.gelu(b_u)
    vv = _layer_norm(jax.nn.gelu(b_v), sgu_ln_gain, sgu_ln_bias)
    vr = vv.reshape(B, S // SGU_CHUNK, SGU_CHUNK, SGU_GROUPS, SGU_GROUP_DIM)
    mixed = jnp.einsum('gpq,bnqgc->bnpgc', sgu_w, vr) + sgu_b.T[None, None, :, :, None]
    o_b = u * mixed.reshape(B, S, SGU_WIDTH) * jax.nn.silu(b_gate)

    qc = _partial_rope(_rms_norm(c_q.reshape(B, S, ATT_Q_HEADS, HEAD_DIM), q_norm_gain))
    kc = _partial_rope(_rms_norm(c_k.reshape(B, S, ATT_KV_HEADS, HEAD_DIM), k_norm_gain))
    vc = c_v.reshape(B, S, ATT_KV_HEADS, HEAD_DIM)
    o_c = _window_attention(qc, kc, vc, sink) * jax.nn.silu(c_gate)

    gates = jax.nn.sigmoid(g_merge.reshape(B, S, N_BRANCH, D_MODEL) + gate_bias)
    y_a = o_a @ w_br[:GLA_WIDTH]
    y_b = o_b @ w_br[GLA_WIDTH:GLA_WIDTH + SGU_WIDTH]
    y_c = o_c @ w_br[GLA_WIDTH + SGU_WIDTH:]
    merged = gates[:, :, 0] * y_a + gates[:, :, 1] * y_b + gates[:, :, 2] * y_c
    return x + merged @ w_out


def setup_inputs(seed: int = 0) -> dict:
    key = jax.random.key(seed)
    ks = jax.random.split(key, 17)
    nrm = jax.random.normal
    f32 = jnp.float32
    return {
        "x_prompt": nrm(ks[0], (BATCH, SEQ, D_MODEL), f32),
        "x_sample": nrm(ks[1], (DEC_BATCH, DEC_SEQ, D_MODEL), f32),
        "norm_gain": 1.0 + 0.02 * nrm(ks[2], (DEPTH, D_MODEL), f32),
        "w_in": nrm(ks[3], (DEPTH, D_MODEL, N_IN), f32) * (D_MODEL ** -0.5),
        "gla_gate_up": nrm(ks[4], (DEPTH, 2, GLA_GATE_RANK, GLA_QK_WIDTH), f32) * (GLA_GATE_RANK ** -0.5),
        "gla_gate_bias": 0.1 * nrm(ks[5], (DEPTH, 2, GLA_QK_WIDTH), f32),
        "gla_norm_gain": 1.0 + 0.02 * nrm(ks[6], (DEPTH, GLA_DV), f32),
        "sgu_ln_gain": 1.0 + 0.02 * nrm(ks[7], (DEPTH, SGU_WIDTH), f32),
        "sgu_ln_bias": 0.02 * nrm(ks[8], (DEPTH, SGU_WIDTH), f32),
        "sgu_w": nrm(ks[9], (DEPTH, SGU_GROUPS, SGU_CHUNK, SGU_CHUNK), f32) * (SGU_CHUNK ** -0.5),
        "sgu_b": 1.0 + 0.02 * nrm(ks[10], (DEPTH, SGU_GROUPS, SGU_CHUNK), f32),
        "q_norm_gain": 1.0 + 0.02 * nrm(ks[11], (DEPTH, HEAD_DIM), f32),
        "k_norm_gain": 1.0 + 0.02 * nrm(ks[12], (DEPTH, HEAD_DIM), f32),
        "sink": 0.5 * nrm(ks[13], (DEPTH, ATT_Q_HEADS), f32),
        "gate_bias": 0.1 * nrm(ks[14], (DEPTH, N_BRANCH, D_MODEL), f32),
        "w_br": nrm(ks[15], (DEPTH, MIX_WIDTH, D_MODEL), f32) * (MIX_WIDTH ** -0.5),
        "w_out": nrm(ks[16], (DEPTH, D_MODEL, D_MODEL), f32) * (D_MODEL ** -0.5),
    }


def reference(x_prompt, x_sample, norm_gain, w_in, gla_gate_up, gla_gate_bias, gla_norm_gain,
              sgu_ln_gain, sgu_ln_bias, sgu_w, sgu_b, q_norm_gain, k_norm_gain, sink,
              gate_bias, w_br, w_out):
    y_prompt = x_prompt
    y_sample = x_sample
    for l in range(DEPTH):
        params = (norm_gain[l], w_in[l], gla_gate_up[l], gla_gate_bias[l], gla_norm_gain[l],
                  sgu_ln_gain[l], sgu_ln_bias[l], sgu_w[l], sgu_b[l], q_norm_gain[l],
                  k_norm_gain[l], sink[l], gate_bias[l], w_br[l], w_out[l])
        y_prompt = _layer(y_prompt, *params)
        y_sample = _layer(y_sample, *params)
    return (y_prompt, y_sample)
```

```python
import functools
import math

import jax
import jax.numpy as jnp
from jax import lax
from jax.experimental import pallas as pl
from jax.experimental.pallas import tpu as pltpu

F32 = jnp.float32
BF16 = jnp.bfloat16

D_MODEL = 4096
HEAD_DIM = 128
GLA_WIDTH = 1536
GLA_DV = 256
GLA_HEADS = 6
GLA_DK = 128
GLA_QK_WIDTH = 768
GLA_GATE_RANK = 16
GLA_GATE_TAU = 16.0
SGU_WIDTH = 1024
SGU_CHUNK = 128
SGU_GROUPS = 8
ATT_WIDTH = 1536
ATT_Q_HEADS = 12
ATT_KV_HEADS = 4
ATT_GROUP = 3
ATT_KV_WIDTH = 512
WINDOW = 128
ROPE_THETA = 500000.0
ROPE_DIMS = 32
N_BRANCH = 3
NORM_EPS = 1e-6
LN_EPS = 1e-5

VMEM_LIMIT_BYTES = 56 * 1024 * 1024
LANES = 128

NORM_ROWS = 256
PROJ_TM = 1024
PROJ_TN = 512
GLA_ROWS = 256
GLA_CHUNK = 64
GLA_SUB = 16
SGU_ROWS = 512
MERGE_TM = 512
MERGE_TN = 512
OUT_TM = 1024
OUT_TN = 512
EXP_CLAMP = 80.0


def _params(*sem):
    return pltpu.CompilerParams(dimension_semantics=sem, vmem_limit_bytes=VMEM_LIMIT_BYTES)


def _silu(x):
    return x * jax.nn.sigmoid(x)


def _gelu_tanh(x):
    c = math.sqrt(2.0 / math.pi)
    return x * (0.5 * (1.0 + jnp.tanh(c * (x + 0.044715 * (x * x * x)))))


def _norm_kernel(x_ref, g_ref, o_ref):
    x = x_ref[...]
    ms = jnp.mean(x * x, axis=-1, keepdims=True)
    o_ref[...] = (x * lax.rsqrt(ms + NORM_EPS) * g_ref[...]).astype(BF16)


def _pre_norm(x2, gain):
    t = x2.shape[0]
    return pl.pallas_call(
        _norm_kernel,
        grid=(t // NORM_ROWS,),
        in_specs=[pl.BlockSpec((NORM_ROWS, D_MODEL), lambda i: (i, 0)),
                  pl.BlockSpec((1, D_MODEL), lambda i: (0, 0))],
        out_specs=pl.BlockSpec((NORM_ROWS, D_MODEL), lambda i: (i, 0)),
        out_shape=jax.ShapeDtypeStruct((t, D_MODEL), BF16),
        compiler_params=_params("parallel"),
        name="pre_norm",
    )(x2, gain.reshape(1, D_MODEL))


def _proj_a_kernel(x_ref, w_ref, o_ref):
    j = pl.program_id(1)
    acc = jnp.dot(x_ref[...], w_ref[...], preferred_element_type=F32)
    n_raw = (2 * GLA_QK_WIDTH + GLA_WIDTH) // PROJ_TN

    @pl.when(j < n_raw)
    def _():
        o_ref[...] = acc.astype(BF16)

    @pl.when(j >= n_raw)
    def _():
        o_ref[...] = _silu(acc).astype(BF16)


def _proj_b_kernel(x_ref, w_ref, o_ref):
    j = pl.program_id(1)
    acc = jnp.dot(x_ref[...], w_ref[...], preferred_element_type=F32)
    n_gelu = (2 * SGU_WIDTH) // PROJ_TN

    @pl.when(j < n_gelu)
    def _():
        o_ref[...] = _gelu_tanh(acc).astype(BF16)

    @pl.when(j >= n_gelu)
    def _():
        o_ref[...] = _silu(acc).astype(BF16)


def _head_norm_rope(acc, gain, cos_t, sin_lo, sin_hi):
    outs = []
    for h in range(PROJ_TN // HEAD_DIM):
        xh = acc[:, h * HEAD_DIM:(h + 1) * HEAD_DIM]
        ms = jnp.mean(xh * xh, axis=-1, keepdims=True)
        y = xh * lax.rsqrt(ms + NORM_EPS) * gain
        half = ROPE_DIMS // 2
        up = pltpu.roll(y, HEAD_DIM - half, axis=1)
        dn = pltpu.roll(y, half, axis=1)
        outs.append(y * cos_t + up * sin_lo + dn * sin_hi)
    return jnp.concatenate(outs, axis=1)


def _proj_c_kernel(x_ref, w_ref, qg_ref, kg_ref, cos_ref, slo_ref, shi_ref, o_ref):
    j = pl.program_id(1)
    acc = jnp.dot(x_ref[...], w_ref[...], preferred_element_type=F32)
    n_q = ATT_WIDTH // PROJ_TN
    k_tile = 2 * n_q
    v_tile = k_tile + 1

    @pl.when(j < n_q)
    def _():
        o_ref[...] = _head_norm_rope(acc, qg_ref[...], cos_ref[...], slo_ref[...], shi_ref[...]).astype(BF16)

    @pl.when(jnp.logical_and(j >= n_q, j < k_tile))
    def _():
        o_ref[...] = _silu(acc).astype(BF16)

    @pl.when(j == k_tile)
    def _():
        o_ref[...] = _head_norm_rope(acc, kg_ref[...], cos_ref[...], slo_ref[...], shi_ref[...]).astype(BF16)

    @pl.when(j == v_tile)
    def _():
        o_ref[...] = acc.astype(BF16)


def _proj(kernel, xn, w, extra=(), extra_specs=(), name="proj"):
    t = xn.shape[0]
    n = w.shape[1]
    tm = min(PROJ_TM, t)
    return pl.pallas_call(
        kernel,
        grid=(t // tm, n // PROJ_TN),
        in_specs=[pl.BlockSpec((tm, D_MODEL), lambda i, j: (i, 0)),
                  pl.BlockSpec((D_MODEL, PROJ_TN), lambda i, j: (0, j)),
                  *extra_specs],
        out_specs=pl.BlockSpec((tm, PROJ_TN), lambda i, j: (i, j)),
        out_shape=jax.ShapeDtypeStruct((t, n), BF16),
        compiler_params=_params("parallel", "arbitrary"),
        name=name,
    )(xn, w, *extra)


def _rope_tables(seq):
    inv_freq = ROPE_THETA ** (-jnp.arange(0, ROPE_DIMS, 2, dtype=F32) / ROPE_DIMS)
    ang = jnp.arange(seq, dtype=F32)[:, None] * inv_freq[None, :]
    cos, sin = jnp.cos(ang), jnp.sin(ang)
    half = ROPE_DIMS // 2
    ones = jnp.ones((seq, HEAD_DIM - ROPE_DIMS), F32)
    zeros = jnp.zeros((seq, HEAD_DIM - ROPE_DIMS), F32)
    zh = jnp.zeros((seq, half), F32)
    cos_t = jnp.concatenate([cos, cos, ones], axis=1)
    sin_lo = jnp.concatenate([-sin, zh, zeros], axis=1)
    sin_hi = jnp.concatenate([zh, sin, zeros], axis=1)
    return cos_t, sin_lo, sin_hi


def _decay_kernel(x_ref, w_ref, up_ref, b_ref, o_ref):
    lr = jnp.dot(x_ref[...], w_ref[...], preferred_element_type=F32)
    z = jnp.dot(lr.astype(BF16), up_ref[...], preferred_element_type=F32) + b_ref[...]
    o_ref[...] = (jnp.minimum(z, 0.0) - jnp.log1p(jnp.exp(-jnp.abs(z)))) * (1.0 / GLA_GATE_TAU)


def _gla_log_decay(xn, w_lr, up2, bias2):
    t = xn.shape[0]
    tm = min(PROJ_TM, t)
    n = 2 * GLA_QK_WIDTH
    return pl.pallas_call(
        _decay_kernel,
        grid=(t // tm,),
        in_specs=[pl.BlockSpec((tm, D_MODEL), lambda i: (i, 0)),
                  pl.BlockSpec((D_MODEL, LANES), lambda i: (0, 0)),
                  pl.BlockSpec((LANES, n), lambda i: (0, 0)),
                  pl.BlockSpec((1, n), lambda i: (0, 0))],
        out_specs=pl.BlockSpec((tm, n), lambda i: (i, 0)),
        out_shape=jax.ShapeDtypeStruct((t, n), F32),
        compiler_params=_params("parallel"),
        name="gla_log_decay",
    )(xn, w_lr, up2, bias2)


def _gla_chunk(q, k, v, g, state, reverse):
    c_rows = q.shape[0]
    row = lax.broadcasted_iota(jnp.int32, (c_rows, c_rows), 0)
    col = lax.broadcasted_iota(jnp.int32, (c_rows, c_rows), 1)
    tri = (col >= row) if reverse else (col <= row)
    tri_b = tri.astype(BF16)
    g_hi = g.astype(BF16)
    g_lo = (g - g_hi.astype(F32)).astype(BF16)
    b = (jnp.dot(tri_b, g_hi, preferred_element_type=F32)
         + jnp.dot(tri_b, g_lo, preferred_element_type=F32))
    b_ex = b - g
    edge = 0 if reverse else c_rows - 1
    b_tot = b[edge:edge + 1, :]

    o = jnp.dot((q * jnp.exp(b)).astype(BF16), state.astype(BF16), preferred_element_type=F32)

    p_rows = []
    for i in range(c_rows // GLA_SUB):
        lo = i * GLA_SUB
        first = lo + GLA_SUB - 1 if reverse else lo
        ref = b_ex[first:first + 1, :]
        qt = q[lo:lo + GLA_SUB, :] * jnp.exp(b[lo:lo + GLA_SUB, :] - ref)
        kt = k * jnp.exp(jnp.minimum(ref - b, EXP_CLAMP))
        s = lax.dot_general(qt.astype(BF16), kt.astype(BF16), (((1,), (1,)), ((), ())),
                            preferred_element_type=F32)
        p_rows.append(jnp.where(tri[lo:lo + GLA_SUB, :], s, 0.0))
    p = jnp.concatenate(p_rows, axis=0)
    o = o + jnp.dot(p.astype(BF16), v, preferred_element_type=F32)

    kb = k * jnp.exp(b_tot - b)
    upd = lax.dot_general(kb.astype(BF16), v, (((0,), (0,)), ((), ())), preferred_element_type=F32)
    dec = jnp.exp(jnp.transpose(jnp.broadcast_to(b_tot, (GLA_DK, GLA_DK))))
    new_state = jnp.concatenate([dec] * (GLA_DV // GLA_DK), axis=1) * state + upd
    return o, new_state


def _gla_scan_rows(q_ref, k_ref, v_ref, g_ref, state_ref, reverse, emit):
    n_chunks = GLA_ROWS // GLA_CHUNK
    order = range(n_chunks - 1, -1, -1) if reverse else range(n_chunks)
    state = state_ref[...]
    for ci in order:
        rows = pl.ds(ci * GLA_CHUNK, GLA_CHUNK)
        q = q_ref[rows, :].astype(F32) * (GLA_DK ** -0.5)
        k = k_ref[rows, :].astype(F32)
        o, state = _gla_chunk(q, k, v_ref[rows, :], g_ref[rows, :], state, reverse)
        emit(rows, o)
    state_ref[...] = state


def _gla_fwd_kernel(q_ref, k_ref, v_ref, g_ref, o_ref, state_ref):
    @pl.when(pl.program_id(2) == 0)
    def _():
        state_ref[...] = jnp.zeros_like(state_ref)

    def emit(rows, o):
        o_ref[rows, :] = o

    _gla_scan_rows(q_ref, k_ref, v_ref, g_ref, state_ref, False, emit)


def _gla_bwd_kernel(q_ref, k_ref, v_ref, g_ref, fwd_ref, gate_ref, gain_ref, o_ref, state_ref):
    @pl.when(pl.program_id(2) == 0)
    def _():
        state_ref[...] = jnp.zeros_like(state_ref)

    def emit(rows, o):
        tot = fwd_ref[rows, :] + o
        ms = jnp.mean(tot * tot, axis=-1, keepdims=True)
        y = tot * lax.rsqrt(ms + NORM_EPS) * gain_ref[...]
        o_ref[rows, :] = (y * gate_ref[rows, :].astype(F32)).astype(BF16)

    _gla_scan_rows(q_ref, k_ref, v_ref, g_ref, state_ref, True, emit)


def _gla(h_a, log_a, gain, batch, seq):
    t = batch * seq
    nb = seq // GLA_ROWS
    k_off = GLA_QK_WIDTH // GLA_DK
    v_off = 2 * GLA_QK_WIDTH // GLA_DV
    gate_off = (2 * GLA_QK_WIDTH + GLA_WIDTH) // GLA_DV
    grid = (batch, GLA_HEADS, nb)
    scratch = [pltpu.VMEM((GLA_DK, GLA_DV), F32)]

    def specs(rowmap, direction):
        return [pl.BlockSpec((GLA_ROWS, GLA_DK), lambda b, h, n: (rowmap(b, n), h)),
                pl.BlockSpec((GLA_ROWS, GLA_DK), lambda b, h, n: (rowmap(b, n), k_off + h)),
                pl.BlockSpec((GLA_ROWS, GLA_DV), lambda b, h, n: (rowmap(b, n), v_off + h)),
                pl.BlockSpec((GLA_ROWS, GLA_DK), lambda b, h, n: (rowmap(b, n), direction * GLA_HEADS + h))]

    fmap = lambda b, n: b * nb + n
    rmap = lambda b, n: b * nb + (nb - 1 - n)
    fwd = pl.pallas_call(
        _gla_fwd_kernel,
        grid=grid,
        in_specs=specs(fmap, 0),
        out_specs=pl.BlockSpec((GLA_ROWS, GLA_DV), lambda b, h, n: (fmap(b, n), h)),
        out_shape=jax.ShapeDtypeStruct((t, GLA_WIDTH), F32),
        scratch_shapes=scratch,
        compiler_params=_params("parallel", "parallel", "arbitrary"),
        name="gla_fwd",
    )(h_a, h_a, h_a, log_a)
    return pl.pallas_call(
        _gla_bwd_kernel,
        grid=grid,
        in_specs=specs(rmap, 1) + [
            pl.BlockSpec((GLA_ROWS, GLA_DV), lambda b, h, n: (rmap(b, n), h)),
            pl.BlockSpec((GLA_ROWS, GLA_DV), lambda b, h, n: (rmap(b, n), gate_off + h)),
            pl.BlockSpec((1, GLA_DV), lambda b, h, n: (0, 0))],
        out_specs=pl.BlockSpec((GLA_ROWS, GLA_DV), lambda b, h, n: (rmap(b, n), h)),
        out_shape=jax.ShapeDtypeStruct((t, GLA_WIDTH), BF16),
        scratch_shapes=scratch,
        compiler_params=_params("parallel", "parallel", "arbitrary"),
        name="gla_bwd",
    )(h_a, h_a, h_a, log_a, fwd, h_a, gain.reshape(1, GLA_DV))


def _sgu_kernel(u_ref, v_ref, gate_ref, lng_ref, lnb_ref, w_ref, bt_ref, o_ref):
    for c in range(SGU_ROWS // SGU_CHUNK):
        rows = pl.ds(c * SGU_CHUNK, SGU_CHUNK)
        x = v_ref[rows, :].astype(F32)
        mu = jnp.mean(x, axis=-1, keepdims=True)
        xc = x - mu
        var = jnp.mean(xc * xc, axis=-1, keepdims=True)
        y = (xc * lax.rsqrt(var + LN_EPS) * lng_ref[...] + lnb_ref[...]).astype(BF16)
        for g in range(SGU_GROUPS):
            cols = pl.ds(g * LANES, LANES)
            mixed = jnp.dot(w_ref[g], y[:, g * LANES:(g + 1) * LANES], preferred_element_type=F32)
            mixed = mixed + bt_ref[:, g:g + 1]
            o_ref[rows, cols] = (u_ref[rows, cols].astype(F32) * mixed
                                 * gate_ref[rows, cols].astype(F32)).astype(BF16)


def _sgu(h_b, ln_gain, ln_bias, w, b):
    t = h_b.shape[0]
    return pl.pallas_call(
        _sgu_kernel,
        grid=(t // SGU_ROWS,),
        in_specs=[pl.BlockSpec((SGU_ROWS, SGU_WIDTH), lambda i: (i, 0)),
                  pl.BlockSpec((SGU_ROWS, SGU_WIDTH), lambda i: (i, 1)),
                  pl.BlockSpec((SGU_ROWS, SGU_WIDTH), lambda i: (i, 2)),
                  pl.BlockSpec((1, SGU_WIDTH), lambda i: (0, 0)),
                  pl.BlockSpec((1, SGU_WIDTH), lambda i: (0, 0)),
                  pl.BlockSpec((SGU_GROUPS, SGU_CHUNK, SGU_CHUNK), lambda i: (0, 0, 0)),
                  pl.BlockSpec((SGU_CHUNK, SGU_GROUPS), lambda i: (0, 0))],
        out_specs=pl.BlockSpec((SGU_ROWS, SGU_WIDTH), lambda i: (i, 0)),
        out_shape=jax.ShapeDtypeStruct((t, SGU_WIDTH), BF16),
        compiler_params=_params("parallel"),
        name="sgu",
    )(h_b, h_b, h_b, ln_gain.reshape(1, SGU_WIDTH), ln_bias.reshape(1, SGU_WIDTH),
      w.astype(BF16), jnp.transpose(b))


def _attn_kernel(sink_ref, q_ref, gate_ref, kp_ref, kc_ref, kn_ref, vp_ref, vc_ref, vn_ref, o_ref, *, nb):
    n = pl.program_id(1)
    qpos = lax.broadcasted_iota(jnp.int32, (WINDOW, 3 * WINDOW), 0)
    kpos = lax.broadcasted_iota(jnp.int32, (WINDOW, 3 * WINDOW), 1) - WINDOW
    valid = jnp.abs(kpos - qpos) <= WINDOW
    valid = jnp.logical_and(valid, jnp.logical_or(kpos >= 0, n > 0))
    valid = jnp.logical_and(valid, jnp.logical_or(kpos < WINDOW, n < nb - 1))
    valid = jnp.concatenate([valid] * ATT_GROUP, axis=0)
    for hk in range(ATT_KV_HEADS):
        kcols = pl.ds(hk * HEAD_DIM, HEAD_DIM)
        q = jnp.concatenate([q_ref[:, pl.ds((hk * ATT_GROUP + g) * HEAD_DIM, HEAD_DIM)]
                             for g in range(ATT_GROUP)], axis=0)
        k = jnp.concatenate([kp_ref[:, kcols], kc_ref[:, kcols], kn_ref[:, kcols]], axis=0)
        v = jnp.concatenate([vp_ref[:, kcols], vc_ref[:, kcols], vn_ref[:, kcols]], axis=0)
        s = lax.dot_general(q, k, (((1,), (1,)), ((), ())), preferred_element_type=F32) * (HEAD_DIM ** -0.5)
        s = jnp.where(valid, s, -jnp.inf)
        sink = jnp.concatenate([jnp.full((WINDOW, 1), sink_ref[hk * ATT_GROUP + g], F32)
                                for g in range(ATT_GROUP)], axis=0)
        m = jnp.maximum(jnp.max(s, axis=-1, keepdims=True), sink)
        e = jnp.exp(s - m)
        denom = jnp.sum(e, axis=-1, keepdims=True) + jnp.exp(sink - m)
        o = jnp.dot(e.astype(BF16), v, preferred_element_type=F32) / denom
        for g in range(ATT_GROUP):
            cols = pl.ds((hk * ATT_GROUP + g) * HEAD_DIM, HEAD_DIM)
            o_ref[:, cols] = (o[g * WINDOW:(g + 1) * WINDOW, :] * gate_ref[:, cols].astype(F32)).astype(BF16)


def _attention(h_c, sink, batch, seq):
    t = batch * seq
    nb = seq // WINDOW
    k_blk = 2 * ATT_WIDTH // ATT_KV_WIDTH
    v_blk = k_blk + 1
    cur = lambda b, n: b * nb + n
    prv = lambda b, n: b * nb + jnp.maximum(n - 1, 0)
    nxt = lambda b, n: b * nb + jnp.minimum(n + 1, nb - 1)
    kv = lambda rowmap, blk: pl.BlockSpec((WINDOW, ATT_KV_WIDTH), lambda b, n: (rowmap(b, n), blk))
    return pl.pallas_call(
        functools.partial(_attn_kernel, nb=nb),
        grid=(batch, nb),
        in_specs=[pl.BlockSpec(memory_space=pltpu.SMEM),
                  pl.BlockSpec((WINDOW, ATT_WIDTH), lambda b, n: (cur(b, n), 0)),
                  pl.BlockSpec((WINDOW, ATT_WIDTH), lambda b, n: (cur(b, n), 1)),
                  kv(prv, k_blk), kv(cur, k_blk), kv(nxt, k_blk),
                  kv(prv, v_blk), kv(cur, v_blk), kv(nxt, v_blk)],
        out_specs=pl.BlockSpec((WINDOW, ATT_WIDTH), lambda b, n: (cur(b, n), 0)),
        out_shape=jax.ShapeDtypeStruct((t, ATT_WIDTH), BF16),
        compiler_params=_params("parallel", "parallel"),
        name="window_attention",
    )(sink, h_c, h_c, h_c, h_c, h_c, h_c, h_c, h_c)


def _merge_kernel(x_ref, oa_ref, ob_ref, oc_ref, g0_ref, g1_ref, g2_ref, wa_ref, wb_ref, wc_ref, bias_ref, o_ref):
    x = x_ref[...]
    acc = None
    for i, (g_ref, o_in, w_ref) in enumerate(((g0_ref, oa_ref, wa_ref), (g1_ref, ob_ref, wb_ref),
                                               (g2_ref, oc_ref, wc_ref))):
        gate = jax.nn.sigmoid(jnp.dot(x, g_ref[...], preferred_element_type=F32) + bias_ref[i:i + 1, :])
        term = gate * jnp.dot(o_in[...], w_ref[...], preferred_element_type=F32)
        acc = term if acc is None else acc + term
    o_ref[...] = acc.astype(BF16)


def _merge(xn, o_a, o_b, o_c, w_g, w_br_a, w_br_b, w_br_c, gate_bias):
    t = xn.shape[0]
    tm = min(MERGE_TM, t)
    nj = D_MODEL // MERGE_TN
    row = lambda width: pl.BlockSpec((tm, width), lambda i, j: (i, 0))
    gcol = lambda br: pl.BlockSpec((D_MODEL, MERGE_TN), lambda i, j: (0, br * nj + j))
    wcol = lambda width: pl.BlockSpec((width, MERGE_TN), lambda i, j: (0, j))
    return pl.pallas_call(
        _merge_kernel,
        grid=(t // tm, nj),
        in_specs=[row(D_MODEL), row(GLA_WIDTH), row(SGU_WIDTH), row(ATT_WIDTH),
                  gcol(0), gcol(1), gcol(2),
                  wcol(GLA_WIDTH), wcol(SGU_WIDTH), wcol(ATT_WIDTH),
                  pl.BlockSpec((N_BRANCH, MERGE_TN), lambda i, j: (0, j))],
        out_specs=pl.BlockSpec((tm, MERGE_TN), lambda i, j: (i, j)),
        out_shape=jax.ShapeDtypeStruct((t, D_MODEL), BF16),
        compiler_params=_params("parallel", "arbitrary"),
        name="gated_merge",
    )(xn, o_a, o_b, o_c, w_g, w_g, w_g, w_br_a, w_br_b, w_br_c, gate_bias)


def _out_kernel(m_ref, w_ref, x_ref, o_ref):
    o_ref[...] = x_ref[...] + jnp.dot(m_ref[...], w_ref[...], preferred_element_type=F32)


def _out_proj(merged, w_out, x2):
    t = x2.shape[0]
    tm = min(OUT_TM, t)
    return pl.pallas_call(
        _out_kernel,
        grid=(t // tm, D_MODEL // OUT_TN),
        in_specs=[pl.BlockSpec((tm, D_MODEL), lambda i, j: (i, 0)),
                  pl.BlockSpec((D_MODEL, OUT_TN), lambda i, j: (0, j)),
                  pl.BlockSpec((tm, OUT_TN), lambda i, j: (i, j))],
        out_specs=pl.BlockSpec((tm, OUT_TN), lambda i, j: (i, j)),
        out_shape=jax.ShapeDtypeStruct((t, D_MODEL), F32),
        compiler_params=_params("parallel", "arbitrary"),
        name="out_proj",
    )(merged, w_out, x2)


def _prepare_layer(norm_gain, w_in, gla_gate_up, gla_gate_bias, gla_norm_gain, sgu_ln_gain, sgu_ln_bias,
                   sgu_w, sgu_b, q_norm_gain, k_norm_gain, sink, gate_bias, w_br, w_out):
    sizes = (GLA_QK_WIDTH, GLA_QK_WIDTH, GLA_WIDTH, 2 * GLA_GATE_RANK, GLA_WIDTH,
             SGU_WIDTH, SGU_WIDTH, SGU_WIDTH,
             ATT_WIDTH, ATT_KV_WIDTH, ATT_KV_WIDTH, ATT_WIDTH, N_BRANCH * D_MODEL)
    offs = [0]
    for s in sizes:
        offs.append(offs[-1] + s)
    col = lambda i: w_in[:, offs[i]:offs[i + 1]]
    a_q, a_k, a_v, a_lr, a_gate, b_u, b_v, b_gate, c_q, c_k, c_v, c_gate, g_merge = (col(i) for i in range(13))
    w_a = jnp.concatenate([a_q, a_k, a_v, a_gate], axis=1).astype(BF16)
    w_b = jnp.concatenate([b_u, b_v, b_gate], axis=1).astype(BF16)
    w_c = jnp.concatenate([c_q, c_gate, c_k, c_v], axis=1).astype(BF16)
    w_lr = jnp.pad(a_lr, ((0, 0), (0, LANES - 2 * GLA_GATE_RANK))).astype(BF16)
    up2 = jnp.zeros((LANES, 2 * GLA_QK_WIDTH), F32)
    up2 = up2.at[:GLA_GATE_RANK, :GLA_QK_WIDTH].set(gla_gate_up[0])
    up2 = up2.at[GLA_GATE_RANK:2 * GLA_GATE_RANK, GLA_QK_WIDTH:].set(gla_gate_up[1])
    return dict(
        norm_gain=norm_gain, w_a=w_a, w_b=w_b, w_c=w_c, w_lr=w_lr, up2=up2.astype(BF16),
        bias2=gla_gate_bias.reshape(1, 2 * GLA_QK_WIDTH), gla_norm_gain=gla_norm_gain,
        sgu_ln_gain=sgu_ln_gain, sgu_ln_bias=sgu_ln_bias, sgu_w=sgu_w, sgu_b=sgu_b,
        q_gain=q_norm_gain.reshape(1, HEAD_DIM), k_gain=k_norm_gain.reshape(1, HEAD_DIM), sink=sink,
        gate_bias=gate_bias, w_g=g_merge.astype(BF16),
        w_br_a=w_br[:GLA_WIDTH].astype(BF16),
        w_br_b=w_br[GLA_WIDTH:GLA_WIDTH + SGU_WIDTH].astype(BF16),
        w_br_c=w_br[GLA_WIDTH + SGU_WIDTH:].astype(BF16),
        w_out=w_out.astype(BF16))


def _layer(x, p):
    batch, seq, _ = x.shape
    t = batch * seq
    x2 = x.reshape(t, D_MODEL)
    xn = _pre_norm(x2, p["norm_gain"])

    tm = min(PROJ_TM, t)
    blocks_per_seq = seq // tm if seq >= tm else 1
    vec = pl.BlockSpec((1, HEAD_DIM), lambda i, j: (0, 0))
    tab = pl.BlockSpec((tm, HEAD_DIM), lambda i, j: (i % blocks_per_seq, 0))
    tables = _rope_tables(seq)
    if seq < tm:
        tables = tuple(jnp.tile(tb, (tm // seq, 1)) for tb in tables)

    h_a = _proj(_proj_a_kernel, xn, p["w_a"], name="proj_gla")
    h_b = _proj(_proj_b_kernel, xn, p["w_b"], name="proj_sgu")
    h_c = _proj(_proj_c_kernel, xn, p["w_c"], extra=(p["q_gain"], p["k_gain"], *tables),
                extra_specs=(vec, vec, tab, tab, tab), name="proj_attn")
    log_a = _gla_log_decay(xn, p["w_lr"], p["up2"], p["bias2"])

    o_a = _gla(h_a, log_a, p["gla_norm_gain"], batch, seq)
    o_b = _sgu(h_b, p["sgu_ln_gain"], p["sgu_ln_bias"], p["sgu_w"], p["sgu_b"])
    o_c = _attention(h_c, p["sink"], batch, seq)

    merged = _merge(xn, o_a, o_b, o_c, p["w_g"], p["w_br_a"], p["w_br_b"], p["w_br_c"], p["gate_bias"])
    return _out_proj(merged, p["w_out"], x2).reshape(batch, seq, D_MODEL)


def kernel(x_prompt, x_sample, norm_gain, w_in, gla_gate_up, gla_gate_bias, gla_norm_gain, sgu_ln_gain,
           sgu_ln_bias, sgu_w, sgu_b, q_norm_gain, k_norm_gain, sink, gate_bias, w_br, w_out):
    y_prompt, y_sample = x_prompt, x_sample
    for l in range(norm_gain.shape[0]):
        p = _prepare_layer(norm_gain[l], w_in[l], gla_gate_up[l], gla_gate_bias[l], gla_norm_gain[l],
                           sgu_ln_gain[l], sgu_ln_bias[l], sgu_w[l], sgu_b[l], q_norm_gain[l],
                           k_norm_gain[l], sink[l], gate_bias[l], w_br[l], w_out[l])
        y_prompt = _layer(y_prompt, p)
        y_sample = _layer(y_sample, p)
    return (y_prompt, y_sample)
```

```python
import functools
import math

import jax
import jax.numpy as jnp
from jax import lax
from jax.experimental import pallas as pl
from jax.experimental.pallas import tpu as pltpu

F32 = jnp.float32
BF16 = jnp.bfloat16

D_MODEL = 4096
HEAD_DIM = 128
GLA_WIDTH = 1536
GLA_DV = 256
GLA_HEADS = 6
GLA_DK = 128
GLA_QK_WIDTH = 768
GLA_GATE_RANK = 16
GLA_GATE_TAU = 16.0
SGU_WIDTH = 1024
SGU_CHUNK = 128
SGU_GROUPS = 8
ATT_WIDTH = 1536
ATT_Q_HEADS = 12
ATT_KV_HEADS = 4
ATT_GROUP = 3
ATT_KV_WIDTH = 512
WINDOW = 128
ROPE_THETA = 500000.0
ROPE_DIMS = 32
N_BRANCH = 3
NORM_EPS = 1e-6
LN_EPS = 1e-5

VMEM_LIMIT_BYTES = 56 * 1024 * 1024
LANES = 128
SUBLANES = 8

NORM_ROWS = 256
PROJ_TM = 1024
PROJ_TN = 512
PROJ_SUB_ROWS = 256
GLA_ROWS = 256
GLA_HEADS_PER_STEP = 6
GLA_CHUNK = 64
GLA_FAST_MAX_CHUNK_DECAY = 40.0
SGU_ROWS = 512
MERGE_TM = 512
MERGE_TN = 512
OUT_TM = 1024
OUT_TN = 1024
assert ATT_KV_WIDTH == PROJ_TN
ATTN_K_TILE = ATT_WIDTH // PROJ_TN
ATTN_V_TILE = ATTN_K_TILE + 1
COL_GLA = 0
COL_SGU = 2 * GLA_QK_WIDTH + 2 * GLA_WIDTH
COL_ATTN = COL_SGU + 3 * SGU_WIDTH
COL_MERGE = COL_ATTN + 2 * ATT_WIDTH + 2 * ATT_KV_WIDTH


def _params(*sem):
    return pltpu.CompilerParams(dimension_semantics=sem, vmem_limit_bytes=VMEM_LIMIT_BYTES)


def _silu(x):
    return x * jax.nn.sigmoid(x)


def _gelu_tanh(x):
    c = math.sqrt(2.0 / math.pi)
    return x * (0.5 * (1.0 + jnp.tanh(c * (x + 0.044715 * (x * x * x)))))


def _norm_kernel(x_ref, g_ref, o_ref):
    x = x_ref[...]
    ms = jnp.mean(x * x, axis=-1, keepdims=True)
    o_ref[...] = (x * lax.rsqrt(ms + NORM_EPS) * g_ref[...]).astype(BF16)


def _pre_norm(x2, gain):
    t = x2.shape[0]
    return pl.pallas_call(
        _norm_kernel,
        grid=(t // NORM_ROWS,),
        in_specs=[pl.BlockSpec((NORM_ROWS, D_MODEL), lambda i: (i, 0)),
                  pl.BlockSpec((1, D_MODEL), lambda i: (0, 0))],
        out_specs=pl.BlockSpec((NORM_ROWS, D_MODEL), lambda i: (i, 0)),
        out_shape=jax.ShapeDtypeStruct((t, D_MODEL), BF16),
        compiler_params=_params("parallel"),
        name="pre_norm",
    )(x2, gain.reshape(1, D_MODEL))


def _dot_epilogue(x_ref, w_ref, o_ref, epilogue):
    w = w_ref[...]
    for r in range(x_ref.shape[0] // PROJ_SUB_ROWS):
        rows = pl.ds(r * PROJ_SUB_ROWS, PROJ_SUB_ROWS)
        acc = jnp.dot(x_ref[rows, :], w, preferred_element_type=F32)
        o_ref[rows, :] = epilogue(acc, rows).astype(BF16)


def _proj_a_kernel(x_ref, w_ref, o_ref):
    j = pl.program_id(1)
    n_raw = (2 * GLA_QK_WIDTH + GLA_WIDTH) // PROJ_TN

    @pl.when(j < n_raw)
    def _():
        _dot_epilogue(x_ref, w_ref, o_ref, lambda acc, rows: acc)

    @pl.when(j >= n_raw)
    def _():
        _dot_epilogue(x_ref, w_ref, o_ref, lambda acc, rows: _silu(acc))


def _proj_b_kernel(x_ref, w_ref, o_ref):
    j = pl.program_id(1)
    n_gelu = (2 * SGU_WIDTH) // PROJ_TN

    @pl.when(j < n_gelu)
    def _():
        _dot_epilogue(x_ref, w_ref, o_ref, lambda acc, rows: _gelu_tanh(acc))

    @pl.when(j >= n_gelu)
    def _():
        _dot_epilogue(x_ref, w_ref, o_ref, lambda acc, rows: _silu(acc))


def _head_norm_rope(acc, gain, cos_t, sin_lo, sin_hi):
    outs = []
    for h in range(PROJ_TN // HEAD_DIM):
        xh = acc[:, h * HEAD_DIM:(h + 1) * HEAD_DIM]
        ms = jnp.mean(xh * xh, axis=-1, keepdims=True)
        y = xh * lax.rsqrt(ms + NORM_EPS) * gain
        half = ROPE_DIMS // 2
        up = pltpu.roll(y, HEAD_DIM - half, axis=1)
        dn = pltpu.roll(y, half, axis=1)
        outs.append(y * cos_t + up * sin_lo + dn * sin_hi)
    return jnp.concatenate(outs, axis=1)


def _proj_c_kernel(x_ref, w_ref, qg_ref, kg_ref, cos_ref, slo_ref, shi_ref, o_ref):
    j = pl.program_id(1)

    def norm_rope(gain_ref):
        return lambda acc, rows: _head_norm_rope(acc, gain_ref[...], cos_ref[rows, :], slo_ref[rows, :],
                                                 shi_ref[rows, :])

    @pl.when(j < ATTN_K_TILE)
    def _():
        _dot_epilogue(x_ref, w_ref, o_ref, norm_rope(qg_ref))

    @pl.when(j == ATTN_K_TILE)
    def _():
        _dot_epilogue(x_ref, w_ref, o_ref, norm_rope(kg_ref))

    @pl.when(j == ATTN_V_TILE)
    def _():
        _dot_epilogue(x_ref, w_ref, o_ref, lambda acc, rows: acc)

    @pl.when(j > ATTN_V_TILE)
    def _():
        _dot_epilogue(x_ref, w_ref, o_ref, lambda acc, rows: _silu(acc))


def _attn_out_tile(j):
    n_gate = ATT_WIDTH // PROJ_TN
    return jnp.where(j < ATTN_K_TILE, j, jnp.where(j <= ATTN_V_TILE, j + n_gate, j - 2))


def _proj(kernel, xn, w, col0, n, out_tile=lambda j: j, extra=(), extra_specs=(), name="proj"):
    t = xn.shape[0]
    tm = min(PROJ_TM, t)
    tile0 = col0 // PROJ_TN
    return pl.pallas_call(
        kernel,
        grid=(t // tm, n // PROJ_TN),
        in_specs=[pl.BlockSpec((tm, D_MODEL), lambda i, j: (i, 0)),
                  pl.BlockSpec((D_MODEL, PROJ_TN), lambda i, j: (0, tile0 + j)),
                  *extra_specs],
        out_specs=pl.BlockSpec((tm, PROJ_TN), lambda i, j: (i, out_tile(j))),
        out_shape=jax.ShapeDtypeStruct((t, n), BF16),
        compiler_params=_params("parallel", "arbitrary"),
        name=name,
    )(xn, w, *extra)


def _rope_tables(seq):
    inv_freq = ROPE_THETA ** (-jnp.arange(0, ROPE_DIMS, 2, dtype=F32) / ROPE_DIMS)
    ang = jnp.arange(seq, dtype=F32)[:, None] * inv_freq[None, :]
    cos, sin = jnp.cos(ang), jnp.sin(ang)
    half = ROPE_DIMS // 2
    ones = jnp.ones((seq, HEAD_DIM - ROPE_DIMS), F32)
    zeros = jnp.zeros((seq, HEAD_DIM - ROPE_DIMS), F32)
    zh = jnp.zeros((seq, half), F32)
    cos_t = jnp.concatenate([cos, cos, ones], axis=1)
    sin_lo = jnp.concatenate([-sin, zh, zeros], axis=1)
    sin_hi = jnp.concatenate([zh, sin, zeros], axis=1)
    return cos_t, sin_lo, sin_hi


def _decay_kernel(x_ref, w_ref, up_ref, b_ref, o_ref, mx_ref):
    lr = jnp.dot(x_ref[...], w_ref[...], preferred_element_type=F32)
    z = jnp.dot(lr.astype(BF16), up_ref[...], preferred_element_type=F32) + b_ref[...]
    log_a = (jnp.minimum(z, 0.0) - jnp.log1p(jnp.exp(-jnp.abs(z)))) * (1.0 / GLA_GATE_TAU)
    o_ref[...] = log_a
    rows = []
    for r in range(SUBLANES):
        if r < log_a.shape[0] // GLA_ROWS:
            blk = jnp.abs(log_a[r * GLA_ROWS:(r + 1) * GLA_ROWS, :])
            m = jnp.max(jnp.max(blk, axis=1, keepdims=True), axis=0, keepdims=True)
            rows.append(jnp.broadcast_to(m, (1, LANES)))
        else:
            rows.append(jnp.zeros((1, LANES), F32))
    mx_ref[0] = jnp.concatenate(rows, axis=0)


def _gla_log_decay(xn, w_lr, up2, bias2):
    t = xn.shape[0]
    tm = min(PROJ_TM, t)
    n = 2 * GLA_QK_WIDTH
    log_a, mx = pl.pallas_call(
        _decay_kernel,
        grid=(t // tm,),
        in_specs=[pl.BlockSpec((tm, D_MODEL), lambda i: (i, 0)),
                  pl.BlockSpec((D_MODEL, LANES), lambda i: (0, 0)),
                  pl.BlockSpec((LANES, n), lambda i: (0, 0)),
                  pl.BlockSpec((1, n), lambda i: (0, 0))],
        out_specs=[pl.BlockSpec((tm, n), lambda i: (i, 0)),
                   pl.BlockSpec((1, SUBLANES, LANES), lambda i: (i, 0, 0))],
        out_shape=[jax.ShapeDtypeStruct((t, n), F32),
                   jax.ShapeDtypeStruct((t // tm, SUBLANES, LANES), F32)],
        compiler_params=_params("parallel"),
        name="gla_log_decay",
    )(xn, w_lr, up2, bias2)
    step_max = mx[:, :tm // GLA_ROWS, 0].reshape(t // GLA_ROWS)
    exact = jnp.logical_not(step_max <= GLA_FAST_MAX_CHUNK_DECAY / GLA_CHUNK).astype(jnp.int32)
    return log_a, exact


def _gla_head_rows(q, k, v, b, state, mask, reverse, exact_scratch):
    n_chunks = q.shape[0] // GLA_CHUNK
    chunk = lambda x, c: x[c * GLA_CHUNK:(c + 1) * GLA_CHUNK]
    edge = 0 if reverse else GLA_CHUNK - 1
    b_tot = [chunk(b, c)[edge:edge + 1, :] for c in range(n_chunks)]
    dec_row = [jnp.exp(bt) for bt in b_tot]
    qb = (q * jnp.exp(b)).astype(BF16)

    if exact_scratch is None:
        k_inv = k * jnp.exp(-b)
        s = lax.dot_general(qb, k_inv.astype(BF16), (((1,), (1,)), ((), ())), preferred_element_type=F32)
        o = jnp.dot(jnp.where(mask, s, 0.0).astype(BF16), v, preferred_element_type=F32)
        kb = [chunk(k_inv, c) * dec_row[c] for c in range(n_chunks)]
    else:
        b_scr, k_scr = exact_scratch
        col = lax.broadcasted_iota(jnp.int32, (GLA_CHUNK, GLA_CHUNK), 1)
        o_chunks, kb = [], []
        for c in range(n_chunks):
            qc, bc = chunk(q, c), chunk(b, c)
            b_scr[...] = bc
            k_scr[...] = chunk(k, c)

            def column(j, acc, qc=qc, bc=bc):
                decay = jnp.exp(jnp.minimum(bc - b_scr[pl.ds(j, 1), :], 0.0))
                w = qc * decay * k_scr[pl.ds(j, 1), :]
                return jnp.where(col == j, jnp.sum(w, axis=-1, keepdims=True), acc)

            s = lax.fori_loop(0, GLA_CHUNK, column, jnp.zeros((GLA_CHUNK, GLA_CHUNK), F32))
            lo = c * GLA_CHUNK
            p = jnp.where(mask[lo:lo + GLA_CHUNK, lo:lo + GLA_CHUNK], s, 0.0)
            o_chunks.append(jnp.dot(p.astype(BF16), chunk(v, c), preferred_element_type=F32))
            kb.append(chunk(k, c) * jnp.exp(b_tot[c] - bc))
        o = jnp.concatenate(o_chunks, axis=0)

    o_state = [None] * n_chunks
    for c in (range(n_chunks - 1, -1, -1) if reverse else range(n_chunks)):
        o_state[c] = jnp.dot(chunk(qb, c), state.astype(BF16), preferred_element_type=F32)
        upd = lax.dot_general(kb[c].astype(BF16), chunk(v, c), (((0,), (0,)), ((), ())),
                              preferred_element_type=F32)
        dec = jnp.transpose(jnp.broadcast_to(dec_row[c], (GLA_DK, GLA_DK)))
        state = jnp.concatenate([dec] * (GLA_DV // GLA_DK), axis=1) * state + upd
    return o + jnp.concatenate(o_state, axis=0), state


def _gla_scan_rows(q_ref, k_ref, v_ref, g_ref, state_ref, reverse, emit, exact_scratch):
    row = lax.broadcasted_iota(jnp.int32, (GLA_ROWS, GLA_ROWS), 0)
    col = lax.broadcasted_iota(jnp.int32, (GLA_ROWS, GLA_ROWS), 1)
    shift = GLA_CHUNK.bit_length() - 1
    mask = jnp.logical_and((col >= row) if reverse else (col <= row),
                           lax.shift_right_logical(row, shift) == lax.shift_right_logical(col, shift))
    tri_b = mask.astype(BF16)
    g_all = g_ref[...]
    g_hi = g_all.astype(BF16)
    g_lo = (g_all - g_hi.astype(F32)).astype(BF16)
    b_all = (jnp.dot(tri_b, g_hi, preferred_element_type=F32)
             + jnp.dot(tri_b, g_lo, preferred_element_type=F32))
    for h in range(GLA_HEADS_PER_STEP):
        kc = pl.ds(h * GLA_DK, GLA_DK)
        vc = pl.ds(h * GLA_DV, GLA_DV)
        q = q_ref[:, kc].astype(F32) * (GLA_DK ** -0.5)
        k = k_ref[:, kc].astype(F32)
        o, new_state = _gla_head_rows(q, k, v_ref[:, vc], b_all[:, h * GLA_DK:(h + 1) * GLA_DK], state_ref[h],
                                      mask, reverse, exact_scratch)
        state_ref[h] = new_state
        emit(vc, o)


def _gla_scan_step(exact_ref, row_block, refs, state_ref, b_scr, k_scr, reverse, emit):
    @pl.when(pl.program_id(2) == 0)
    def _():
        state_ref[...] = jnp.zeros_like(state_ref)

    exact = exact_ref[row_block]

    @pl.when(exact == 0)
    def _():
        _gla_scan_rows(*refs, state_ref, reverse, emit, None)

    @pl.when(exact != 0)
    def _():
        _gla_scan_rows(*refs, state_ref, reverse, emit, (b_scr, k_scr))


def _gla_fwd_kernel(exact_ref, q_ref, k_ref, v_ref, g_ref, o_ref, state_ref, b_scr, k_scr, *, nb):
    def emit(cols, o):
        o_ref[:, cols] = o

    row_block = pl.program_id(0) * nb + pl.program_id(2)
    _gla_scan_step(exact_ref, row_block, (q_ref, k_ref, v_ref, g_ref), state_ref, b_scr, k_scr, False, emit)


def _gla_bwd_kernel(exact_ref, q_ref, k_ref, v_ref, g_ref, fwd_ref, gate_ref, gain_ref, o_ref,
                    state_ref, b_scr, k_scr, *, nb):
    def emit(cols, o):
        tot = fwd_ref[:, cols] + o
        ms = jnp.mean(tot * tot, axis=-1, keepdims=True)
        y = tot * lax.rsqrt(ms + NORM_EPS) * gain_ref[...]
        o_ref[:, cols] = (y * gate_ref[:, cols].astype(F32)).astype(BF16)

    row_block = pl.program_id(0) * nb + (nb - 1 - pl.program_id(2))
    _gla_scan_step(exact_ref, row_block, (q_ref, k_ref, v_ref, g_ref), state_ref, b_scr, k_scr, True, emit)


def _gla(h_a, log_a, exact, gain, batch, seq):
    t = batch * seq
    nb = seq // GLA_ROWS
    hp = GLA_HEADS_PER_STEP
    qk_w, v_w = hp * GLA_DK, hp * GLA_DV
    k_off = GLA_QK_WIDTH // qk_w
    v_off = 2 * GLA_QK_WIDTH // v_w
    gate_off = (2 * GLA_QK_WIDTH + GLA_WIDTH) // v_w
    dir_off = GLA_QK_WIDTH // qk_w
    grid = (batch, GLA_HEADS // hp, nb)
    scratch = [pltpu.VMEM((hp, GLA_DK, GLA_DV), F32),
               pltpu.VMEM((GLA_CHUNK, GLA_DK), F32), pltpu.VMEM((GLA_CHUNK, GLA_DK), F32)]

    def specs(rowmap, direction):
        return [pl.BlockSpec((GLA_ROWS, qk_w), lambda b, h, n, e: (rowmap(b, n), h)),
                pl.BlockSpec((GLA_ROWS, qk_w), lambda b, h, n, e: (rowmap(b, n), k_off + h)),
                pl.BlockSpec((GLA_ROWS, v_w), lambda b, h, n, e: (rowmap(b, n), v_off + h)),
                pl.BlockSpec((GLA_ROWS, qk_w), lambda b, h, n, e: (rowmap(b, n), direction * dir_off + h))]

    fmap = lambda b, n: b * nb + n
    rmap = lambda b, n: b * nb + (nb - 1 - n)
    fwd = pl.pallas_call(
        functools.partial(_gla_fwd_kernel, nb=nb),
        grid_spec=pltpu.PrefetchScalarGridSpec(
            num_scalar_prefetch=1,
            grid=grid,
            in_specs=specs(fmap, 0),
            out_specs=pl.BlockSpec((GLA_ROWS, v_w), lambda b, h, n, e: (fmap(b, n), h)),
            scratch_shapes=scratch),
        out_shape=jax.ShapeDtypeStruct((t, GLA_WIDTH), F32),
        compiler_params=_params("parallel", "parallel", "arbitrary"),
        name="gla_fwd",
    )(exact, h_a, h_a, h_a, log_a)
    return pl.pallas_call(
        functools.partial(_gla_bwd_kernel, nb=nb),
        grid_spec=pltpu.PrefetchScalarGridSpec(
            num_scalar_prefetch=1,
            grid=grid,
            in_specs=specs(rmap, 1) + [
                pl.BlockSpec((GLA_ROWS, v_w), lambda b, h, n, e: (rmap(b, n), h)),
                pl.BlockSpec((GLA_ROWS, v_w), lambda b, h, n, e: (rmap(b, n), gate_off + h)),
                pl.BlockSpec((1, GLA_DV), lambda b, h, n, e: (0, 0))],
            out_specs=pl.BlockSpec((GLA_ROWS, v_w), lambda b, h, n, e: (rmap(b, n), h)),
            scratch_shapes=scratch),
        out_shape=jax.ShapeDtypeStruct((t, GLA_WIDTH), BF16),
        compiler_params=_params("parallel", "parallel", "arbitrary"),
        name="gla_bwd",
    )(exact, h_a, h_a, h_a, log_a, fwd, h_a, gain.reshape(1, GLA_DV))


def _sgu_kernel(u_ref, v_ref, gate_ref, lng_ref, lnb_ref, w_ref, bt_ref, o_ref):
    for c in range(SGU_ROWS // SGU_CHUNK):
        rows = pl.ds(c * SGU_CHUNK, SGU_CHUNK)
        x = v_ref[rows, :].astype(F32)
        mu = jnp.mean(x, axis=-1, keepdims=True)
        xc = x - mu
        var = jnp.mean(xc * xc, axis=-1, keepdims=True)
        y = (xc * lax.rsqrt(var + LN_EPS) * lng_ref[...] + lnb_ref[...]).astype(BF16)
        for g in range(SGU_GROUPS):
            cols = pl.ds(g * LANES, LANES)
            mixed = jnp.dot(w_ref[g], y[:, g * LANES:(g + 1) * LANES], preferred_element_type=F32)
            mixed = mixed + bt_ref[:, g:g + 1]
            o_ref[rows, cols] = (u_ref[rows, cols].astype(F32) * mixed
                                 * gate_ref[rows, cols].astype(F32)).astype(BF16)


def _sgu(h_b, ln_gain, ln_bias, w, b):
    t = h_b.shape[0]
    return pl.pallas_call(
        _sgu_kernel,
        grid=(t // SGU_ROWS,),
        in_specs=[pl.BlockSpec((SGU_ROWS, SGU_WIDTH), lambda i: (i, 0)),
                  pl.BlockSpec((SGU_ROWS, SGU_WIDTH), lambda i: (i, 1)),
                  pl.BlockSpec((SGU_ROWS, SGU_WIDTH), lambda i: (i, 2)),
                  pl.BlockSpec((1, SGU_WIDTH), lambda i: (0, 0)),
                  pl.BlockSpec((1, SGU_WIDTH), lambda i: (0, 0)),
                  pl.BlockSpec((SGU_GROUPS, SGU_CHUNK, SGU_CHUNK), lambda i: (0, 0, 0)),
                  pl.BlockSpec((SGU_CHUNK, SGU_GROUPS), lambda i: (0, 0))],
        out_specs=pl.BlockSpec((SGU_ROWS, SGU_WIDTH), lambda i: (i, 0)),
        out_shape=jax.ShapeDtypeStruct((t, SGU_WIDTH), BF16),
        compiler_params=_params("parallel"),
        name="sgu",
    )(h_b, h_b, h_b, ln_gain.reshape(1, SGU_WIDTH), ln_bias.reshape(1, SGU_WIDTH),
      w.astype(BF16), jnp.transpose(b))


def _attn_kernel(sink_ref, q_ref, gate_ref, kp_ref, kc_ref, kn_ref, vp_ref, vc_ref, vn_ref, o_ref, *, nb):
    n = pl.program_id(1)
    qpos = lax.broadcasted_iota(jnp.int32, (WINDOW, 3 * WINDOW), 0)
    kpos = lax.broadcasted_iota(jnp.int32, (WINDOW, 3 * WINDOW), 1) - WINDOW
    valid = jnp.abs(kpos - qpos) <= WINDOW
    valid = jnp.logical_and(valid, jnp.logical_or(kpos >= 0, n > 0))
    valid = jnp.logical_and(valid, jnp.logical_or(kpos < WINDOW, n < nb - 1))
    valid = jnp.concatenate([valid] * ATT_GROUP, axis=0)
    for hk in range(ATT_KV_HEADS):
        kcols = pl.ds(hk * HEAD_DIM, HEAD_DIM)
        q = jnp.concatenate([q_ref[:, pl.ds((hk * ATT_GROUP + g) * HEAD_DIM, HEAD_DIM)]
                             for g in range(ATT_GROUP)], axis=0)
        k = jnp.concatenate([kp_ref[:, kcols], kc_ref[:, kcols], kn_ref[:, kcols]], axis=0)
        v = jnp.concatenate([vp_ref[:, kcols], vc_ref[:, kcols], vn_ref[:, kcols]], axis=0)
        s = lax.dot_general(q, k, (((1,), (1,)), ((), ())), preferred_element_type=F32) * (HEAD_DIM ** -0.5)
        s = jnp.where(valid, s, -jnp.inf)
        sink = jnp.concatenate([jnp.full((WINDOW, 1), sink_ref[hk * ATT_GROUP + g], F32)
                                for g in range(ATT_GROUP)], axis=0)
        m = jnp.maximum(jnp.max(s, axis=-1, keepdims=True), sink)
        e = jnp.exp(s - m)
        denom = jnp.sum(e, axis=-1, keepdims=True) + jnp.exp(sink - m)
        o = jnp.dot(e.astype(BF16), v, preferred_element_type=F32) / denom
        for g in range(ATT_GROUP):
            cols = pl.ds((hk * ATT_GROUP + g) * HEAD_DIM, HEAD_DIM)
            o_ref[:, cols] = (o[g * WINDOW:(g + 1) * WINDOW, :] * gate_ref[:, cols].astype(F32)).astype(BF16)


def _attention(h_c, sink, batch, seq):
    t = batch * seq
    nb = seq // WINDOW
    k_blk = 2 * ATT_WIDTH // ATT_KV_WIDTH
    v_blk = k_blk + 1
    cur = lambda b, n: b * nb + n
    prv = lambda b, n: b * nb + jnp.maximum(n - 1, 0)
    nxt = lambda b, n: b * nb + jnp.minimum(n + 1, nb - 1)
    kv = lambda rowmap, blk: pl.BlockSpec((WINDOW, ATT_KV_WIDTH), lambda b, n: (rowmap(b, n), blk))
    return pl.pallas_call(
        functools.partial(_attn_kernel, nb=nb),
        grid=(batch, nb),
        in_specs=[pl.BlockSpec(memory_space=pltpu.SMEM),
                  pl.BlockSpec((WINDOW, ATT_WIDTH), lambda b, n: (cur(b, n), 0)),
                  pl.BlockSpec((WINDOW, ATT_WIDTH), lambda b, n: (cur(b, n), 1)),
                  kv(prv, k_blk), kv(cur, k_blk), kv(nxt, k_blk),
                  kv(prv, v_blk), kv(cur, v_blk), kv(nxt, v_blk)],
        out_specs=pl.BlockSpec((WINDOW, ATT_WIDTH), lambda b, n: (cur(b, n), 0)),
        out_shape=jax.ShapeDtypeStruct((t, ATT_WIDTH), BF16),
        compiler_params=_params("parallel", "parallel"),
        name="window_attention",
    )(sink, h_c, h_c, h_c, h_c, h_c, h_c, h_c, h_c)


def _merge_kernel(x_ref, oa_ref, ob_ref, oc_ref, g0_ref, g1_ref, g2_ref, wa_ref, wb_ref, wc_ref, bias_ref, o_ref):
    x = x_ref[...]
    acc = None
    for i, (g_ref, o_in, w_ref) in enumerate(((g0_ref, oa_ref, wa_ref), (g1_ref, ob_ref, wb_ref),
                                               (g2_ref, oc_ref, wc_ref))):
        gate = jax.nn.sigmoid(jnp.dot(x, g_ref[...], preferred_element_type=F32) + bias_ref[i:i + 1, :])
        term = gate * jnp.dot(o_in[...], w_ref[...], preferred_element_type=F32)
        acc = term if acc is None else acc + term
    o_ref[...] = acc.astype(BF16)


def _merge(xn, o_a, o_b, o_c, w_g, w_br_a, w_br_b, w_br_c, gate_bias):
    t = xn.shape[0]
    tm = min(MERGE_TM, t)
    nj = D_MODEL // MERGE_TN
    row = lambda width: pl.BlockSpec((tm, width), lambda i, j: (i, 0))
    g0 = COL_MERGE // MERGE_TN
    gcol = lambda br: pl.BlockSpec((D_MODEL, MERGE_TN), lambda i, j: (0, g0 + br * nj + j))
    wcol = lambda width: pl.BlockSpec((width, MERGE_TN), lambda i, j: (0, j))
    return pl.pallas_call(
        _merge_kernel,
        grid=(t // tm, nj),
        in_specs=[row(D_MODEL), row(GLA_WIDTH), row(SGU_WIDTH), row(ATT_WIDTH),
                  gcol(0), gcol(1), gcol(2),
                  wcol(GLA_WIDTH), wcol(SGU_WIDTH), wcol(ATT_WIDTH),
                  pl.BlockSpec((N_BRANCH, MERGE_TN), lambda i, j: (0, j))],
        out_specs=pl.BlockSpec((tm, MERGE_TN), lambda i, j: (i, j)),
        out_shape=jax.ShapeDtypeStruct((t, D_MODEL), BF16),
        compiler_params=_params("parallel", "arbitrary"),
        name="gated_merge",
    )(xn, o_a, o_b, o_c, w_g, w_g, w_g, w_br_a, w_br_b, w_br_c, gate_bias)


def _out_kernel(m_ref, w_ref, x_ref, o_ref):
    o_ref[...] = x_ref[...] + jnp.dot(m_ref[...], w_ref[...], preferred_element_type=F32)


def _out_proj(merged, w_out, x2):
    t = x2.shape[0]
    tm = min(OUT_TM, t)
    return pl.pallas_call(
        _out_kernel,
        grid=(t // tm, D_MODEL // OUT_TN),
        in_specs=[pl.BlockSpec((tm, D_MODEL), lambda i, j: (i, 0)),
                  pl.BlockSpec((D_MODEL, OUT_TN), lambda i, j: (0, j)),
                  pl.BlockSpec((tm, OUT_TN), lambda i, j: (i, j))],
        out_specs=pl.BlockSpec((tm, OUT_TN), lambda i, j: (i, j)),
        out_shape=jax.ShapeDtypeStruct((t, D_MODEL), F32),
        compiler_params=_params("parallel", "arbitrary"),
        name="out_proj",
    )(merged, w_out, x2)


def _prepare_layer(norm_gain, w_in, gla_gate_up, gla_gate_bias, gla_norm_gain, sgu_ln_gain, sgu_ln_bias,
                   sgu_w, sgu_b, q_norm_gain, k_norm_gain, sink, gate_bias, w_br, w_out):
    lr0 = 2 * GLA_QK_WIDTH + GLA_WIDTH
    lr1 = lr0 + 2 * GLA_GATE_RANK
    w_main = jnp.concatenate([w_in[:, :lr0], w_in[:, lr1:]], axis=1).astype(BF16)
    w_lr = jnp.pad(w_in[:, lr0:lr1], ((0, 0), (0, LANES - 2 * GLA_GATE_RANK))).astype(BF16)
    up2 = jnp.zeros((LANES, 2 * GLA_QK_WIDTH), F32)
    up2 = up2.at[:GLA_GATE_RANK, :GLA_QK_WIDTH].set(gla_gate_up[0])
    up2 = up2.at[GLA_GATE_RANK:2 * GLA_GATE_RANK, GLA_QK_WIDTH:].set(gla_gate_up[1])
    return dict(
        norm_gain=norm_gain, w_main=w_main, w_lr=w_lr, up2=up2.astype(BF16),
        bias2=gla_gate_bias.reshape(1, 2 * GLA_QK_WIDTH), gla_norm_gain=gla_norm_gain,
        sgu_ln_gain=sgu_ln_gain, sgu_ln_bias=sgu_ln_bias, sgu_w=sgu_w, sgu_b=sgu_b,
        q_gain=q_norm_gain.reshape(1, HEAD_DIM), k_gain=k_norm_gain.reshape(1, HEAD_DIM), sink=sink,
        gate_bias=gate_bias,
        w_br_a=w_br[:GLA_WIDTH].astype(BF16),
        w_br_b=w_br[GLA_WIDTH:GLA_WIDTH + SGU_WIDTH].astype(BF16),
        w_br_c=w_br[GLA_WIDTH + SGU_WIDTH:].astype(BF16),
        w_out=w_out.astype(BF16))


def _layer(x, p):
    batch, seq, _ = x.shape
    t = batch * seq
    x2 = x.reshape(t, D_MODEL)
    xn = _pre_norm(x2, p["norm_gain"])

    tm = min(PROJ_TM, t)
    blocks_per_seq = seq // tm if seq >= tm else 1
    vec = pl.BlockSpec((1, HEAD_DIM), lambda i, j: (0, 0))
    tab = pl.BlockSpec((tm, HEAD_DIM), lambda i, j: (i % blocks_per_seq, 0))
    tables = _rope_tables(seq)
    if seq < tm:
        tables = tuple(jnp.tile(tb, (tm // seq, 1)) for tb in tables)

    w = p["w_main"]
    h_a = _proj(_proj_a_kernel, xn, w, COL_GLA, COL_SGU - COL_GLA, name="proj_gla")
    h_b = _proj(_proj_b_kernel, xn, w, COL_SGU, COL_ATTN - COL_SGU, name="proj_sgu")
    h_c = _proj(_proj_c_kernel, xn, w, COL_ATTN, COL_MERGE - COL_ATTN, out_tile=_attn_out_tile,
                extra=(p["q_gain"], p["k_gain"], *tables), extra_specs=(vec, vec, tab, tab, tab),
                name="proj_attn")
    log_a, gla_exact = _gla_log_decay(xn, p["w_lr"], p["up2"], p["bias2"])

    o_a = _gla(h_a, log_a, gla_exact, p["gla_norm_gain"], batch, seq)
    o_b = _sgu(h_b, p["sgu_ln_gain"], p["sgu_ln_bias"], p["sgu_w"], p["sgu_b"])
    o_c = _attention(h_c, p["sink"], batch, seq)

    merged = _merge(xn, o_a, o_b, o_c, w, p["w_br_a"], p["w_br_b"], p["w_br_c"], p["gate_bias"])
    return _out_proj(merged, p["w_out"], x2).reshape(batch, seq, D_MODEL)


def kernel(x_prompt, x_sample, norm_gain, w_in, gla_gate_up, gla_gate_bias, gla_norm_gain, sgu_ln_gain,
           sgu_ln_bias, sgu_w, sgu_b, q_norm_gain, k_norm_gain, sink, gate_bias, w_br, w_out):
    y_prompt, y_sample = x_prompt, x_sample
    for l in range(norm_gain.shape[0]):
        p = _prepare_layer(norm_gain[l], w_in[l], gla_gate_up[l], gla_gate_bias[l], gla_norm_gain[l],
                           sgu_ln_gain[l], sgu_ln_bias[l], sgu_w[l], sgu_b[l], q_norm_gain[l],
                           k_norm_gain[l], sink[l], gate_bias[l], w_br[l], w_out[l])
        y_prompt = _layer(y_prompt, p)
        y_sample = _layer(y_sample, p)
    return (y_prompt, y_sample)
```

```python
import functools
import math

import jax
import jax.numpy as jnp
from jax import lax
from jax.experimental import pallas as pl
from jax.experimental.pallas import tpu as pltpu

F32 = jnp.float32
BF16 = jnp.bfloat16

D_MODEL = 4096
HEAD_DIM = 128
GLA_WIDTH = 1536
GLA_DV = 256
GLA_HEADS = 6
GLA_DK = 128
GLA_QK_WIDTH = 768
GLA_GATE_RANK = 16
GLA_GATE_TAU = 16.0
SGU_WIDTH = 1024
SGU_CHUNK = 128
SGU_GROUPS = 8
ATT_WIDTH = 1536
ATT_Q_HEADS = 12
ATT_KV_HEADS = 4
ATT_GROUP = 3
ATT_KV_WIDTH = 512
WINDOW = 128
ROPE_THETA = 500000.0
ROPE_DIMS = 32
N_BRANCH = 3
NORM_EPS = 1e-6
LN_EPS = 1e-5

VMEM_LIMIT_BYTES = 56 * 1024 * 1024
LANES = 128
SUBLANES = 8

NORM_ROWS = 256
PROJ_TM = 1024
PROJ_TN = 512
PROJ_RAW_TN = 1024
PROJ_SUB_ROWS = 256
GLA_ROWS = 256
GLA_HEADS_PER_STEP = 6
GLA_CHUNK = 64
GLA_FAST_MAX_CHUNK_DECAY = 40.0
SGU_ROWS = 512
ATTN_BLOCKS = 4
ATTN_LOOKAHEAD = 2
MERGE_TM = 512
MERGE_TN = 512
OUT_TM = 1024
OUT_TN = 1024

KIND_SILU, KIND_GELU, KIND_QNORM, KIND_KNORM, KIND_RAW = range(5)
_REST_KIND = ([KIND_SILU] * 3 + [KIND_GELU] * 4 + [KIND_SILU] * 2 + [KIND_QNORM] * 3 + [KIND_KNORM, KIND_RAW]
              + [KIND_SILU] * 3)
_REST_OUT_TILE = [6, 7, 8, 0, 1, 2, 3, 4, 5, 9, 10, 11, 15, 16, 12, 13, 14]
REST_ACT_WIDTH = len(_REST_KIND) * PROJ_TN
H_SGU_U, H_SGU_V, H_SGU_GATE = 0, 1, 2
H_GLA_GATE, H_ATT_Q, H_ATT_GATE = 2, 3, 4
H_ATT_K, H_ATT_V = 15, 16


def _params(*sem):
    return pltpu.CompilerParams(dimension_semantics=sem, vmem_limit_bytes=VMEM_LIMIT_BYTES)


def _silu(x):
    return x * jax.nn.sigmoid(x)


def _gelu_tanh(x):
    c = math.sqrt(2.0 / math.pi)
    return x * (0.5 * (1.0 + jnp.tanh(c * (x + 0.044715 * (x * x * x)))))


def _norm_kernel(x_ref, g_ref, o_ref):
    x = x_ref[...]
    ms = jnp.mean(x * x, axis=-1, keepdims=True)
    o_ref[...] = (x * lax.rsqrt(ms + NORM_EPS) * g_ref[...]).astype(BF16)


def _pre_norm(x2, gain):
    t = x2.shape[0]
    return pl.pallas_call(
        _norm_kernel,
        grid=(t // NORM_ROWS,),
        in_specs=[pl.BlockSpec((NORM_ROWS, D_MODEL), lambda i: (i, 0)),
                  pl.BlockSpec((1, D_MODEL), lambda i: (0, 0))],
        out_specs=pl.BlockSpec((NORM_ROWS, D_MODEL), lambda i: (i, 0)),
        out_shape=jax.ShapeDtypeStruct((t, D_MODEL), BF16),
        compiler_params=_params("parallel"),
        name="pre_norm",
    )(x2, gain.reshape(1, D_MODEL))


def _dot_epilogue(x_ref, w_ref, o_ref, epilogue):
    w = w_ref[...]
    for r in range(x_ref.shape[0] // PROJ_SUB_ROWS):
        rows = pl.ds(r * PROJ_SUB_ROWS, PROJ_SUB_ROWS)
        acc = jnp.dot(x_ref[rows, :], w, preferred_element_type=F32)
        o_ref[rows, :] = epilogue(acc, rows).astype(BF16)


def _proj_raw_kernel(x_ref, w_ref, o_ref):
    _dot_epilogue(x_ref, w_ref, o_ref, lambda acc, rows: acc)


def _proj_raw(xn, w, name):
    t = xn.shape[0]
    n = w.shape[1]
    tm = min(PROJ_TM, t)
    return pl.pallas_call(
        _proj_raw_kernel,
        grid=(t // tm, n // PROJ_RAW_TN),
        in_specs=[pl.BlockSpec((tm, D_MODEL), lambda i, j: (i, 0)),
                  pl.BlockSpec((D_MODEL, PROJ_RAW_TN), lambda i, j: (0, j))],
        out_specs=pl.BlockSpec((tm, PROJ_RAW_TN), lambda i, j: (i, j)),
        out_shape=jax.ShapeDtypeStruct((t, n), BF16),
        compiler_params=_params("parallel", "arbitrary"),
        name=name,
    )(xn, w)


def _head_norm_rope(acc, gain, cos_t, sin_lo, sin_hi):
    outs = []
    for h in range(PROJ_TN // HEAD_DIM):
        xh = acc[:, h * HEAD_DIM:(h + 1) * HEAD_DIM]
        ms = jnp.mean(xh * xh, axis=-1, keepdims=True)
        y = xh * lax.rsqrt(ms + NORM_EPS) * gain
        half = ROPE_DIMS // 2
        up = pltpu.roll(y, HEAD_DIM - half, axis=1)
        dn = pltpu.roll(y, half, axis=1)
        outs.append(y * cos_t + up * sin_lo + dn * sin_hi)
    return jnp.concatenate(outs, axis=1)


def _proj_rest_kernel(kind_ref, tile_ref, x_ref, w_ref, qg_ref, kg_ref, cos_ref, slo_ref, shi_ref, o_ref):
    del tile_ref
    kind = kind_ref[pl.program_id(1)]

    def norm_rope(gain_ref):
        return lambda acc, rows: _head_norm_rope(acc, gain_ref[...], cos_ref[rows, :], slo_ref[rows, :],
                                                 shi_ref[rows, :])

    epilogues = {
        KIND_SILU: lambda acc, rows: _silu(acc),
        KIND_GELU: lambda acc, rows: _gelu_tanh(acc),
        KIND_QNORM: norm_rope(qg_ref),
        KIND_KNORM: norm_rope(kg_ref),
        KIND_RAW: lambda acc, rows: acc,
    }
    for k, epilogue in epilogues.items():
        pl.when(kind == k)(functools.partial(_dot_epilogue, x_ref, w_ref, o_ref, epilogue))


def _proj_rest(xn, w_rest, q_gain, k_gain, tables, blocks_per_seq):
    t = xn.shape[0]
    tm = min(PROJ_TM, t)
    vec = pl.BlockSpec((1, HEAD_DIM), lambda i, j, kind, tile: (0, 0))
    tab = pl.BlockSpec((tm, HEAD_DIM), lambda i, j, kind, tile: (i % blocks_per_seq, 0))
    return pl.pallas_call(
        _proj_rest_kernel,
        grid_spec=pltpu.PrefetchScalarGridSpec(
            num_scalar_prefetch=2,
            grid=(t // tm, len(_REST_KIND)),
            in_specs=[pl.BlockSpec((tm, D_MODEL), lambda i, j, kind, tile: (i, 0)),
                      pl.BlockSpec((D_MODEL, PROJ_TN), lambda i, j, kind, tile: (0, j)),
                      vec, vec, tab, tab, tab],
            out_specs=pl.BlockSpec((tm, PROJ_TN), lambda i, j, kind, tile: (i, tile[j]))),
        out_shape=jax.ShapeDtypeStruct((t, REST_ACT_WIDTH), BF16),
        compiler_params=_params("parallel", "arbitrary"),
        name="proj_rest",
    )(jnp.asarray(_REST_KIND, jnp.int32), jnp.asarray(_REST_OUT_TILE, jnp.int32),
      xn, w_rest, q_gain, k_gain, *tables)


def _rope_tables(seq):
    inv_freq = ROPE_THETA ** (-jnp.arange(0, ROPE_DIMS, 2, dtype=F32) / ROPE_DIMS)
    ang = jnp.arange(seq, dtype=F32)[:, None] * inv_freq[None, :]
    cos, sin = jnp.cos(ang), jnp.sin(ang)
    half = ROPE_DIMS // 2
    ones = jnp.ones((seq, HEAD_DIM - ROPE_DIMS), F32)
    zeros = jnp.zeros((seq, HEAD_DIM - ROPE_DIMS), F32)
    zh = jnp.zeros((seq, half), F32)
    cos_t = jnp.concatenate([cos, cos, ones], axis=1)
    sin_lo = jnp.concatenate([-sin, zh, zeros], axis=1)
    sin_hi = jnp.concatenate([zh, sin, zeros], axis=1)
    return cos_t, sin_lo, sin_hi


def _decay_kernel(x_ref, w_ref, up_ref, b_ref, o_ref, mx_ref):
    lr = jnp.dot(x_ref[...], w_ref[...], preferred_element_type=F32)
    z = jnp.dot(lr.astype(BF16), up_ref[...], preferred_element_type=F32) + b_ref[...]
    log_a = (jnp.minimum(z, 0.0) - jnp.log1p(jnp.exp(-jnp.abs(z)))) * (1.0 / GLA_GATE_TAU)
    o_ref[...] = log_a
    rows = []
    for r in range(SUBLANES):
        if r < log_a.shape[0] // GLA_ROWS:
            blk = jnp.abs(log_a[r * GLA_ROWS:(r + 1) * GLA_ROWS, :])
            m = jnp.max(jnp.max(blk, axis=1, keepdims=True), axis=0, keepdims=True)
            rows.append(jnp.broadcast_to(m, (1, LANES)))
        else:
            rows.append(jnp.zeros((1, LANES), F32))
    mx_ref[0] = jnp.concatenate(rows, axis=0)


def _gla_log_decay(xn, w_lr, up2, bias2):
    t = xn.shape[0]
    tm = min(PROJ_TM, t)
    n = 2 * GLA_QK_WIDTH
    log_a, mx = pl.pallas_call(
        _decay_kernel,
        grid=(t // tm,),
        in_specs=[pl.BlockSpec((tm, D_MODEL), lambda i: (i, 0)),
                  pl.BlockSpec((D_MODEL, LANES), lambda i: (0, 0)),
                  pl.BlockSpec((LANES, n), lambda i: (0, 0)),
                  pl.BlockSpec((1, n), lambda i: (0, 0))],
        out_specs=[pl.BlockSpec((tm, n), lambda i: (i, 0)),
                   pl.BlockSpec((1, SUBLANES, LANES), lambda i: (i, 0, 0))],
        out_shape=[jax.ShapeDtypeStruct((t, n), F32),
                   jax.ShapeDtypeStruct((t // tm, SUBLANES, LANES), F32)],
        compiler_params=_params("parallel"),
        name="gla_log_decay",
    )(xn, w_lr, up2, bias2)
    step_max = mx[:, :tm // GLA_ROWS, 0].reshape(t // GLA_ROWS)
    exact = jnp.logical_not(step_max <= GLA_FAST_MAX_CHUNK_DECAY / GLA_CHUNK).astype(jnp.int32)
    return log_a, exact


def _gla_head_rows(q, k, v, b, state, mask, reverse, exact_scratch):
    n_chunks = q.shape[0] // GLA_CHUNK
    chunk = lambda x, c: x[c * GLA_CHUNK:(c + 1) * GLA_CHUNK]
    edge = 0 if reverse else GLA_CHUNK - 1
    b_tot = [chunk(b, c)[edge:edge + 1, :] for c in range(n_chunks)]
    dec_row = [jnp.exp(bt) for bt in b_tot]
    qb = (q * jnp.exp(b)).astype(BF16)

    if exact_scratch is None:
        k_inv = k * jnp.exp(-b)
        s = lax.dot_general(qb, k_inv.astype(BF16), (((1,), (1,)), ((), ())), preferred_element_type=F32)
        o = jnp.dot(jnp.where(mask, s, 0.0).astype(BF16), v, preferred_element_type=F32)
        kb = [chunk(k_inv, c) * dec_row[c] for c in range(n_chunks)]
    else:
        b_scr, k_scr = exact_scratch
        col = lax.broadcasted_iota(jnp.int32, (GLA_CHUNK, GLA_CHUNK), 1)
        o_chunks, kb = [], []
        for c in range(n_chunks):
            qc, bc = chunk(q, c), chunk(b, c)
            b_scr[...] = bc
            k_scr[...] = chunk(k, c)

            def column(j, acc, qc=qc, bc=bc):
                decay = jnp.exp(jnp.minimum(bc - b_scr[pl.ds(j, 1), :], 0.0))
                w = qc * decay * k_scr[pl.ds(j, 1), :]
                return jnp.where(col == j, jnp.sum(w, axis=-1, keepdims=True), acc)

            s = lax.fori_loop(0, GLA_CHUNK, column, jnp.zeros((GLA_CHUNK, GLA_CHUNK), F32))
            lo = c * GLA_CHUNK
            p = jnp.where(mask[lo:lo + GLA_CHUNK, lo:lo + GLA_CHUNK], s, 0.0)
            o_chunks.append(jnp.dot(p.astype(BF16), chunk(v, c), preferred_element_type=F32))
            kb.append(chunk(k, c) * jnp.exp(b_tot[c] - bc))
        o = jnp.concatenate(o_chunks, axis=0)

    o_state = [None] * n_chunks
    for c in (range(n_chunks - 1, -1, -1) if reverse else range(n_chunks)):
        o_state[c] = jnp.dot(chunk(qb, c), state.astype(BF16), preferred_element_type=F32)
        upd = lax.dot_general(kb[c].astype(BF16), chunk(v, c), (((0,), (0,)), ((), ())),
                              preferred_element_type=F32)
        dec = jnp.transpose(jnp.broadcast_to(dec_row[c], (GLA_DK, GLA_DK)))
        state = jnp.concatenate([dec] * (GLA_DV // GLA_DK), axis=1) * state + upd
    return o + jnp.concatenate(o_state, axis=0), state


def _gla_scan_rows(q_ref, k_ref, v_ref, g_ref, state_ref, reverse, emit, exact_scratch):
    row = lax.broadcasted_iota(jnp.int32, (GLA_ROWS, GLA_ROWS), 0)
    col = lax.broadcasted_iota(jnp.int32, (GLA_ROWS, GLA_ROWS), 1)
    shift = GLA_CHUNK.bit_length() - 1
    mask = jnp.logical_and((col >= row) if reverse else (col <= row),
                           lax.shift_right_logical(row, shift) == lax.shift_right_logical(col, shift))
    tri_b = mask.astype(BF16)
    g_all = g_ref[...]
    g_hi = g_all.astype(BF16)
    g_lo = (g_all - g_hi.astype(F32)).astype(BF16)
    b_all = (jnp.dot(tri_b, g_hi, preferred_element_type=F32)
             + jnp.dot(tri_b, g_lo, preferred_element_type=F32))
    for h in range(GLA_HEADS_PER_STEP):
        kc = pl.ds(h * GLA_DK, GLA_DK)
        vc = pl.ds(h * GLA_DV, GLA_DV)
        q = q_ref[:, kc].astype(F32) * (GLA_DK ** -0.5)
        k = k_ref[:, kc].astype(F32)
        o, new_state = _gla_head_rows(q, k, v_ref[:, vc], b_all[:, h * GLA_DK:(h + 1) * GLA_DK], state_ref[h],
                                      mask, reverse, exact_scratch)
        state_ref[h] = new_state
        emit(vc, o)


def _gla_scan_step(exact_ref, row_block, refs, state_ref, b_scr, k_scr, reverse, emit):
    @pl.when(pl.program_id(2) == 0)
    def _():
        state_ref[...] = jnp.zeros_like(state_ref)

    exact = exact_ref[row_block]

    @pl.when(exact == 0)
    def _():
        _gla_scan_rows(*refs, state_ref, reverse, emit, None)

    @pl.when(exact != 0)
    def _():
        _gla_scan_rows(*refs, state_ref, reverse, emit, (b_scr, k_scr))


def _gla_fwd_kernel(exact_ref, q_ref, k_ref, v_ref, g_ref, o_ref, state_ref, b_scr, k_scr, *, nb):
    def emit(cols, o):
        o_ref[:, cols] = o

    row_block = pl.program_id(0) * nb + pl.program_id(2)
    _gla_scan_step(exact_ref, row_block, (q_ref, k_ref, v_ref, g_ref), state_ref, b_scr, k_scr, False, emit)


def _gla_bwd_kernel(exact_ref, q_ref, k_ref, v_ref, g_ref, fwd_ref, gate_ref, gain_ref, o_ref,
                    state_ref, b_scr, k_scr, *, nb):
    def emit(cols, o):
        tot = fwd_ref[:, cols] + o
        ms = jnp.mean(tot * tot, axis=-1, keepdims=True)
        y = tot * lax.rsqrt(ms + NORM_EPS) * gain_ref[...]
        o_ref[:, cols] = (y * gate_ref[:, cols].astype(F32)).astype(BF16)

    row_block = pl.program_id(0) * nb + (nb - 1 - pl.program_id(2))
    _gla_scan_step(exact_ref, row_block, (q_ref, k_ref, v_ref, g_ref), state_ref, b_scr, k_scr, True, emit)


def _gla(h_qkv, h_rest, log_a, exact, gain, batch, seq):
    t = batch * seq
    nb = seq // GLA_ROWS
    hp = GLA_HEADS_PER_STEP
    qk_w, v_w = hp * GLA_DK, hp * GLA_DV
    k_off = GLA_QK_WIDTH // qk_w
    v_off = 2 * GLA_QK_WIDTH // v_w
    gate_off = H_GLA_GATE * GLA_WIDTH // v_w
    dir_off = GLA_QK_WIDTH // qk_w
    grid = (batch, GLA_HEADS // hp, nb)
    scratch = [pltpu.VMEM((hp, GLA_DK, GLA_DV), F32),
               pltpu.VMEM((GLA_CHUNK, GLA_DK), F32), pltpu.VMEM((GLA_CHUNK, GLA_DK), F32)]

    def specs(rowmap, direction):
        return [pl.BlockSpec((GLA_ROWS, qk_w), lambda b, h, n, e: (rowmap(b, n), h)),
                pl.BlockSpec((GLA_ROWS, qk_w), lambda b, h, n, e: (rowmap(b, n), k_off + h)),
                pl.BlockSpec((GLA_ROWS, v_w), lambda b, h, n, e: (rowmap(b, n), v_off + h)),
                pl.BlockSpec((GLA_ROWS, qk_w), lambda b, h, n, e: (rowmap(b, n), direction * dir_off + h))]

    fmap = lambda b, n: b * nb + n
    rmap = lambda b, n: b * nb + (nb - 1 - n)
    fwd = pl.pallas_call(
        functools.partial(_gla_fwd_kernel, nb=nb),
        grid_spec=pltpu.PrefetchScalarGridSpec(
            num_scalar_prefetch=1,
            grid=grid,
            in_specs=specs(fmap, 0),
            out_specs=pl.BlockSpec((GLA_ROWS, v_w), lambda b, h, n, e: (fmap(b, n), h)),
            scratch_shapes=scratch),
        out_shape=jax.ShapeDtypeStruct((t, GLA_WIDTH), F32),
        compiler_params=_params("parallel", "parallel", "arbitrary"),
        name="gla_fwd",
    )(exact, h_qkv, h_qkv, h_qkv, log_a)
    return pl.pallas_call(
        functools.partial(_gla_bwd_kernel, nb=nb),
        grid_spec=pltpu.PrefetchScalarGridSpec(
            num_scalar_prefetch=1,
            grid=grid,
            in_specs=specs(rmap, 1) + [
                pl.BlockSpec((GLA_ROWS, v_w), lambda b, h, n, e: (rmap(b, n), h)),
                pl.BlockSpec((GLA_ROWS, v_w), lambda b, h, n, e: (rmap(b, n), gate_off + h)),
                pl.BlockSpec((1, GLA_DV), lambda b, h, n, e: (0, 0))],
            out_specs=pl.BlockSpec((GLA_ROWS, v_w), lambda b, h, n, e: (rmap(b, n), h)),
            scratch_shapes=scratch),
        out_shape=jax.ShapeDtypeStruct((t, GLA_WIDTH), BF16),
        compiler_params=_params("parallel", "parallel", "arbitrary"),
        name="gla_bwd",
    )(exact, h_qkv, h_qkv, h_qkv, log_a, fwd, h_rest, gain.reshape(1, GLA_DV))


def _sgu_kernel(u_ref, v_ref, gate_ref, lng_ref, lnb_ref, w_ref, bt_ref, o_ref):
    for c in range(SGU_ROWS // SGU_CHUNK):
        rows = pl.ds(c * SGU_CHUNK, SGU_CHUNK)
        x = v_ref[rows, :].astype(F32)
        mu = jnp.mean(x, axis=-1, keepdims=True)
        xc = x - mu
        var = jnp.mean(xc * xc, axis=-1, keepdims=True)
        y = (xc * lax.rsqrt(var + LN_EPS) * lng_ref[...] + lnb_ref[...]).astype(BF16)
        for g in range(SGU_GROUPS):
            cols = pl.ds(g * LANES, LANES)
            mixed = jnp.dot(w_ref[g], y[:, g * LANES:(g + 1) * LANES], preferred_element_type=F32)
            mixed = mixed + bt_ref[:, g:g + 1]
            o_ref[rows, cols] = (u_ref[rows, cols].astype(F32) * mixed
                                 * gate_ref[rows, cols].astype(F32)).astype(BF16)


def _sgu(h_rest, ln_gain, ln_bias, w, b):
    t = h_rest.shape[0]
    blk = lambda c: pl.BlockSpec((SGU_ROWS, SGU_WIDTH), lambda i: (i, c))
    return pl.pallas_call(
        _sgu_kernel,
        grid=(t // SGU_ROWS,),
        in_specs=[blk(H_SGU_U), blk(H_SGU_V), blk(H_SGU_GATE),
                  pl.BlockSpec((1, SGU_WIDTH), lambda i: (0, 0)),
                  pl.BlockSpec((1, SGU_WIDTH), lambda i: (0, 0)),
                  pl.BlockSpec((SGU_GROUPS, SGU_CHUNK, SGU_CHUNK), lambda i: (0, 0, 0)),
                  pl.BlockSpec((SGU_CHUNK, SGU_GROUPS), lambda i: (0, 0))],
        out_specs=pl.BlockSpec((SGU_ROWS, SGU_WIDTH), lambda i: (i, 0)),
        out_shape=jax.ShapeDtypeStruct((t, SGU_WIDTH), BF16),
        compiler_params=_params("parallel"),
        name="sgu",
    )(h_rest, h_rest, h_rest, ln_gain.reshape(1, SGU_WIDTH), ln_bias.reshape(1, SGU_WIDTH),
      w.astype(BF16), jnp.transpose(b))


def _attn_kernel(sink_ref, q_ref, gate_ref, kp_ref, kc_ref, kn_ref, vp_ref, vc_ref, vn_ref, o_ref, *, n_steps):
    n = pl.program_id(1)
    log2e = math.log2(math.e)
    qpos = lax.broadcasted_iota(jnp.int32, (WINDOW, 3 * WINDOW), 0)
    kpos = lax.broadcasted_iota(jnp.int32, (WINDOW, 3 * WINDOW), 1) - WINDOW
    band = jnp.abs(kpos - qpos) <= WINDOW
    full = slice(None)
    block = lambda c: pl.ds(c * WINDOW, WINDOW)

    items = []
    for a in range(ATTN_BLOCKS):
        prev = (kp_ref, vp_ref, full) if a == 0 else (kc_ref, vc_ref, block(a - 1))
        nxt = (kn_ref, vn_ref, full) if a == ATTN_BLOCKS - 1 else (kc_ref, vc_ref, block(a + 1))
        keys = [(prev[0], prev[2]), (kc_ref, block(a)), (nxt[0], nxt[2])]
        vals = [(prev[1], prev[2]), (vc_ref, block(a)), (nxt[1], nxt[2])]
        valid = band
        if a == 0:
            valid = jnp.logical_and(valid, jnp.logical_or(kpos >= 0, n > 0))
        if a == ATTN_BLOCKS - 1:
            valid = jnp.logical_and(valid, jnp.logical_or(kpos < WINDOW, n < n_steps - 1))
        valid = jnp.concatenate([valid] * ATT_GROUP, axis=0)
        items += [(block(a), keys, vals, valid, hk) for hk in range(ATT_KV_HEADS)]

    def scores(item):
        rows, keys, _, _, hk = item
        kcols = pl.ds(hk * HEAD_DIM, HEAD_DIM)
        q = jnp.concatenate([q_ref[rows, pl.ds((hk * ATT_GROUP + g) * HEAD_DIM, HEAD_DIM)]
                             for g in range(ATT_GROUP)], axis=0)
        k = jnp.concatenate([ref[r, kcols] for ref, r in keys], axis=0)
        return lax.dot_general(q, k, (((1,), (1,)), ((), ())), preferred_element_type=F32)

    def finish(item, s):
        rows, _, vals, valid, hk = item
        kcols = pl.ds(hk * HEAD_DIM, HEAD_DIM)
        v = jnp.concatenate([ref[r, kcols] for ref, r in vals], axis=0)
        s = jnp.where(valid, s * (HEAD_DIM ** -0.5 * log2e), -jnp.inf)
        sink = jnp.concatenate([jnp.full((WINDOW, 1), sink_ref[hk * ATT_GROUP + g] * log2e, F32)
                                for g in range(ATT_GROUP)], axis=0)
        m = jnp.maximum(jnp.max(s, axis=-1, keepdims=True), sink)
        e = jnp.exp2(s - m)
        denom = jnp.sum(e, axis=-1, keepdims=True) + jnp.exp2(sink - m)
        o = jnp.dot(e.astype(BF16), v, preferred_element_type=F32) / denom
        for g in range(ATT_GROUP):
            cols = pl.ds((hk * ATT_GROUP + g) * HEAD_DIM, HEAD_DIM)
            o_ref[rows, cols] = (o[g * WINDOW:(g + 1) * WINDOW, :] * gate_ref[rows, cols].astype(F32)).astype(BF16)

    pending = [scores(item) for item in items[:ATTN_LOOKAHEAD]]
    for i, item in enumerate(items):
        if i + ATTN_LOOKAHEAD < len(items):
            pending.append(scores(items[i + ATTN_LOOKAHEAD]))
        finish(item, pending.pop(0))


def _attention(h_rest, sink, batch, seq):
    t = batch * seq
    rows = ATTN_BLOCKS * WINDOW
    n_steps = seq // rows
    nb = seq // WINDOW
    cur = lambda b, n: b * n_steps + n
    prv = lambda b, n: b * nb + jnp.maximum(n * ATTN_BLOCKS - 1, 0)
    nxt = lambda b, n: b * nb + jnp.minimum((n + 1) * ATTN_BLOCKS, nb - 1)
    wide = lambda blk: pl.BlockSpec((rows, ATT_WIDTH), lambda b, n: (cur(b, n), blk))
    kv_cur = lambda blk: pl.BlockSpec((rows, ATT_KV_WIDTH), lambda b, n: (cur(b, n), blk))
    kv_edge = lambda rowmap, blk: pl.BlockSpec((WINDOW, ATT_KV_WIDTH), lambda b, n: (rowmap(b, n), blk))
    return pl.pallas_call(
        functools.partial(_attn_kernel, n_steps=n_steps),
        grid=(batch, n_steps),
        in_specs=[pl.BlockSpec(memory_space=pltpu.SMEM),
                  wide(H_ATT_Q), wide(H_ATT_GATE),
                  kv_edge(prv, H_ATT_K), kv_cur(H_ATT_K), kv_edge(nxt, H_ATT_K),
                  kv_edge(prv, H_ATT_V), kv_cur(H_ATT_V), kv_edge(nxt, H_ATT_V)],
        out_specs=pl.BlockSpec((rows, ATT_WIDTH), lambda b, n: (cur(b, n), 0)),
        out_shape=jax.ShapeDtypeStruct((t, ATT_WIDTH), BF16),
        compiler_params=_params("parallel", "parallel"),
        name="window_attention",
    )(sink, h_rest, h_rest, h_rest, h_rest, h_rest, h_rest, h_rest, h_rest)


def _merge_kernel(x_ref, oa_ref, ob_ref, oc_ref, g0_ref, g1_ref, g2_ref, wbr_ref, bias_ref, o_ref):
    x = x_ref[...]
    acc = None
    row0 = 0
    for i, (g_ref, o_in) in enumerate(((g0_ref, oa_ref), (g1_ref, ob_ref), (g2_ref, oc_ref))):
        width = o_in.shape[1]
        gate = jax.nn.sigmoid(jnp.dot(x, g_ref[...], preferred_element_type=F32) + bias_ref[i:i + 1, :])
        term = gate * jnp.dot(o_in[...], wbr_ref[row0:row0 + width, :], preferred_element_type=F32)
        acc = term if acc is None else acc + term
        row0 += width
    o_ref[...] = acc.astype(BF16)


def _merge(xn, o_a, o_b, o_c, w_rest, w_br, gate_bias):
    t = xn.shape[0]
    tm = min(MERGE_TM, t)
    nj = D_MODEL // MERGE_TN
    g0 = REST_ACT_WIDTH // MERGE_TN
    row = lambda width: pl.BlockSpec((tm, width), lambda i, j: (i, 0))
    gcol = lambda br: pl.BlockSpec((D_MODEL, MERGE_TN), lambda i, j: (0, g0 + br * nj + j))
    return pl.pallas_call(
        _merge_kernel,
        grid=(t // tm, nj),
        in_specs=[row(D_MODEL), row(GLA_WIDTH), row(SGU_WIDTH), row(ATT_WIDTH),
                  gcol(0), gcol(1), gcol(2),
                  pl.BlockSpec((GLA_WIDTH + SGU_WIDTH + ATT_WIDTH, MERGE_TN), lambda i, j: (0, j)),
                  pl.BlockSpec((N_BRANCH, MERGE_TN), lambda i, j: (0, j))],
        out_specs=pl.BlockSpec((tm, MERGE_TN), lambda i, j: (i, j)),
        out_shape=jax.ShapeDtypeStruct((t, D_MODEL), BF16),
        compiler_params=_params("parallel", "arbitrary"),
        name="gated_merge",
    )(xn, o_a, o_b, o_c, w_rest, w_rest, w_rest, w_br, gate_bias)


def _out_kernel(m_ref, w_ref, x_ref, o_ref):
    o_ref[...] = x_ref[...] + jnp.dot(m_ref[...], w_ref[...], preferred_element_type=F32)


def _out_proj(merged, w_out, x2):
    t = x2.shape[0]
    tm = min(OUT_TM, t)
    return pl.pallas_call(
        _out_kernel,
        grid=(t // tm, D_MODEL // OUT_TN),
        in_specs=[pl.BlockSpec((tm, D_MODEL), lambda i, j: (i, 0)),
                  pl.BlockSpec((D_MODEL, OUT_TN), lambda i, j: (0, j)),
                  pl.BlockSpec((tm, OUT_TN), lambda i, j: (i, j))],
        out_specs=pl.BlockSpec((tm, OUT_TN), lambda i, j: (i, j)),
        out_shape=jax.ShapeDtypeStruct((t, D_MODEL), F32),
        compiler_params=_params("parallel", "arbitrary"),
        name="out_proj",
    )(merged, w_out, x2)


def _prepare_layer(norm_gain, w_in, gla_gate_up, gla_gate_bias, gla_norm_gain, sgu_ln_gain, sgu_ln_bias,
                   sgu_w, sgu_b, q_norm_gain, k_norm_gain, sink, gate_bias, w_br, w_out):
    lr0 = 2 * GLA_QK_WIDTH + GLA_WIDTH
    lr1 = lr0 + 2 * GLA_GATE_RANK
    w_lr = jnp.pad(w_in[:, lr0:lr1], ((0, 0), (0, LANES - 2 * GLA_GATE_RANK))).astype(BF16)
    up2 = jnp.zeros((LANES, 2 * GLA_QK_WIDTH), F32)
    up2 = up2.at[:GLA_GATE_RANK, :GLA_QK_WIDTH].set(gla_gate_up[0])
    up2 = up2.at[GLA_GATE_RANK:2 * GLA_GATE_RANK, GLA_QK_WIDTH:].set(gla_gate_up[1])
    return dict(
        norm_gain=norm_gain, w_qkv=w_in[:, :lr0].astype(BF16), w_rest=w_in[:, lr1:].astype(BF16),
        w_lr=w_lr, up2=up2.astype(BF16),
        bias2=gla_gate_bias.reshape(1, 2 * GLA_QK_WIDTH), gla_norm_gain=gla_norm_gain,
        sgu_ln_gain=sgu_ln_gain, sgu_ln_bias=sgu_ln_bias, sgu_w=sgu_w, sgu_b=sgu_b,
        q_gain=q_norm_gain.reshape(1, HEAD_DIM), k_gain=k_norm_gain.reshape(1, HEAD_DIM), sink=sink,
        gate_bias=gate_bias, w_br=w_br.astype(BF16), w_out=w_out.astype(BF16))


def _layer(x, p):
    batch, seq, _ = x.shape
    t = batch * seq
    x2 = x.reshape(t, D_MODEL)
    xn = _pre_norm(x2, p["norm_gain"])

    tm = min(PROJ_TM, t)
    blocks_per_seq = seq // tm if seq >= tm else 1
    tables = _rope_tables(seq)
    if seq < tm:
        tables = tuple(jnp.tile(tb, (tm // seq, 1)) for tb in tables)

    h_qkv = _proj_raw(xn, p["w_qkv"], "proj_gla_qkv")
    h_rest = _proj_rest(xn, p["w_rest"], p["q_gain"], p["k_gain"], tables, blocks_per_seq)
    log_a, gla_exact = _gla_log_decay(xn, p["w_lr"], p["up2"], p["bias2"])

    o_a = _gla(h_qkv, h_rest, log_a, gla_exact, p["gla_norm_gain"], batch, seq)
    o_b = _sgu(h_rest, p["sgu_ln_gain"], p["sgu_ln_bias"], p["sgu_w"], p["sgu_b"])
    o_c = _attention(h_rest, p["sink"], batch, seq)

    merged = _merge(xn, o_a, o_b, o_c, p["w_rest"], p["w_br"], p["gate_bias"])
    return _out_proj(merged, p["w_out"], x2).reshape(batch, seq, D_MODEL)


def kernel(x_prompt, x_sample, norm_gain, w_in, gla_gate_up, gla_gate_bias, gla_norm_gain, sgu_ln_gain,
           sgu_ln_bias, sgu_w, sgu_b, q_norm_gain, k_norm_gain, sink, gate_bias, w_br, w_out):
    y_prompt, y_sample = x_prompt, x_sample
    for l in range(norm_gain.shape[0]):
        p = _prepare_layer(norm_gain[l], w_in[l], gla_gate_up[l], gla_gate_bias[l], gla_norm_gain[l],
                           sgu_ln_gain[l], sgu_ln_bias[l], sgu_w[l], sgu_b[l], q_norm_gain[l],
                           k_norm_gain[l], sink[l], gate_bias[l], w_br[l], w_out[l])
        y_prompt = _layer(y_prompt, p)
        y_sample = _layer(y_sample, p)
    return (y_prompt, y_sample)
```

```python
import functools
import math

import jax
import jax.numpy as jnp
from jax import lax
from jax.experimental import pallas as pl
from jax.experimental.pallas import tpu as pltpu

F32 = jnp.float32
BF16 = jnp.bfloat16

D_MODEL = 4096
HEAD_DIM = 128
GLA_WIDTH = 1536
GLA_DV = 256
GLA_HEADS = 6
GLA_DK = 128
GLA_QK_WIDTH = 768
GLA_GATE_RANK = 16
GLA_GATE_TAU = 16.0
SGU_WIDTH = 1024
SGU_CHUNK = 128
SGU_GROUPS = 8
ATT_WIDTH = 1536
ATT_Q_HEADS = 12
ATT_KV_HEADS = 4
ATT_GROUP = 3
ATT_KV_WIDTH = 512
WINDOW = 128
ROPE_THETA = 500000.0
ROPE_DIMS = 32
N_BRANCH = 3
NORM_EPS = 1e-6
LN_EPS = 1e-5

VMEM_LIMIT_BYTES = 56 * 1024 * 1024
LANES = 128
SUBLANES = 8

NORM_ROWS = 256
PROJ_TM = 1024
PROJ_REST_TM = 2048
PROJ_TN = 512
PROJ_RAW_TN = 1024
PROJ_SUB_ROWS = 256
GLA_ROWS = 256
GLA_HEADS_PER_STEP = 6
GLA_CHUNK = 64
GLA_FAST_MAX_DECAY = 40.0
GLA_PATH_WIDE, GLA_PATH_CHUNKED, GLA_PATH_EXACT = 0, 1, 2
SGU_ROWS = 512
ATTN_BLOCKS = 4
ATTN_LOOKAHEAD = 2
MERGE_TM = 512
MERGE_TN = 512
OUT_TM = 1024
OUT_TN = 1024
CAST_ROWS = 2048
CAST_TN = 512
CAST_SHIFT = 2 * GLA_GATE_RANK

KIND_SILU, KIND_GELU, KIND_QNORM, KIND_KNORM, KIND_RAW = range(5)
_REST_KIND = ([KIND_SILU] * 3 + [KIND_GELU] * 4 + [KIND_SILU] * 2 + [KIND_QNORM] * 3 + [KIND_KNORM, KIND_RAW]
              + [KIND_SILU] * 3)
_REST_OUT_TILE = [6, 7, 8, 0, 1, 2, 3, 4, 5, 9, 10, 11, 15, 16, 12, 13, 14]
REST_ACT_WIDTH = len(_REST_KIND) * PROJ_TN
H_SGU_U, H_SGU_V, H_SGU_GATE = 0, 1, 2
H_GLA_GATE, H_ATT_Q, H_ATT_GATE = 2, 3, 4
H_ATT_K, H_ATT_V = 15, 16


def _params(*sem):
    return pltpu.CompilerParams(dimension_semantics=sem, vmem_limit_bytes=VMEM_LIMIT_BYTES)


def _silu(x):
    return x * jax.nn.sigmoid(x)


def _gelu_tanh(x):
    c = math.sqrt(2.0 / math.pi)
    return x * (0.5 * (1.0 + jnp.tanh(c * (x + 0.044715 * (x * x * x)))))


def _norm_kernel(x_ref, g_ref, o_ref):
    x = x_ref[...]
    ms = jnp.mean(x * x, axis=-1, keepdims=True)
    o_ref[...] = (x * lax.rsqrt(ms + NORM_EPS) * g_ref[...]).astype(BF16)


def _pre_norm(x2, gain):
    t = x2.shape[0]
    return pl.pallas_call(
        _norm_kernel,
        grid=(t // NORM_ROWS,),
        in_specs=[pl.BlockSpec((NORM_ROWS, D_MODEL), lambda i: (i, 0)),
                  pl.BlockSpec((1, D_MODEL), lambda i: (0, 0))],
        out_specs=pl.BlockSpec((NORM_ROWS, D_MODEL), lambda i: (i, 0)),
        out_shape=jax.ShapeDtypeStruct((t, D_MODEL), BF16),
        compiler_params=_params("parallel"),
        name="pre_norm",
    )(x2, gain.reshape(1, D_MODEL))


def _dot_epilogue(x_ref, w_ref, o_ref, epilogue):
    w = w_ref[...]
    for r in range(x_ref.shape[0] // PROJ_SUB_ROWS):
        rows = pl.ds(r * PROJ_SUB_ROWS, PROJ_SUB_ROWS)
        acc = jnp.dot(x_ref[rows, :], w, preferred_element_type=F32)
        o_ref[rows, :] = epilogue(acc, rows).astype(BF16)


def _proj_raw_kernel(x_ref, w_ref, o_ref):
    _dot_epilogue(x_ref, w_ref, o_ref, lambda acc, rows: acc)


def _proj_raw(xn, w, name):
    t = xn.shape[0]
    n = w.shape[1]
    tm = min(PROJ_TM, t)
    return pl.pallas_call(
        _proj_raw_kernel,
        grid=(t // tm, n // PROJ_RAW_TN),
        in_specs=[pl.BlockSpec((tm, D_MODEL), lambda i, j: (i, 0)),
                  pl.BlockSpec((D_MODEL, PROJ_RAW_TN), lambda i, j: (0, j))],
        out_specs=pl.BlockSpec((tm, PROJ_RAW_TN), lambda i, j: (i, j)),
        out_shape=jax.ShapeDtypeStruct((t, n), BF16),
        compiler_params=_params("parallel", "arbitrary"),
        name=name,
    )(xn, w)


def _head_norm_rope(acc, gain, cos_t, sin_lo, sin_hi):
    outs = []
    for h in range(PROJ_TN // HEAD_DIM):
        xh = acc[:, h * HEAD_DIM:(h + 1) * HEAD_DIM]
        ms = jnp.mean(xh * xh, axis=-1, keepdims=True)
        y = xh * lax.rsqrt(ms + NORM_EPS) * gain
        half = ROPE_DIMS // 2
        up = pltpu.roll(y, HEAD_DIM - half, axis=1)
        dn = pltpu.roll(y, half, axis=1)
        outs.append(y * cos_t + up * sin_lo + dn * sin_hi)
    return jnp.concatenate(outs, axis=1)


def _proj_rest_kernel(kind_ref, tile_ref, x_ref, w_ref, qg_ref, kg_ref, cos_ref, slo_ref, shi_ref, o_ref):
    del tile_ref
    kind = kind_ref[pl.program_id(1)]

    def norm_rope(gain_ref):
        return lambda acc, rows: _head_norm_rope(acc, gain_ref[...], cos_ref[rows, :], slo_ref[rows, :],
                                                 shi_ref[rows, :])

    epilogues = {
        KIND_SILU: lambda acc, rows: _silu(acc),
        KIND_GELU: lambda acc, rows: _gelu_tanh(acc),
        KIND_QNORM: norm_rope(qg_ref),
        KIND_KNORM: norm_rope(kg_ref),
        KIND_RAW: lambda acc, rows: acc,
    }
    for k, epilogue in epilogues.items():
        pl.when(kind == k)(functools.partial(_dot_epilogue, x_ref, w_ref, o_ref, epilogue))


def _proj_rest(xn, w_rest, q_gain, k_gain, seq):
    t = xn.shape[0]
    tm = min(PROJ_REST_TM, t)
    blocks_per_seq = seq // tm if seq >= tm else 1
    tables = _rope_tables(seq)
    if seq < tm:
        tables = tuple(jnp.tile(tb, (tm // seq, 1)) for tb in tables)
    vec = pl.BlockSpec((1, HEAD_DIM), lambda i, j, kind, tile: (0, 0))
    tab = pl.BlockSpec((tm, HEAD_DIM), lambda i, j, kind, tile: (i % blocks_per_seq, 0))
    return pl.pallas_call(
        _proj_rest_kernel,
        grid_spec=pltpu.PrefetchScalarGridSpec(
            num_scalar_prefetch=2,
            grid=(t // tm, len(_REST_KIND)),
            in_specs=[pl.BlockSpec((tm, D_MODEL), lambda i, j, kind, tile: (i, 0)),
                      pl.BlockSpec((D_MODEL, PROJ_TN), lambda i, j, kind, tile: (0, j)),
                      vec, vec, tab, tab, tab],
            out_specs=pl.BlockSpec((tm, PROJ_TN), lambda i, j, kind, tile: (i, tile[j]))),
        out_shape=jax.ShapeDtypeStruct((t, REST_ACT_WIDTH), BF16),
        compiler_params=_params("parallel", "arbitrary"),
        name="proj_rest",
    )(jnp.asarray(_REST_KIND, jnp.int32), jnp.asarray(_REST_OUT_TILE, jnp.int32),
      xn, w_rest, q_gain, k_gain, *tables)


def _rope_tables(seq):
    inv_freq = ROPE_THETA ** (-jnp.arange(0, ROPE_DIMS, 2, dtype=F32) / ROPE_DIMS)
    ang = jnp.arange(seq, dtype=F32)[:, None] * inv_freq[None, :]
    cos, sin = jnp.cos(ang), jnp.sin(ang)
    half = ROPE_DIMS // 2
    ones = jnp.ones((seq, HEAD_DIM - ROPE_DIMS), F32)
    zeros = jnp.zeros((seq, HEAD_DIM - ROPE_DIMS), F32)
    zh = jnp.zeros((seq, half), F32)
    cos_t = jnp.concatenate([cos, cos, ones], axis=1)
    sin_lo = jnp.concatenate([-sin, zh, zeros], axis=1)
    sin_hi = jnp.concatenate([zh, sin, zeros], axis=1)
    return cos_t, sin_lo, sin_hi


def _decay_kernel(x_ref, w_ref, up_ref, b_ref, o_ref, mx_ref):
    lr = jnp.dot(x_ref[...], w_ref[...], preferred_element_type=F32)
    z = jnp.dot(lr.astype(BF16), up_ref[...], preferred_element_type=F32) + b_ref[...]
    log_a = (jnp.minimum(z, 0.0) - jnp.log(1.0 + jnp.exp(-jnp.abs(z)))) * (1.0 / GLA_GATE_TAU)
    o_ref[...] = log_a
    n_blocks = log_a.shape[0] // GLA_ROWS
    full_max = lambda x: jnp.broadcast_to(jnp.max(jnp.max(x, axis=1, keepdims=True), axis=0, keepdims=True),
                                          (1, LANES))
    blocks = [log_a[r * GLA_ROWS:(r + 1) * GLA_ROWS, :] for r in range(n_blocks)]
    zero = [jnp.zeros((1, LANES), F32)] * (SUBLANES // 2 - n_blocks)
    rows = ([full_max(-blk) for blk in blocks] + zero
            + [full_max(-jnp.sum(blk, axis=0, keepdims=True)) for blk in blocks] + zero)
    mx_ref[0] = jnp.concatenate(rows, axis=0)


def _gla_log_decay(xn, w_lr, up2, bias2):
    t = xn.shape[0]
    tm = min(PROJ_TM, t)
    n = 2 * GLA_QK_WIDTH
    log_a, mx = pl.pallas_call(
        _decay_kernel,
        grid=(t // tm,),
        in_specs=[pl.BlockSpec((tm, D_MODEL), lambda i: (i, 0)),
                  pl.BlockSpec((D_MODEL, LANES), lambda i: (0, 0)),
                  pl.BlockSpec((LANES, n), lambda i: (0, 0)),
                  pl.BlockSpec((1, n), lambda i: (0, 0))],
        out_specs=[pl.BlockSpec((tm, n), lambda i: (i, 0)),
                   pl.BlockSpec((1, SUBLANES, LANES), lambda i: (i, 0, 0))],
        out_shape=[jax.ShapeDtypeStruct((t, n), F32),
                   jax.ShapeDtypeStruct((t // tm, SUBLANES, LANES), F32)],
        compiler_params=_params("parallel"),
        name="gla_log_decay",
    )(xn, w_lr, up2, bias2)
    n_blocks = tm // GLA_ROWS
    step_max = mx[:, :n_blocks, 0].reshape(t // GLA_ROWS)
    block_max = mx[:, SUBLANES // 2:SUBLANES // 2 + n_blocks, 0].reshape(t // GLA_ROWS)
    path = jnp.where(block_max <= GLA_FAST_MAX_DECAY, GLA_PATH_WIDE,
                     jnp.where(step_max <= GLA_FAST_MAX_DECAY / GLA_CHUNK, GLA_PATH_CHUNKED, GLA_PATH_EXACT))
    return log_a, path.astype(jnp.int32)


def _gla_head_rows(q, k, v, b, state, mask, chunk_rows, reverse, exact_scratch):
    n_chunks = q.shape[0] // chunk_rows
    chunk = lambda x, c: x[c * chunk_rows:(c + 1) * chunk_rows]
    edge = 0 if reverse else chunk_rows - 1
    b_tot = [chunk(b, c)[edge:edge + 1, :] for c in range(n_chunks)]
    dec_row = [jnp.exp(bt) for bt in b_tot]
    qb = (q * jnp.exp(b)).astype(BF16)

    if exact_scratch is None:
        k_inv = k * jnp.exp(-b)
        s = lax.dot_general(qb, k_inv.astype(BF16), (((1,), (1,)), ((), ())), preferred_element_type=F32)
        o = jnp.dot(jnp.where(mask, s, 0.0).astype(BF16), v, preferred_element_type=F32)
        kb = [chunk(k_inv, c) * dec_row[c] for c in range(n_chunks)]
    else:
        b_scr, k_scr = exact_scratch
        col = lax.broadcasted_iota(jnp.int32, (chunk_rows, chunk_rows), 1)
        o_chunks, kb = [], []
        for c in range(n_chunks):
            qc, bc = chunk(q, c), chunk(b, c)
            b_scr[...] = bc
            k_scr[...] = chunk(k, c)

            def column(j, acc, qc=qc, bc=bc):
                decay = jnp.exp(jnp.minimum(bc - b_scr[pl.ds(j, 1), :], 0.0))
                w = qc * decay * k_scr[pl.ds(j, 1), :]
                return jnp.where(col == j, jnp.sum(w, axis=-1, keepdims=True), acc)

            s = lax.fori_loop(0, chunk_rows, column, jnp.zeros((chunk_rows, chunk_rows), F32))
            lo = c * chunk_rows
            p = jnp.where(mask[lo:lo + chunk_rows, lo:lo + chunk_rows], s, 0.0)
            o_chunks.append(jnp.dot(p.astype(BF16), chunk(v, c), preferred_element_type=F32))
            kb.append(chunk(k, c) * jnp.exp(b_tot[c] - bc))
        o = jnp.concatenate(o_chunks, axis=0)

    o_state = [None] * n_chunks
    for c in (range(n_chunks - 1, -1, -1) if reverse else range(n_chunks)):
        o_state[c] = jnp.dot(chunk(qb, c), state.astype(BF16), preferred_element_type=F32)
        upd = lax.dot_general(kb[c].astype(BF16), chunk(v, c), (((0,), (0,)), ((), ())),
                              preferred_element_type=F32)
        dec = jnp.transpose(jnp.broadcast_to(dec_row[c], (GLA_DK, GLA_DK)))
        state = jnp.concatenate([dec] * (GLA_DV // GLA_DK), axis=1) * state + upd
    return o + jnp.concatenate(o_state, axis=0), state


def _gla_scan_rows(q_ref, k_ref, v_ref, g_ref, state_ref, chunk_rows, reverse, emit, exact_scratch):
    row = lax.broadcasted_iota(jnp.int32, (GLA_ROWS, GLA_ROWS), 0)
    col = lax.broadcasted_iota(jnp.int32, (GLA_ROWS, GLA_ROWS), 1)
    shift = chunk_rows.bit_length() - 1
    mask = jnp.logical_and((col >= row) if reverse else (col <= row),
                           lax.shift_right_logical(row, shift) == lax.shift_right_logical(col, shift))
    tri_b = mask.astype(BF16)
    rest = g_ref[...]
    b_all = None
    for _ in range(3):
        term = rest.astype(BF16)
        rest = rest - term.astype(F32)
        part = jnp.dot(tri_b, term, preferred_element_type=F32)
        b_all = part if b_all is None else b_all + part
    for h in range(GLA_HEADS_PER_STEP):
        kc = pl.ds(h * GLA_DK, GLA_DK)
        vc = pl.ds(h * GLA_DV, GLA_DV)
        q = q_ref[:, kc].astype(F32) * (GLA_DK ** -0.5)
        k = k_ref[:, kc].astype(F32)
        o, new_state = _gla_head_rows(q, k, v_ref[:, vc], b_all[:, h * GLA_DK:(h + 1) * GLA_DK], state_ref[h],
                                      mask, chunk_rows, reverse, exact_scratch)
        state_ref[h] = new_state
        emit(vc, o)


def _gla_scan_step(path_ref, row_block, refs, state_ref, b_scr, k_scr, reverse, emit):
    @pl.when(pl.program_id(2) == 0)
    def _():
        state_ref[...] = jnp.zeros_like(state_ref)

    path = path_ref[row_block]

    @pl.when(path == GLA_PATH_WIDE)
    def _():
        _gla_scan_rows(*refs, state_ref, GLA_ROWS, reverse, emit, None)

    @pl.when(path == GLA_PATH_CHUNKED)
    def _():
        _gla_scan_rows(*refs, state_ref, GLA_CHUNK, reverse, emit, None)

    @pl.when(path == GLA_PATH_EXACT)
    def _():
        _gla_scan_rows(*refs, state_ref, GLA_CHUNK, reverse, emit, (b_scr, k_scr))


def _gla_fwd_kernel(exact_ref, q_ref, k_ref, v_ref, g_ref, o_ref, state_ref, b_scr, k_scr, *, nb):
    def emit(cols, o):
        o_ref[:, cols] = o

    row_block = pl.program_id(0) * nb + pl.program_id(2)
    _gla_scan_step(exact_ref, row_block, (q_ref, k_ref, v_ref, g_ref), state_ref, b_scr, k_scr, False, emit)


def _gla_bwd_kernel(exact_ref, q_ref, k_ref, v_ref, g_ref, fwd_ref, gate_ref, gain_ref, o_ref,
                    state_ref, b_scr, k_scr, *, nb):
    def emit(cols, o):
        tot = fwd_ref[:, cols] + o
        ms = jnp.mean(tot * tot, axis=-1, keepdims=True)
        y = tot * lax.rsqrt(ms + NORM_EPS) * gain_ref[...]
        o_ref[:, cols] = (y * gate_ref[:, cols].astype(F32)).astype(BF16)

    row_block = pl.program_id(0) * nb + (nb - 1 - pl.program_id(2))
    _gla_scan_step(exact_ref, row_block, (q_ref, k_ref, v_ref, g_ref), state_ref, b_scr, k_scr, True, emit)


def _gla(h_qkv, h_rest, log_a, exact, gain, batch, seq):
    t = batch * seq
    nb = seq // GLA_ROWS
    hp = GLA_HEADS_PER_STEP
    qk_w, v_w = hp * GLA_DK, hp * GLA_DV
    k_off = GLA_QK_WIDTH // qk_w
    v_off = 2 * GLA_QK_WIDTH // v_w
    gate_off = H_GLA_GATE * GLA_WIDTH // v_w
    dir_off = GLA_QK_WIDTH // qk_w
    grid = (batch, GLA_HEADS // hp, nb)
    scratch = [pltpu.VMEM((hp, GLA_DK, GLA_DV), F32),
               pltpu.VMEM((GLA_CHUNK, GLA_DK), F32), pltpu.VMEM((GLA_CHUNK, GLA_DK), F32)]

    def specs(rowmap, direction):
        return [pl.BlockSpec((GLA_ROWS, qk_w), lambda b, h, n, e: (rowmap(b, n), h)),
                pl.BlockSpec((GLA_ROWS, qk_w), lambda b, h, n, e: (rowmap(b, n), k_off + h)),
                pl.BlockSpec((GLA_ROWS, v_w), lambda b, h, n, e: (rowmap(b, n), v_off + h)),
                pl.BlockSpec((GLA_ROWS, qk_w), lambda b, h, n, e: (rowmap(b, n), direction * dir_off + h))]

    fmap = lambda b, n: b * nb + n
    rmap = lambda b, n: b * nb + (nb - 1 - n)
    fwd = pl.pallas_call(
        functools.partial(_gla_fwd_kernel, nb=nb),
        grid_spec=pltpu.PrefetchScalarGridSpec(
            num_scalar_prefetch=1,
            grid=grid,
            in_specs=specs(fmap, 0),
            out_specs=pl.BlockSpec((GLA_ROWS, v_w), lambda b, h, n, e: (fmap(b, n), h)),
            scratch_shapes=scratch),
        out_shape=jax.ShapeDtypeStruct((t, GLA_WIDTH), F32),
        compiler_params=_params("parallel", "parallel", "arbitrary"),
        name="gla_fwd",
    )(exact, h_qkv, h_qkv, h_qkv, log_a)
    return pl.pallas_call(
        functools.partial(_gla_bwd_kernel, nb=nb),
        grid_spec=pltpu.PrefetchScalarGridSpec(
            num_scalar_prefetch=1,
            grid=grid,
            in_specs=specs(rmap, 1) + [
                pl.BlockSpec((GLA_ROWS, v_w), lambda b, h, n, e: (rmap(b, n), h)),
                pl.BlockSpec((GLA_ROWS, v_w), lambda b, h, n, e: (rmap(b, n), gate_off + h)),
                pl.BlockSpec((1, GLA_DV), lambda b, h, n, e: (0, 0))],
            out_specs=pl.BlockSpec((GLA_ROWS, v_w), lambda b, h, n, e: (rmap(b, n), h)),
            scratch_shapes=scratch),
        out_shape=jax.ShapeDtypeStruct((t, GLA_WIDTH), BF16),
        compiler_params=_params("parallel", "parallel", "arbitrary"),
        name="gla_bwd",
    )(exact, h_qkv, h_qkv, h_qkv, log_a, fwd, h_rest, gain.reshape(1, GLA_DV))


def _sgu_kernel(u_ref, v_ref, gate_ref, lng_ref, lnb_ref, w_ref, bt_ref, o_ref):
    for c in range(SGU_ROWS // SGU_CHUNK):
        rows = pl.ds(c * SGU_CHUNK, SGU_CHUNK)
        x = v_ref[rows, :].astype(F32)
        mu = jnp.mean(x, axis=-1, keepdims=True)
        xc = x - mu
        var = jnp.mean(xc * xc, axis=-1, keepdims=True)
        y = (xc * lax.rsqrt(var + LN_EPS) * lng_ref[...] + lnb_ref[...]).astype(BF16)
        for g in range(SGU_GROUPS):
            cols = pl.ds(g * LANES, LANES)
            mixed = jnp.dot(w_ref[g], y[:, g * LANES:(g + 1) * LANES], preferred_element_type=F32)
            mixed = mixed + bt_ref[:, g:g + 1]
            o_ref[rows, cols] = (u_ref[rows, cols].astype(F32) * mixed
                                 * gate_ref[rows, cols].astype(F32)).astype(BF16)


def _sgu(h_rest, ln_gain, ln_bias, w, b):
    t = h_rest.shape[0]
    blk = lambda c: pl.BlockSpec((SGU_ROWS, SGU_WIDTH), lambda i: (i, c))
    return pl.pallas_call(
        _sgu_kernel,
        grid=(t // SGU_ROWS,),
        in_specs=[blk(H_SGU_U), blk(H_SGU_V), blk(H_SGU_GATE),
                  pl.BlockSpec((1, SGU_WIDTH), lambda i: (0, 0)),
                  pl.BlockSpec((1, SGU_WIDTH), lambda i: (0, 0)),
                  pl.BlockSpec((SGU_GROUPS, SGU_CHUNK, SGU_CHUNK), lambda i: (0, 0, 0)),
                  pl.BlockSpec((SGU_CHUNK, SGU_GROUPS), lambda i: (0, 0))],
        out_specs=pl.BlockSpec((SGU_ROWS, SGU_WIDTH), lambda i: (i, 0)),
        out_shape=jax.ShapeDtypeStruct((t, SGU_WIDTH), BF16),
        compiler_params=_params("parallel"),
        name="sgu",
    )(h_rest, h_rest, h_rest, ln_gain.reshape(1, SGU_WIDTH), ln_bias.reshape(1, SGU_WIDTH),
      w.astype(BF16), jnp.transpose(b))


def _attn_kernel(sink_ref, q_ref, gate_ref, kp_ref, kc_ref, kn_ref, vp_ref, vc_ref, vn_ref, o_ref, *, n_steps):
    n = pl.program_id(1)
    log2e = math.log2(math.e)
    qpos = lax.broadcasted_iota(jnp.int32, (WINDOW, 3 * WINDOW), 0)
    kpos = lax.broadcasted_iota(jnp.int32, (WINDOW, 3 * WINDOW), 1) - WINDOW
    band = jnp.abs(kpos - qpos) <= WINDOW
    full = slice(None)
    block = lambda c: pl.ds(c * WINDOW, WINDOW)

    items = []
    for a in range(ATTN_BLOCKS):
        prev = (kp_ref, vp_ref, full) if a == 0 else (kc_ref, vc_ref, block(a - 1))
        nxt = (kn_ref, vn_ref, full) if a == ATTN_BLOCKS - 1 else (kc_ref, vc_ref, block(a + 1))
        keys = [(prev[0], prev[2]), (kc_ref, block(a)), (nxt[0], nxt[2])]
        vals = [(prev[1], prev[2]), (vc_ref, block(a)), (nxt[1], nxt[2])]
        valid = band
        if a == 0:
            valid = jnp.logical_and(valid, jnp.logical_or(kpos >= 0, n > 0))
        if a == ATTN_BLOCKS - 1:
            valid = jnp.logical_and(valid, jnp.logical_or(kpos < WINDOW, n < n_steps - 1))
        valid = jnp.concatenate([valid] * ATT_GROUP, axis=0)
        items += [(block(a), keys, vals, valid, hk) for hk in range(ATT_KV_HEADS)]

    def scores(item):
        rows, keys, _, _, hk = item
        kcols = pl.ds(hk * HEAD_DIM, HEAD_DIM)
        q = jnp.concatenate([q_ref[rows, pl.ds((hk * ATT_GROUP + g) * HEAD_DIM, HEAD_DIM)]
                             for g in range(ATT_GROUP)], axis=0)
        k = jnp.concatenate([ref[r, kcols] for ref, r in keys], axis=0)
        return lax.dot_general(q, k, (((1,), (1,)), ((), ())), preferred_element_type=F32)

    def finish(item, s):
        rows, _, vals, valid, hk = item
        kcols = pl.ds(hk * HEAD_DIM, HEAD_DIM)
        v = jnp.concatenate([ref[r, kcols] for ref, r in vals], axis=0)
        s = jnp.where(valid, s * (HEAD_DIM ** -0.5 * log2e), -jnp.inf)
        sink = jnp.concatenate([jnp.full((WINDOW, 1), sink_ref[hk * ATT_GROUP + g] * log2e, F32)
                                for g in range(ATT_GROUP)], axis=0)
        m = jnp.maximum(jnp.max(s, axis=-1, keepdims=True), sink)
        e = jnp.exp2(s - m)
        denom = jnp.sum(e, axis=-1, keepdims=True) + jnp.exp2(sink - m)
        o = jnp.dot(e.astype(BF16), v, preferred_element_type=F32) / denom
        for g in range(ATT_GROUP):
            cols = pl.ds((hk * ATT_GROUP + g) * HEAD_DIM, HEAD_DIM)
            o_ref[rows, cols] = (o[g * WINDOW:(g + 1) * WINDOW, :] * gate_ref[rows, cols].astype(F32)).astype(BF16)

    pending = [scores(item) for item in items[:ATTN_LOOKAHEAD]]
    for i, item in enumerate(items):
        if i + ATTN_LOOKAHEAD < len(items):
            pending.append(scores(items[i + ATTN_LOOKAHEAD]))
        finish(item, pending.pop(0))


def _attention(h_rest, sink, batch, seq):
    t = batch * seq
    rows = ATTN_BLOCKS * WINDOW
    n_steps = seq // rows
    nb = seq // WINDOW
    cur = lambda b, n: b * n_steps + n
    prv = lambda b, n: b * nb + jnp.maximum(n * ATTN_BLOCKS - 1, 0)
    nxt = lambda b, n: b * nb + jnp.minimum((n + 1) * ATTN_BLOCKS, nb - 1)
    wide = lambda blk: pl.BlockSpec((rows, ATT_WIDTH), lambda b, n: (cur(b, n), blk))
    kv_cur = lambda blk: pl.BlockSpec((rows, ATT_KV_WIDTH), lambda b, n: (cur(b, n), blk))
    kv_edge = lambda rowmap, blk: pl.BlockSpec((WINDOW, ATT_KV_WIDTH), lambda b, n: (rowmap(b, n), blk))
    return pl.pallas_call(
        functools.partial(_attn_kernel, n_steps=n_steps),
        grid=(batch, n_steps),
        in_specs=[pl.BlockSpec(memory_space=pltpu.SMEM),
                  wide(H_ATT_Q), wide(H_ATT_GATE),
                  kv_edge(prv, H_ATT_K), kv_cur(H_ATT_K), kv_edge(nxt, H_ATT_K),
                  kv_edge(prv, H_ATT_V), kv_cur(H_ATT_V), kv_edge(nxt, H_ATT_V)],
        out_specs=pl.BlockSpec((rows, ATT_WIDTH), lambda b, n: (cur(b, n), 0)),
        out_shape=jax.ShapeDtypeStruct((t, ATT_WIDTH), BF16),
        compiler_params=_params("parallel", "parallel"),
        name="window_attention",
    )(sink, h_rest, h_rest, h_rest, h_rest, h_rest, h_rest, h_rest, h_rest)


def _merge_kernel(x_ref, oa_ref, ob_ref, oc_ref, g0_ref, g1_ref, g2_ref, wbr_ref, bias_ref, o_ref):
    x = x_ref[...]
    acc = None
    row0 = 0
    for i, (g_ref, o_in) in enumerate(((g0_ref, oa_ref), (g1_ref, ob_ref), (g2_ref, oc_ref))):
        width = o_in.shape[1]
        gate = jax.nn.sigmoid(jnp.dot(x, g_ref[...], preferred_element_type=F32) + bias_ref[i:i + 1, :])
        term = gate * jnp.dot(o_in[...], wbr_ref[row0:row0 + width, :], preferred_element_type=F32)
        acc = term if acc is None else acc + term
        row0 += width
    o_ref[...] = acc.astype(BF16)


def _merge(xn, o_a, o_b, o_c, w_rest, w_br, gate_bias):
    t = xn.shape[0]
    tm = min(MERGE_TM, t)
    nj = D_MODEL // MERGE_TN
    g0 = REST_ACT_WIDTH // MERGE_TN
    row = lambda width: pl.BlockSpec((tm, width), lambda i, j: (i, 0))
    gcol = lambda br: pl.BlockSpec((D_MODEL, MERGE_TN), lambda i, j: (0, g0 + br * nj + j))
    return pl.pallas_call(
        _merge_kernel,
        grid=(t // tm, nj),
        in_specs=[row(D_MODEL), row(GLA_WIDTH), row(SGU_WIDTH), row(ATT_WIDTH),
                  gcol(0), gcol(1), gcol(2),
                  pl.BlockSpec((GLA_WIDTH + SGU_WIDTH + ATT_WIDTH, MERGE_TN), lambda i, j: (0, j)),
                  pl.BlockSpec((N_BRANCH, MERGE_TN), lambda i, j: (0, j))],
        out_specs=pl.BlockSpec((tm, MERGE_TN), lambda i, j: (i, j)),
        out_shape=jax.ShapeDtypeStruct((t, D_MODEL), BF16),
        compiler_params=_params("parallel", "arbitrary"),
        name="gated_merge",
    )(xn, o_a, o_b, o_c, w_rest, w_rest, w_rest, w_br, gate_bias)


def _out_kernel(m_ref, w_ref, x_ref, o_ref):
    o_ref[...] = x_ref[...] + jnp.dot(m_ref[...], w_ref[...], preferred_element_type=F32)


def _out_proj(merged, w_out, x2):
    t = x2.shape[0]
    tm = min(OUT_TM, t)
    return pl.pallas_call(
        _out_kernel,
        grid=(t // tm, D_MODEL // OUT_TN),
        in_specs=[pl.BlockSpec((tm, D_MODEL), lambda i, j: (i, 0)),
                  pl.BlockSpec((D_MODEL, OUT_TN), lambda i, j: (0, j)),
                  pl.BlockSpec((tm, OUT_TN), lambda i, j: (i, j))],
        out_specs=pl.BlockSpec((tm, OUT_TN), lambda i, j: (i, j)),
        out_shape=jax.ShapeDtypeStruct((t, D_MODEL), F32),
        compiler_params=_params("parallel", "arbitrary"),
        name="out_proj",
    )(merged, w_out, x2)


def _cast_kernel(x_ref, o_ref):
    o_ref[...] = x_ref[...].astype(BF16)


def _cast_shift_kernel(a_ref, b_ref, o_ref):
    both = jnp.concatenate([a_ref[...], b_ref[...]], axis=1)
    o_ref[...] = both[:, CAST_SHIFT:CAST_SHIFT + CAST_TN].astype(BF16)


def _cast_bf16(w, layer, col0, n_cols, name):
    rows = w.shape[1]
    tile0 = col0 // CAST_TN
    return pl.pallas_call(
        _cast_kernel,
        grid=(rows // CAST_ROWS, n_cols // CAST_TN),
        in_specs=[pl.BlockSpec((None, CAST_ROWS, CAST_TN), lambda r, j: (layer, r, tile0 + j))],
        out_specs=pl.BlockSpec((CAST_ROWS, CAST_TN), lambda r, j: (r, j)),
        out_shape=jax.ShapeDtypeStruct((rows, n_cols), BF16),
        compiler_params=_params("parallel", "parallel"),
        name=name,
    )(w)


def _cast_bf16_shifted(w, layer, col0, name):
    rows, cols = w.shape[1], w.shape[2]
    n_cols = cols - col0 - CAST_SHIFT
    tile0 = col0 // CAST_TN
    lanes_per_tile = CAST_TN // LANES
    return pl.pallas_call(
        _cast_shift_kernel,
        grid=(rows // CAST_ROWS, n_cols // CAST_TN),
        in_specs=[pl.BlockSpec((None, CAST_ROWS, CAST_TN), lambda r, j: (layer, r, tile0 + j)),
                  pl.BlockSpec((None, CAST_ROWS, LANES), lambda r, j: (layer, r, (tile0 + j + 1) * lanes_per_tile))],
        out_specs=pl.BlockSpec((CAST_ROWS, CAST_TN), lambda r, j: (r, j)),
        out_shape=jax.ShapeDtypeStruct((rows, n_cols), BF16),
        compiler_params=_params("parallel", "parallel"),
        name=name,
    )(w, w)


def _prepare_layer(layer, norm_gain, w_in, gla_gate_up, gla_gate_bias, gla_norm_gain, sgu_ln_gain, sgu_ln_bias,
                   sgu_w, sgu_b, q_norm_gain, k_norm_gain, sink, gate_bias, w_br, w_out):
    lr0 = 2 * GLA_QK_WIDTH + GLA_WIDTH
    lr1 = lr0 + 2 * GLA_GATE_RANK
    assert lr1 - lr0 == CAST_SHIFT and lr0 % CAST_TN == 0
    w_lr = jnp.pad(w_in[layer, :, lr0:lr1], ((0, 0), (0, LANES - 2 * GLA_GATE_RANK))).astype(BF16)
    up2 = jnp.zeros((LANES, 2 * GLA_QK_WIDTH), F32)
    up2 = up2.at[:GLA_GATE_RANK, :GLA_QK_WIDTH].set(gla_gate_up[layer, 0])
    up2 = up2.at[GLA_GATE_RANK:2 * GLA_GATE_RANK, GLA_QK_WIDTH:].set(gla_gate_up[layer, 1])
    return dict(
        norm_gain=norm_gain[layer],
        w_qkv=_cast_bf16(w_in, layer, 0, lr0, "cast_w_qkv"),
        w_rest=_cast_bf16_shifted(w_in, layer, lr0, "cast_w_rest"),
        w_lr=w_lr, up2=up2.astype(BF16),
        bias2=gla_gate_bias[layer].reshape(1, 2 * GLA_QK_WIDTH), gla_norm_gain=gla_norm_gain[layer],
        sgu_ln_gain=sgu_ln_gain[layer], sgu_ln_bias=sgu_ln_bias[layer], sgu_w=sgu_w[layer], sgu_b=sgu_b[layer],
        q_gain=q_norm_gain[layer].reshape(1, HEAD_DIM), k_gain=k_norm_gain[layer].reshape(1, HEAD_DIM),
        sink=sink[layer], gate_bias=gate_bias[layer],
        w_br=_cast_bf16(w_br, layer, 0, D_MODEL, "cast_w_br"),
        w_out=_cast_bf16(w_out, layer, 0, D_MODEL, "cast_w_out"))


def _layer(x, p):
    batch, seq, _ = x.shape
    t = batch * seq
    x2 = x.reshape(t, D_MODEL)
    xn = _pre_norm(x2, p["norm_gain"])

    h_qkv = _proj_raw(xn, p["w_qkv"], "proj_gla_qkv")
    h_rest = _proj_rest(xn, p["w_rest"], p["q_gain"], p["k_gain"], seq)
    log_a, gla_exact = _gla_log_decay(xn, p["w_lr"], p["up2"], p["bias2"])

    o_a = _gla(h_qkv, h_rest, log_a, gla_exact, p["gla_norm_gain"], batch, seq)
    o_b = _sgu(h_rest, p["sgu_ln_gain"], p["sgu_ln_bias"], p["sgu_w"], p["sgu_b"])
    o_c = _attention(h_rest, p["sink"], batch, seq)

    merged = _merge(xn, o_a, o_b, o_c, p["w_rest"], p["w_br"], p["gate_bias"])
    return _out_proj(merged, p["w_out"], x2).reshape(batch, seq, D_MODEL)


def kernel(x_prompt, x_sample, norm_gain, w_in, gla_gate_up, gla_gate_bias, gla_norm_gain, sgu_ln_gain,
           sgu_ln_bias, sgu_w, sgu_b, q_norm_gain, k_norm_gain, sink, gate_bias, w_br, w_out):
    y_prompt, y_sample = x_prompt, x_sample
    for l in range(norm_gain.shape[0]):
        p = _prepare_layer(l, norm_gain, w_in, gla_gate_up, gla_gate_bias, gla_norm_gain, sgu_ln_gain,
                           sgu_ln_bias, sgu_w, sgu_b, q_norm_gain, k_norm_gain, sink, gate_bias, w_br, w_out)
        y_prompt = _layer(y_prompt, p)
        y_sample = _layer(y_sample, p)
    return (y_prompt, y_sample)
```

```python
import functools
import math

import jax
import jax.numpy as jnp
from jax import lax
from jax.experimental import pallas as pl
from jax.experimental.pallas import tpu as pltpu

F32 = jnp.float32
BF16 = jnp.bfloat16

D_MODEL = 4096
HEAD_DIM = 128
GLA_WIDTH = 1536
GLA_DV = 256
GLA_HEADS = 6
GLA_DK = 128
GLA_QK_WIDTH = 768
GLA_GATE_RANK = 16
GLA_GATE_TAU = 16.0
SGU_WIDTH = 1024
SGU_CHUNK = 128
SGU_GROUPS = 8
ATT_WIDTH = 1536
ATT_Q_HEADS = 12
ATT_KV_HEADS = 4
ATT_GROUP = 3
ATT_KV_WIDTH = 512
WINDOW = 128
ROPE_THETA = 500000.0
ROPE_DIMS = 32
N_BRANCH = 3
NORM_EPS = 1e-6
LN_EPS = 1e-5

VMEM_LIMIT_BYTES = 56 * 1024 * 1024
LANES = 128
SUBLANES = 8

NORM_ROWS = 256
PROJ_TM = 1024
PROJ_REST_TM = 2048
PROJ_TN = 512
PROJ_RAW_TN = 1024
PROJ_SUB_ROWS = 256
GLA_ROWS = 256
GLA_HEADS_PER_STEP = 6
GLA_CHUNK = 64
GLA_FAST_MAX_DECAY = 40.0
GLA_PATH_WIDE, GLA_PATH_CHUNKED, GLA_PATH_EXACT = 0, 1, 2
SGU_ROWS = 512
ATTN_BLOCKS = 4
ATTN_LOOKAHEAD = 2
MERGE_TM = 512
MERGE_TN = 512
OUT_TM = 1024
OUT_TN = 1024
CAST_ROWS = 2048
CAST_TN = 512
CAST_SHIFT = 2 * GLA_GATE_RANK

KIND_SILU, KIND_GELU, KIND_QNORM, KIND_KNORM, KIND_RAW = range(5)
_REST_KIND = ([KIND_SILU] * 3 + [KIND_GELU] * 4 + [KIND_SILU] * 2 + [KIND_QNORM] * 3 + [KIND_KNORM, KIND_RAW]
              + [KIND_SILU] * 3)
_REST_OUT_TILE = [6, 7, 8, 0, 1, 2, 3, 4, 5, 9, 10, 11, 15, 16, 12, 13, 14]
REST_ACT_WIDTH = len(_REST_KIND) * PROJ_TN
H_SGU_U, H_SGU_V, H_SGU_GATE = 0, 1, 2
H_GLA_GATE, H_ATT_Q, H_ATT_GATE = 2, 3, 4
H_ATT_K, H_ATT_V = 15, 16


def _params(*sem):
    return pltpu.CompilerParams(dimension_semantics=sem, vmem_limit_bytes=VMEM_LIMIT_BYTES)


def _silu(x):
    return x * jax.nn.sigmoid(x)


def _gelu_tanh(x):
    c = math.sqrt(2.0 / math.pi)
    return x * (0.5 * (1.0 + jnp.tanh(c * (x + 0.044715 * (x * x * x)))))


def _norm_kernel(x_ref, g_ref, o_ref):
    x = x_ref[...]
    ms = jnp.mean(x * x, axis=-1, keepdims=True)
    o_ref[...] = (x * lax.rsqrt(ms + NORM_EPS) * g_ref[...]).astype(BF16)


def _pre_norm(x2, gain):
    t = x2.shape[0]
    return pl.pallas_call(
        _norm_kernel,
        grid=(t // NORM_ROWS,),
        in_specs=[pl.BlockSpec((NORM_ROWS, D_MODEL), lambda i: (i, 0)),
                  pl.BlockSpec((1, D_MODEL), lambda i: (0, 0))],
        out_specs=pl.BlockSpec((NORM_ROWS, D_MODEL), lambda i: (i, 0)),
        out_shape=jax.ShapeDtypeStruct((t, D_MODEL), BF16),
        compiler_params=_params("parallel"),
        name="pre_norm",
    )(x2, gain.reshape(1, D_MODEL))


def _dot_epilogue(x_ref, w_ref, o_ref, epilogue):
    w = w_ref[...]
    for r in range(x_ref.shape[0] // PROJ_SUB_ROWS):
        rows = pl.ds(r * PROJ_SUB_ROWS, PROJ_SUB_ROWS)
        acc = jnp.dot(x_ref[rows, :], w, preferred_element_type=F32)
        o_ref[rows, :] = epilogue(acc, rows).astype(BF16)


def _proj_raw_kernel(x_ref, w_ref, o_ref):
    _dot_epilogue(x_ref, w_ref, o_ref, lambda acc, rows: acc)


def _proj_raw(xn, w, name):
    t = xn.shape[0]
    n = w.shape[1]
    tm = min(PROJ_TM, t)
    return pl.pallas_call(
        _proj_raw_kernel,
        grid=(t // tm, n // PROJ_RAW_TN),
        in_specs=[pl.BlockSpec((tm, D_MODEL), lambda i, j: (i, 0)),
                  pl.BlockSpec((D_MODEL, PROJ_RAW_TN), lambda i, j: (0, j))],
        out_specs=pl.BlockSpec((tm, PROJ_RAW_TN), lambda i, j: (i, j)),
        out_shape=jax.ShapeDtypeStruct((t, n), BF16),
        compiler_params=_params("parallel", "arbitrary"),
        name=name,
    )(xn, w)


def _head_norm_rope(acc, gain, cos_t, sin_lo, sin_hi):
    outs = []
    for h in range(PROJ_TN // HEAD_DIM):
        xh = acc[:, h * HEAD_DIM:(h + 1) * HEAD_DIM]
        ms = jnp.mean(xh * xh, axis=-1, keepdims=True)
        y = xh * lax.rsqrt(ms + NORM_EPS) * gain
        half = ROPE_DIMS // 2
        up = pltpu.roll(y, HEAD_DIM - half, axis=1)
        dn = pltpu.roll(y, half, axis=1)
        outs.append(y * cos_t + up * sin_lo + dn * sin_hi)
    return jnp.concatenate(outs, axis=1)


def _proj_rest_kernel(kind_ref, tile_ref, x_ref, w_ref, qg_ref, kg_ref, cos_ref, slo_ref, shi_ref, o_ref):
    del tile_ref
    kind = kind_ref[pl.program_id(1)]

    def norm_rope(gain_ref):
        return lambda acc, rows: _head_norm_rope(acc, gain_ref[...], cos_ref[rows, :], slo_ref[rows, :],
                                                 shi_ref[rows, :])

    epilogues = {
        KIND_SILU: lambda acc, rows: _silu(acc),
        KIND_GELU: lambda acc, rows: _gelu_tanh(acc),
        KIND_QNORM: norm_rope(qg_ref),
        KIND_KNORM: norm_rope(kg_ref),
        KIND_RAW: lambda acc, rows: acc,
    }
    for k, epilogue in epilogues.items():
        pl.when(kind == k)(functools.partial(_dot_epilogue, x_ref, w_ref, o_ref, epilogue))


def _proj_rest(xn, w_rest, q_gain, k_gain, seq):
    t = xn.shape[0]
    tm = min(PROJ_REST_TM, t)
    blocks_per_seq = seq // tm if seq >= tm else 1
    tables = _rope_tables(seq)
    if seq < tm:
        tables = tuple(jnp.tile(tb, (tm // seq, 1)) for tb in tables)
    vec = pl.BlockSpec((1, HEAD_DIM), lambda i, j, kind, tile: (0, 0))
    tab = pl.BlockSpec((tm, HEAD_DIM), lambda i, j, kind, tile: (i % blocks_per_seq, 0))
    return pl.pallas_call(
        _proj_rest_kernel,
        grid_spec=pltpu.PrefetchScalarGridSpec(
            num_scalar_prefetch=2,
            grid=(t // tm, len(_REST_KIND)),
            in_specs=[pl.BlockSpec((tm, D_MODEL), lambda i, j, kind, tile: (i, 0)),
                      pl.BlockSpec((D_MODEL, PROJ_TN), lambda i, j, kind, tile: (0, j)),
                      vec, vec, tab, tab, tab],
            out_specs=pl.BlockSpec((tm, PROJ_TN), lambda i, j, kind, tile: (i, tile[j]))),
        out_shape=jax.ShapeDtypeStruct((t, REST_ACT_WIDTH), BF16),
        compiler_params=_params("parallel", "arbitrary"),
        name="proj_rest",
    )(jnp.asarray(_REST_KIND, jnp.int32), jnp.asarray(_REST_OUT_TILE, jnp.int32),
      xn, w_rest, q_gain, k_gain, *tables)


def _rope_tables(seq):
    inv_freq = ROPE_THETA ** (-jnp.arange(0, ROPE_DIMS, 2, dtype=F32) / ROPE_DIMS)
    ang = jnp.arange(seq, dtype=F32)[:, None] * inv_freq[None, :]
    cos, sin = jnp.cos(ang), jnp.sin(ang)
    half = ROPE_DIMS // 2
    ones = jnp.ones((seq, HEAD_DIM - ROPE_DIMS), F32)
    zeros = jnp.zeros((seq, HEAD_DIM - ROPE_DIMS), F32)
    zh = jnp.zeros((seq, half), F32)
    cos_t = jnp.concatenate([cos, cos, ones], axis=1)
    sin_lo = jnp.concatenate([-sin, zh, zeros], axis=1)
    sin_hi = jnp.concatenate([zh, sin, zeros], axis=1)
    return cos_t, sin_lo, sin_hi


def _decay_kernel(x_ref, w_ref, up_ref, b_ref, o_ref, mx_ref):
    lr = jnp.dot(x_ref[...], w_ref[...], preferred_element_type=F32)
    z = jnp.dot(lr.astype(BF16), up_ref[...], preferred_element_type=F32) + b_ref[...]
    log_a = (jnp.minimum(z, 0.0) - jnp.log(1.0 + jnp.exp(-jnp.abs(z)))) * (1.0 / GLA_GATE_TAU)
    o_ref[...] = log_a
    n_blocks = log_a.shape[0] // GLA_ROWS
    full_max = lambda x: jnp.broadcast_to(jnp.max(jnp.max(x, axis=1, keepdims=True), axis=0, keepdims=True),
                                          (1, LANES))
    blocks = [log_a[r * GLA_ROWS:(r + 1) * GLA_ROWS, :] for r in range(n_blocks)]
    zero = [jnp.zeros((1, LANES), F32)] * (SUBLANES // 2 - n_blocks)
    rows = ([full_max(-blk) for blk in blocks] + zero
            + [full_max(-jnp.sum(blk, axis=0, keepdims=True)) for blk in blocks] + zero)
    mx_ref[0] = jnp.concatenate(rows, axis=0)


def _gla_log_decay(xn, w_lr, up2, bias2):
    t = xn.shape[0]
    tm = min(PROJ_TM, t)
    n = 2 * GLA_QK_WIDTH
    log_a, mx = pl.pallas_call(
        _decay_kernel,
        grid=(t // tm,),
        in_specs=[pl.BlockSpec((tm, D_MODEL), lambda i: (i, 0)),
                  pl.BlockSpec((D_MODEL, LANES), lambda i: (0, 0)),
                  pl.BlockSpec((LANES, n), lambda i: (0, 0)),
                  pl.BlockSpec((1, n), lambda i: (0, 0))],
        out_specs=[pl.BlockSpec((tm, n), lambda i: (i, 0)),
                   pl.BlockSpec((1, SUBLANES, LANES), lambda i: (i, 0, 0))],
        out_shape=[jax.ShapeDtypeStruct((t, n), F32),
                   jax.ShapeDtypeStruct((t // tm, SUBLANES, LANES), F32)],
        compiler_params=_params("parallel"),
        name="gla_log_decay",
    )(xn, w_lr, up2, bias2)
    n_blocks = tm // GLA_ROWS
    step_max = mx[:, :n_blocks, 0].reshape(t // GLA_ROWS)
    block_max = mx[:, SUBLANES // 2:SUBLANES // 2 + n_blocks, 0].reshape(t // GLA_ROWS)
    path = jnp.where(block_max <= GLA_FAST_MAX_DECAY, GLA_PATH_WIDE,
                     jnp.where(step_max <= GLA_FAST_MAX_DECAY / GLA_CHUNK, GLA_PATH_CHUNKED, GLA_PATH_EXACT))
    return log_a, path.astype(jnp.int32)


def _gla_head_rows(q, k, v, b, state, mask, chunk_rows, reverse, exact_scratch):
    n_chunks = q.shape[0] // chunk_rows
    chunk = lambda x, c: x[c * chunk_rows:(c + 1) * chunk_rows]
    edge = 0 if reverse else chunk_rows - 1
    b_tot = [chunk(b, c)[edge:edge + 1, :] for c in range(n_chunks)]
    dec_row = [jnp.exp(bt) for bt in b_tot]
    qb = (q * jnp.exp(b)).astype(BF16)

    if exact_scratch is None:
        k_inv = k * jnp.exp(-b)
        s = lax.dot_general(qb, k_inv.astype(BF16), (((1,), (1,)), ((), ())), preferred_element_type=F32)
        o = jnp.dot(jnp.where(mask, s, 0.0).astype(BF16), v, preferred_element_type=F32)
        kb = [chunk(k_inv, c) * dec_row[c] for c in range(n_chunks)]
    else:
        b_scr, k_scr = exact_scratch
        col = lax.broadcasted_iota(jnp.int32, (chunk_rows, chunk_rows), 1)
        o_chunks, kb = [], []
        for c in range(n_chunks):
            qc, bc = chunk(q, c), chunk(b, c)
            b_scr[...] = bc
            k_scr[...] = chunk(k, c)

            def column(j, acc, qc=qc, bc=bc):
                decay = jnp.exp(jnp.minimum(bc - b_scr[pl.ds(j, 1), :], 0.0))
                w = qc * decay * k_scr[pl.ds(j, 1), :]
                return jnp.where(col == j, jnp.sum(w, axis=-1, keepdims=True), acc)

            s = lax.fori_loop(0, chunk_rows, column, jnp.zeros((chunk_rows, chunk_rows), F32))
            lo = c * chunk_rows
            p = jnp.where(mask[lo:lo + chunk_rows, lo:lo + chunk_rows], s, 0.0)
            o_chunks.append(jnp.dot(p.astype(BF16), chunk(v, c), preferred_element_type=F32))
            kb.append(chunk(k, c) * jnp.exp(b_tot[c] - bc))
        o = jnp.concatenate(o_chunks, axis=0)

    o_state = [None] * n_chunks
    for c in (range(n_chunks - 1, -1, -1) if reverse else range(n_chunks)):
        o_state[c] = jnp.dot(chunk(qb, c), state.astype(BF16), preferred_element_type=F32)
        upd = lax.dot_general(kb[c].astype(BF16), chunk(v, c), (((0,), (0,)), ((), ())),
                              preferred_element_type=F32)
        dec = jnp.transpose(jnp.broadcast_to(dec_row[c], (GLA_DK, GLA_DK)))
        state = jnp.concatenate([dec] * (GLA_DV // GLA_DK), axis=1) * state + upd
    return o + jnp.concatenate(o_state, axis=0), state


def _gla_scan_rows(q_ref, k_ref, v_ref, g_ref, state_ref, chunk_rows, reverse, emit, exact_scratch):
    row = lax.broadcasted_iota(jnp.int32, (GLA_ROWS, GLA_ROWS), 0)
    col = lax.broadcasted_iota(jnp.int32, (GLA_ROWS, GLA_ROWS), 1)
    shift = chunk_rows.bit_length() - 1
    mask = jnp.logical_and((col >= row) if reverse else (col <= row),
                           lax.shift_right_logical(row, shift) == lax.shift_right_logical(col, shift))
    tri_b = mask.astype(BF16)
    rest = g_ref[...]
    b_all = None
    for _ in range(3):
        term = rest.astype(BF16)
        rest = rest - term.astype(F32)
        part = jnp.dot(tri_b, term, preferred_element_type=F32)
        b_all = part if b_all is None else b_all + part
    for h in range(GLA_HEADS_PER_STEP):
        kc = pl.ds(h * GLA_DK, GLA_DK)
        vc = pl.ds(h * GLA_DV, GLA_DV)
        q = q_ref[:, kc].astype(F32) * (GLA_DK ** -0.5)
        k = k_ref[:, kc].astype(F32)
        o, new_state = _gla_head_rows(q, k, v_ref[:, vc], b_all[:, h * GLA_DK:(h + 1) * GLA_DK], state_ref[h],
                                      mask, chunk_rows, reverse, exact_scratch)
        state_ref[h] = new_state
        emit(vc, o)


def _gla_scan_step(path_ref, row_block, refs, state_ref, b_scr, k_scr, reverse, emit):
    @pl.when(pl.program_id(2) == 0)
    def _():
        state_ref[...] = jnp.zeros_like(state_ref)

    path = path_ref[row_block]

    @pl.when(path == GLA_PATH_WIDE)
    def _():
        _gla_scan_rows(*refs, state_ref, GLA_ROWS, reverse, emit, None)

    @pl.when(path == GLA_PATH_CHUNKED)
    def _():
        _gla_scan_rows(*refs, state_ref, GLA_CHUNK, reverse, emit, None)

    @pl.when(path == GLA_PATH_EXACT)
    def _():
        _gla_scan_rows(*refs, state_ref, GLA_CHUNK, reverse, emit, (b_scr, k_scr))


def _gla_fwd_kernel(exact_ref, q_ref, k_ref, v_ref, g_ref, o_ref, state_ref, b_scr, k_scr, *, nb):
    def emit(cols, o):
        o_ref[:, cols] = o

    row_block = pl.program_id(0) * nb + pl.program_id(2)
    _gla_scan_step(exact_ref, row_block, (q_ref, k_ref, v_ref, g_ref), state_ref, b_scr, k_scr, False, emit)


def _gla_bwd_kernel(exact_ref, q_ref, k_ref, v_ref, g_ref, fwd_ref, gate_ref, gain_ref, o_ref,
                    state_ref, b_scr, k_scr, *, nb):
    def emit(cols, o):
        tot = fwd_ref[:, cols] + o
        ms = jnp.mean(tot * tot, axis=-1, keepdims=True)
        y = tot * lax.rsqrt(ms + NORM_EPS) * gain_ref[...]
        o_ref[:, cols] = (y * gate_ref[:, cols].astype(F32)).astype(BF16)

    row_block = pl.program_id(0) * nb + (nb - 1 - pl.program_id(2))
    _gla_scan_step(exact_ref, row_block, (q_ref, k_ref, v_ref, g_ref), state_ref, b_scr, k_scr, True, emit)


def _gla(h_qkv, h_rest, log_a, exact, gain, batch, seq):
    t = batch * seq
    nb = seq // GLA_ROWS
    hp = GLA_HEADS_PER_STEP
    qk_w, v_w = hp * GLA_DK, hp * GLA_DV
    k_off = GLA_QK_WIDTH // qk_w
    v_off = 2 * GLA_QK_WIDTH // v_w
    gate_off = H_GLA_GATE * GLA_WIDTH // v_w
    dir_off = GLA_QK_WIDTH // qk_w
    grid = (batch, GLA_HEADS // hp, nb)
    scratch = [pltpu.VMEM((hp, GLA_DK, GLA_DV), F32),
               pltpu.VMEM((GLA_CHUNK, GLA_DK), F32), pltpu.VMEM((GLA_CHUNK, GLA_DK), F32)]

    def specs(rowmap, direction):
        return [pl.BlockSpec((GLA_ROWS, qk_w), lambda b, h, n, e: (rowmap(b, n), h)),
                pl.BlockSpec((GLA_ROWS, qk_w), lambda b, h, n, e: (rowmap(b, n), k_off + h)),
                pl.BlockSpec((GLA_ROWS, v_w), lambda b, h, n, e: (rowmap(b, n), v_off + h)),
                pl.BlockSpec((GLA_ROWS, qk_w), lambda b, h, n, e: (rowmap(b, n), direction * dir_off + h))]

    fmap = lambda b, n: b * nb + n
    rmap = lambda b, n: b * nb + (nb - 1 - n)
    fwd = pl.pallas_call(
        functools.partial(_gla_fwd_kernel, nb=nb),
        grid_spec=pltpu.PrefetchScalarGridSpec(
            num_scalar_prefetch=1,
            grid=grid,
            in_specs=specs(fmap, 0),
            out_specs=pl.BlockSpec((GLA_ROWS, v_w), lambda b, h, n, e: (fmap(b, n), h)),
            scratch_shapes=scratch),
        out_shape=jax.ShapeDtypeStruct((t, GLA_WIDTH), F32),
        compiler_params=_params("parallel", "parallel", "arbitrary"),
        name="gla_fwd",
    )(exact, h_qkv, h_qkv, h_qkv, log_a)
    return pl.pallas_call(
        functools.partial(_gla_bwd_kernel, nb=nb),
        grid_spec=pltpu.PrefetchScalarGridSpec(
            num_scalar_prefetch=1,
            grid=grid,
            in_specs=specs(rmap, 1) + [
                pl.BlockSpec((GLA_ROWS, v_w), lambda b, h, n, e: (rmap(b, n), h)),
                pl.BlockSpec((GLA_ROWS, v_w), lambda b, h, n, e: (rmap(b, n), gate_off + h)),
                pl.BlockSpec((1, GLA_DV), lambda b, h, n, e: (0, 0))],
            out_specs=pl.BlockSpec((GLA_ROWS, v_w), lambda b, h, n, e: (rmap(b, n), h)),
            scratch_shapes=scratch),
        out_shape=jax.ShapeDtypeStruct((t, GLA_WIDTH), BF16),
        compiler_params=_params("parallel", "parallel", "arbitrary"),
        name="gla_bwd",
    )(exact, h_qkv, h_qkv, h_qkv, log_a, fwd, h_rest, gain.reshape(1, GLA_DV))


def _sgu_kernel(u_ref, v_ref, gate_ref, lng_ref, lnb_ref, w_ref, bt_ref, o_ref):
    for c in range(SGU_ROWS // SGU_CHUNK):
        rows = pl.ds(c * SGU_CHUNK, SGU_CHUNK)
        x = v_ref[rows, :].astype(F32)
        mu = jnp.mean(x, axis=-1, keepdims=True)
        xc = x - mu
        var = jnp.mean(xc * xc, axis=-1, keepdims=True)
        y = (xc * lax.rsqrt(var + LN_EPS) * lng_ref[...] + lnb_ref[...]).astype(BF16)
        for g in range(SGU_GROUPS):
            cols = pl.ds(g * LANES, LANES)
            mixed = jnp.dot(w_ref[g], y[:, g * LANES:(g + 1) * LANES], preferred_element_type=F32)
            mixed = mixed + bt_ref[:, g:g + 1]
            o_ref[rows, cols] = (u_ref[rows, cols].astype(F32) * mixed
                                 * gate_ref[rows, cols].astype(F32)).astype(BF16)


def _sgu(h_rest, ln_gain, ln_bias, w, b):
    t = h_rest.shape[0]
    blk = lambda c: pl.BlockSpec((SGU_ROWS, SGU_WIDTH), lambda i: (i, c))
    return pl.pallas_call(
        _sgu_kernel,
        grid=(t // SGU_ROWS,),
        in_specs=[blk(H_SGU_U), blk(H_SGU_V), blk(H_SGU_GATE),
                  pl.BlockSpec((1, SGU_WIDTH), lambda i: (0, 0)),
                  pl.BlockSpec((1, SGU_WIDTH), lambda i: (0, 0)),
                  pl.BlockSpec((SGU_GROUPS, SGU_CHUNK, SGU_CHUNK), lambda i: (0, 0, 0)),
                  pl.BlockSpec((SGU_CHUNK, SGU_GROUPS), lambda i: (0, 0))],
        out_specs=pl.BlockSpec((SGU_ROWS, SGU_WIDTH), lambda i: (i, 0)),
        out_shape=jax.ShapeDtypeStruct((t, SGU_WIDTH), BF16),
        compiler_params=_params("parallel"),
        name="sgu",
    )(h_rest, h_rest, h_rest, ln_gain.reshape(1, SGU_WIDTH), ln_bias.reshape(1, SGU_WIDTH),
      w.astype(BF16), jnp.transpose(b))


def _attn_kernel(sink_ref, q_ref, gate_ref, kp_ref, kc_ref, kn_ref, vp_ref, vc_ref, vn_ref, o_ref, *, n_steps):
    n = pl.program_id(1)
    log2e = math.log2(math.e)
    qpos = lax.broadcasted_iota(jnp.int32, (WINDOW, 3 * WINDOW), 0)
    kpos = lax.broadcasted_iota(jnp.int32, (WINDOW, 3 * WINDOW), 1) - WINDOW
    band = jnp.abs(kpos - qpos) <= WINDOW
    full = slice(None)
    block = lambda c: pl.ds(c * WINDOW, WINDOW)

    items = []
    for a in range(ATTN_BLOCKS):
        prev = (kp_ref, vp_ref, full) if a == 0 else (kc_ref, vc_ref, block(a - 1))
        nxt = (kn_ref, vn_ref, full) if a == ATTN_BLOCKS - 1 else (kc_ref, vc_ref, block(a + 1))
        keys = [(prev[0], prev[2]), (kc_ref, block(a)), (nxt[0], nxt[2])]
        vals = [(prev[1], prev[2]), (vc_ref, block(a)), (nxt[1], nxt[2])]
        valid = band
        if a == 0:
            valid = jnp.logical_and(valid, jnp.logical_or(kpos >= 0, n > 0))
        if a == ATTN_BLOCKS - 1:
            valid = jnp.logical_and(valid, jnp.logical_or(kpos < WINDOW, n < n_steps - 1))
        valid = jnp.concatenate([valid] * ATT_GROUP, axis=0)
        items += [(block(a), keys, vals, valid, hk) for hk in range(ATT_KV_HEADS)]

    def scores(item):
        rows, keys, _, _, hk = item
        kcols = pl.ds(hk * HEAD_DIM, HEAD_DIM)
        q = jnp.concatenate([q_ref[rows, pl.ds((hk * ATT_GROUP + g) * HEAD_DIM, HEAD_DIM)]
                             for g in range(ATT_GROUP)], axis=0)
        k = jnp.concatenate([ref[r, kcols] for ref, r in keys], axis=0)
        return lax.dot_general(q, k, (((1,), (1,)), ((), ())), preferred_element_type=F32)

    def finish(item, s):
        rows, _, vals, valid, hk = item
        kcols = pl.ds(hk * HEAD_DIM, HEAD_DIM)
        v = jnp.concatenate([ref[r, kcols] for ref, r in vals], axis=0)
        s = jnp.where(valid, s * (HEAD_DIM ** -0.5 * log2e), -jnp.inf)
        sink = jnp.concatenate([jnp.full((WINDOW, 1), sink_ref[hk * ATT_GROUP + g] * log2e, F32)
                                for g in range(ATT_GROUP)], axis=0)
        m = jnp.maximum(jnp.max(s, axis=-1, keepdims=True), sink)
        e = jnp.exp2(s - m)
        denom = jnp.sum(e, axis=-1, keepdims=True) + jnp.exp2(sink - m)
        o = jnp.dot(e.astype(BF16), v, preferred_element_type=F32) / denom
        for g in range(ATT_GROUP):
            cols = pl.ds((hk * ATT_GROUP + g) * HEAD_DIM, HEAD_DIM)
            o_ref[rows, cols] = (o[g * WINDOW:(g + 1) * WINDOW, :] * gate_ref[rows, cols].astype(F32)).astype(BF16)

    pending = [scores(item) for item in items[:ATTN_LOOKAHEAD]]
    for i, item in enumerate(items):
        if i + ATTN_LOOKAHEAD < len(items):
            pending.append(scores(items[i + ATTN_LOOKAHEAD]))
        finish(item, pending.pop(0))


def _attention(h_rest, sink, batch, seq):
    t = batch * seq
    rows = ATTN_BLOCKS * WINDOW
    n_steps = seq // rows
    nb = seq // WINDOW
    cur = lambda b, n: b * n_steps + n
    prv = lambda b, n: b * nb + jnp.maximum(n * ATTN_BLOCKS - 1, 0)
    nxt = lambda b, n: b * nb + jnp.minimum((n + 1) * ATTN_BLOCKS, nb - 1)
    wide = lambda blk: pl.BlockSpec((rows, ATT_WIDTH), lambda b, n: (cur(b, n), blk))
    kv_cur = lambda blk: pl.BlockSpec((rows, ATT_KV_WIDTH), lambda b, n: (cur(b, n), blk))
    kv_edge = lambda rowmap, blk: pl.BlockSpec((WINDOW, ATT_KV_WIDTH), lambda b, n: (rowmap(b, n), blk))
    return pl.pallas_call(
        functools.partial(_attn_kernel, n_steps=n_steps),
        grid=(batch, n_steps),
        in_specs=[pl.BlockSpec(memory_space=pltpu.SMEM),
                  wide(H_ATT_Q), wide(H_ATT_GATE),
                  kv_edge(prv, H_ATT_K), kv_cur(H_ATT_K), kv_edge(nxt, H_ATT_K),
                  kv_edge(prv, H_ATT_V), kv_cur(H_ATT_V), kv_edge(nxt, H_ATT_V)],
        out_specs=pl.BlockSpec((rows, ATT_WIDTH), lambda b, n: (cur(b, n), 0)),
        out_shape=jax.ShapeDtypeStruct((t, ATT_WIDTH), BF16),
        compiler_params=_params("parallel", "parallel"),
        name="window_attention",
    )(sink, h_rest, h_rest, h_rest, h_rest, h_rest, h_rest, h_rest, h_rest)


def _merge_kernel(x_ref, oa_ref, ob_ref, oc_ref, g0_ref, g1_ref, g2_ref, wbr_ref, bias_ref, o_ref):
    x = x_ref[...]
    acc = None
    row0 = 0
    for i, (g_ref, o_in) in enumerate(((g0_ref, oa_ref), (g1_ref, ob_ref), (g2_ref, oc_ref))):
        width = o_in.shape[1]
        gate = jax.nn.sigmoid(jnp.dot(x, g_ref[...], preferred_element_type=F32) + bias_ref[i:i + 1, :])
        term = gate * jnp.dot(o_in[...], wbr_ref[row0:row0 + width, :], preferred_element_type=F32)
        acc = term if acc is None else acc + term
        row0 += width
    o_ref[...] = acc.astype(BF16)


def _merge(xn, o_a, o_b, o_c, w_rest, w_br, gate_bias):
    t = xn.shape[0]
    tm = min(MERGE_TM, t)
    nj = D_MODEL // MERGE_TN
    g0 = REST_ACT_WIDTH // MERGE_TN
    row = lambda width: pl.BlockSpec((tm, width), lambda i, j: (i, 0))
    gcol = lambda br: pl.BlockSpec((D_MODEL, MERGE_TN), lambda i, j: (0, g0 + br * nj + j))
    return pl.pallas_call(
        _merge_kernel,
        grid=(t // tm, nj),
        in_specs=[row(D_MODEL), row(GLA_WIDTH), row(SGU_WIDTH), row(ATT_WIDTH),
                  gcol(0), gcol(1), gcol(2),
                  pl.BlockSpec((GLA_WIDTH + SGU_WIDTH + ATT_WIDTH, MERGE_TN), lambda i, j: (0, j)),
                  pl.BlockSpec((N_BRANCH, MERGE_TN), lambda i, j: (0, j))],
        out_specs=pl.BlockSpec((tm, MERGE_TN), lambda i, j: (i, j)),
        out_shape=jax.ShapeDtypeStruct((t, D_MODEL), BF16),
        compiler_params=_params("parallel", "arbitrary"),
        name="gated_merge",
    )(xn, o_a, o_b, o_c, w_rest, w_rest, w_rest, w_br, gate_bias)


def _out_kernel(m_ref, w_ref, x_ref, o_ref):
    o_ref[...] = x_ref[...] + jnp.dot(m_ref[...], w_ref[...], preferred_element_type=F32)


def _out_proj(merged, w_out, x2):
    t = x2.shape[0]
    tm = min(OUT_TM, t)
    return pl.pallas_call(
        _out_kernel,
        grid=(t // tm, D_MODEL // OUT_TN),
        in_specs=[pl.BlockSpec((tm, D_MODEL), lambda i, j: (i, 0)),
                  pl.BlockSpec((D_MODEL, OUT_TN), lambda i, j: (0, j)),
                  pl.BlockSpec((tm, OUT_TN), lambda i, j: (i, j))],
        out_specs=pl.BlockSpec((tm, OUT_TN), lambda i, j: (i, j)),
        out_shape=jax.ShapeDtypeStruct((t, D_MODEL), F32),
        compiler_params=_params("parallel", "arbitrary"),
        name="out_proj",
    )(merged, w_out, x2)


def _cast_kernel(x_ref, o_ref):
    o_ref[...] = x_ref[...].astype(BF16)


def _cast_bf16(w, layer, name):
    rows, cols = w.shape[1], w.shape[2]
    return pl.pallas_call(
        _cast_kernel,
        grid=(rows // CAST_ROWS, cols // CAST_TN),
        in_specs=[pl.BlockSpec((None, CAST_ROWS, CAST_TN), lambda r, j: (layer, r, j))],
        out_specs=pl.BlockSpec((CAST_ROWS, CAST_TN), lambda r, j: (r, j)),
        out_shape=jax.ShapeDtypeStruct((rows, cols), BF16),
        compiler_params=_params("parallel", "parallel"),
        name=name,
    )(w)


def _cast_t_kernel(x_ref, o_ref):
    o_ref[...] = jnp.transpose(x_ref[...]).astype(BF16)


def _cast_t_shift_kernel(a_ref, b_ref, o_ref):
    both = jnp.concatenate([a_ref[CAST_SHIFT:, :], b_ref[...]], axis=0)
    o_ref[...] = jnp.transpose(both).astype(BF16)


def _cast_t_head_kernel(x_ref, o_ref):
    xt = jnp.transpose(x_ref[...])
    lane = lax.broadcasted_iota(jnp.int32, xt.shape, 1)
    o_ref[...] = jnp.where(lane < CAST_SHIFT, xt, 0.0).astype(BF16)


def _cast_t_bf16(wt, layer, row0, n_rows, name):
    k = wt.shape[2]
    tile0 = row0 // CAST_TN
    return pl.pallas_call(
        _cast_t_kernel,
        grid=(k // CAST_ROWS, n_rows // CAST_TN),
        in_specs=[pl.BlockSpec((None, CAST_TN, CAST_ROWS), lambda r, j: (layer, tile0 + j, r))],
        out_specs=pl.BlockSpec((CAST_ROWS, CAST_TN), lambda r, j: (r, j)),
        out_shape=jax.ShapeDtypeStruct((k, n_rows), BF16),
        compiler_params=_params("parallel", "parallel"),
        name=name,
    )(wt)


def _cast_t_bf16_shifted(wt, layer, row0, name):
    n, k = wt.shape[1], wt.shape[2]
    n_rows = n - row0 - CAST_SHIFT
    tile0 = row0 // CAST_TN
    shifts_per_tile = CAST_TN // CAST_SHIFT
    return pl.pallas_call(
        _cast_t_shift_kernel,
        grid=(k // CAST_ROWS, n_rows // CAST_TN),
        in_specs=[pl.BlockSpec((None, CAST_TN, CAST_ROWS), lambda r, j: (layer, tile0 + j, r)),
                  pl.BlockSpec((None, CAST_SHIFT, CAST_ROWS),
                               lambda r, j: (layer, (tile0 + j + 1) * shifts_per_tile, r))],
        out_specs=pl.BlockSpec((CAST_ROWS, CAST_TN), lambda r, j: (r, j)),
        out_shape=jax.ShapeDtypeStruct((k, n_rows), BF16),
        compiler_params=_params("parallel", "parallel"),
        name=name,
    )(wt, wt)


def _cast_t_bf16_head(wt, layer, row0, name):
    k = wt.shape[2]
    return pl.pallas_call(
        _cast_t_head_kernel,
        grid=(k // CAST_ROWS,),
        in_specs=[pl.BlockSpec((None, LANES, CAST_ROWS), lambda r: (layer, row0 // LANES, r))],
        out_specs=pl.BlockSpec((CAST_ROWS, LANES), lambda r: (r, 0)),
        out_shape=jax.ShapeDtypeStruct((k, LANES), BF16),
        compiler_params=_params("parallel"),
        name=name,
    )(wt)


def _prepare_layer(layer, norm_gain, w_in, gla_gate_up, gla_gate_bias, gla_norm_gain, sgu_ln_gain, sgu_ln_bias,
                   sgu_w, sgu_b, q_norm_gain, k_norm_gain, sink, gate_bias, w_br, w_out):
    lr0 = 2 * GLA_QK_WIDTH + GLA_WIDTH
    lr1 = lr0 + 2 * GLA_GATE_RANK
    assert lr1 - lr0 == CAST_SHIFT and lr0 % CAST_TN == 0
    w_in_t = jnp.swapaxes(w_in, 1, 2)
    up2 = jnp.zeros((LANES, 2 * GLA_QK_WIDTH), F32)
    up2 = up2.at[:GLA_GATE_RANK, :GLA_QK_WIDTH].set(gla_gate_up[layer, 0])
    up2 = up2.at[GLA_GATE_RANK:2 * GLA_GATE_RANK, GLA_QK_WIDTH:].set(gla_gate_up[layer, 1])
    return dict(
        norm_gain=norm_gain[layer],
        w_qkv=_cast_t_bf16(w_in_t, layer, 0, lr0, "cast_w_qkv"),
        w_rest=_cast_t_bf16_shifted(w_in_t, layer, lr0, "cast_w_rest"),
        w_lr=_cast_t_bf16_head(w_in_t, layer, lr0, "cast_w_lr"), up2=up2.astype(BF16),
        bias2=gla_gate_bias[layer].reshape(1, 2 * GLA_QK_WIDTH), gla_norm_gain=gla_norm_gain[layer],
        sgu_ln_gain=sgu_ln_gain[layer], sgu_ln_bias=sgu_ln_bias[layer], sgu_w=sgu_w[layer], sgu_b=sgu_b[layer],
        q_gain=q_norm_gain[layer].reshape(1, HEAD_DIM), k_gain=k_norm_gain[layer].reshape(1, HEAD_DIM),
        sink=sink[layer], gate_bias=gate_bias[layer],
        w_br=_cast_bf16(w_br, layer, "cast_w_br"),
        w_out=_cast_bf16(w_out, layer, "cast_w_out"))


def _layer(x, p):
    batch, seq, _ = x.shape
    t = batch * seq
    x2 = x.reshape(t, D_MODEL)
    xn = _pre_norm(x2, p["norm_gain"])

    h_qkv = _proj_raw(xn, p["w_qkv"], "proj_gla_qkv")
    h_rest = _proj_rest(xn, p["w_rest"], p["q_gain"], p["k_gain"], seq)
    log_a, gla_exact = _gla_log_decay(xn, p["w_lr"], p["up2"], p["bias2"])

    o_a = _gla(h_qkv, h_rest, log_a, gla_exact, p["gla_norm_gain"], batch, seq)
    o_b = _sgu(h_rest, p["sgu_ln_gain"], p["sgu_ln_bias"], p["sgu_w"], p["sgu_b"])
    o_c = _attention(h_rest, p["sink"], batch, seq)

    merged = _merge(xn, o_a, o_b, o_c, p["w_rest"], p["w_br"], p["gate_bias"])
    return _out_proj(merged, p["w_out"], x2).reshape(batch, seq, D_MODEL)


def kernel(x_prompt, x_sample, norm_gain, w_in, gla_gate_up, gla_gate_bias, gla_norm_gain, sgu_ln_gain,
           sgu_ln_bias, sgu_w, sgu_b, q_norm_gain, k_norm_gain, sink, gate_bias, w_br, w_out):
    y_prompt, y_sample = x_prompt, x_sample
    for l in range(norm_gain.shape[0]):
        p = _prepare_layer(l, norm_gain, w_in, gla_gate_up, gla_gate_bias, gla_norm_gain, sgu_ln_gain,
                           sgu_ln_bias, sgu_w, sgu_b, q_norm_gain, k_norm_gain, sink, gate_bias, w_br, w_out)
        y_prompt = _layer(y_prompt, p)
        y_sample = _layer(y_sample, p)
    return (y_prompt, y_sample)
```

```python
import functools
import math

import jax
import jax.numpy as jnp
from jax import lax
from jax.experimental import pallas as pl
from jax.experimental.pallas import tpu as pltpu

F32 = jnp.float32
BF16 = jnp.bfloat16

D_MODEL = 4096
HEAD_DIM = 128
GLA_WIDTH = 1536
GLA_DV = 256
GLA_HEADS = 6
GLA_DK = 128
GLA_QK_WIDTH = 768
GLA_GATE_RANK = 16
GLA_GATE_TAU = 16.0
SGU_WIDTH = 1024
SGU_CHUNK = 128
SGU_GROUPS = 8
ATT_WIDTH = 1536
ATT_Q_HEADS = 12
ATT_KV_HEADS = 4
ATT_GROUP = 3
ATT_KV_WIDTH = 512
WINDOW = 128
ROPE_THETA = 500000.0
ROPE_DIMS = 32
N_BRANCH = 3
NORM_EPS = 1e-6
LN_EPS = 1e-5

VMEM_LIMIT_BYTES = 56 * 1024 * 1024
LANES = 128
SUBLANES = 8

FRONT_TM = 512
FRONT_TN = 768
FRONT_SUB_ROWS = 256
PROJ_REST_TM = 2048
PROJ_TN = 512
PROJ_SUB_ROWS = 256
GLA_ROWS = 256
GLA_HEADS_PER_STEP = 6
GLA_CHUNK = 64
GLA_FAST_MAX_DECAY = 40.0
GLA_PATH_WIDE, GLA_PATH_CHUNKED, GLA_PATH_EXACT = 0, 1, 2
SGU_ROWS = 512
ATTN_BLOCKS = 4
ATTN_LOOKAHEAD = 2
MERGE_TM = 512
MERGE_TN = 512
OUT_TM = 1024
OUT_TN = 1024
CAST_ROWS = 2048
CAST_TN = 512
CAST_SHIFT = 2 * GLA_GATE_RANK

KIND_SILU, KIND_GELU, KIND_QNORM, KIND_KNORM, KIND_RAW = range(5)
_REST_KIND = ([KIND_SILU] * 3 + [KIND_GELU] * 4 + [KIND_SILU] * 2 + [KIND_QNORM] * 3 + [KIND_KNORM, KIND_RAW]
              + [KIND_SILU] * 3)
_REST_OUT_TILE = [6, 7, 8, 0, 1, 2, 3, 4, 5, 9, 10, 11, 15, 16, 12, 13, 14]
REST_ACT_WIDTH = len(_REST_KIND) * PROJ_TN
H_SGU_U, H_SGU_V, H_SGU_GATE = 0, 1, 2
H_GLA_GATE, H_ATT_Q, H_ATT_GATE = 2, 3, 4
H_ATT_K, H_ATT_V = 15, 16


def _params(*sem):
    return pltpu.CompilerParams(dimension_semantics=sem, vmem_limit_bytes=VMEM_LIMIT_BYTES)


def _silu(x):
    return x * jax.nn.sigmoid(x)


def _gelu_tanh(x):
    c = math.sqrt(2.0 / math.pi)
    return x * (0.5 * (1.0 + jnp.tanh(c * (x + 0.044715 * (x * x * x)))))


def _dot_epilogue(x_ref, w_ref, o_ref, epilogue):
    w = w_ref[...]
    for r in range(x_ref.shape[0] // PROJ_SUB_ROWS):
        rows = pl.ds(r * PROJ_SUB_ROWS, PROJ_SUB_ROWS)
        acc = jnp.dot(x_ref[rows, :], w, preferred_element_type=F32)
        o_ref[rows, :] = epilogue(acc, rows).astype(BF16)


def _head_norm_rope(acc, gain, cos_t, sin_lo, sin_hi):
    outs = []
    for h in range(PROJ_TN // HEAD_DIM):
        xh = acc[:, h * HEAD_DIM:(h + 1) * HEAD_DIM]
        ms = jnp.mean(xh * xh, axis=-1, keepdims=True)
        y = xh * lax.rsqrt(ms + NORM_EPS) * gain
        half = ROPE_DIMS // 2
        up = pltpu.roll(y, HEAD_DIM - half, axis=1)
        dn = pltpu.roll(y, half, axis=1)
        outs.append(y * cos_t + up * sin_lo + dn * sin_hi)
    return jnp.concatenate(outs, axis=1)


def _proj_rest_kernel(kind_ref, tile_ref, x_ref, w_ref, qg_ref, kg_ref, cos_ref, slo_ref, shi_ref, o_ref):
    del tile_ref
    kind = kind_ref[pl.program_id(1)]

    def norm_rope(gain_ref):
        return lambda acc, rows: _head_norm_rope(acc, gain_ref[...], cos_ref[rows, :], slo_ref[rows, :],
                                                 shi_ref[rows, :])

    epilogues = {
        KIND_SILU: lambda acc, rows: _silu(acc),
        KIND_GELU: lambda acc, rows: _gelu_tanh(acc),
        KIND_QNORM: norm_rope(qg_ref),
        KIND_KNORM: norm_rope(kg_ref),
        KIND_RAW: lambda acc, rows: acc,
    }
    for k, epilogue in epilogues.items():
        pl.when(kind == k)(functools.partial(_dot_epilogue, x_ref, w_ref, o_ref, epilogue))


def _proj_rest(xn, w_rest, q_gain, k_gain, seq):
    t = xn.shape[0]
    tm = min(PROJ_REST_TM, t)
    blocks_per_seq = seq // tm if seq >= tm else 1
    tables = _rope_tables(seq)
    if seq < tm:
        tables = tuple(jnp.tile(tb, (tm // seq, 1)) for tb in tables)
    vec = pl.BlockSpec((1, HEAD_DIM), lambda i, j, kind, tile: (0, 0))
    tab = pl.BlockSpec((tm, HEAD_DIM), lambda i, j, kind, tile: (i % blocks_per_seq, 0))
    return pl.pallas_call(
        _proj_rest_kernel,
        grid_spec=pltpu.PrefetchScalarGridSpec(
            num_scalar_prefetch=2,
            grid=(t // tm, len(_REST_KIND)),
            in_specs=[pl.BlockSpec((tm, D_MODEL), lambda i, j, kind, tile: (i, 0)),
                      pl.BlockSpec((D_MODEL, PROJ_TN), lambda i, j, kind, tile: (0, j)),
                      vec, vec, tab, tab, tab],
            out_specs=pl.BlockSpec((tm, PROJ_TN), lambda i, j, kind, tile: (i, tile[j]))),
        out_shape=jax.ShapeDtypeStruct((t, REST_ACT_WIDTH), BF16),
        compiler_params=_params("parallel", "arbitrary"),
        name="proj_rest",
    )(jnp.asarray(_REST_KIND, jnp.int32), jnp.asarray(_REST_OUT_TILE, jnp.int32),
      xn, w_rest, q_gain, k_gain, *tables)


def _rope_tables(seq):
    inv_freq = ROPE_THETA ** (-jnp.arange(0, ROPE_DIMS, 2, dtype=F32) / ROPE_DIMS)
    ang = jnp.arange(seq, dtype=F32)[:, None] * inv_freq[None, :]
    cos, sin = jnp.cos(ang), jnp.sin(ang)
    half = ROPE_DIMS // 2
    ones = jnp.ones((seq, HEAD_DIM - ROPE_DIMS), F32)
    zeros = jnp.zeros((seq, HEAD_DIM - ROPE_DIMS), F32)
    zh = jnp.zeros((seq, half), F32)
    cos_t = jnp.concatenate([cos, cos, ones], axis=1)
    sin_lo = jnp.concatenate([-sin, zh, zeros], axis=1)
    sin_hi = jnp.concatenate([zh, sin, zeros], axis=1)
    return cos_t, sin_lo, sin_hi


def _log_decay(xn, wlr_ref, up_ref, bias_ref):
    lr = jnp.dot(xn, wlr_ref[...], preferred_element_type=F32)
    z = jnp.dot(lr.astype(BF16), up_ref[...], preferred_element_type=F32) + bias_ref[...]
    return (jnp.minimum(z, 0.0) - jnp.log(1.0 + jnp.exp(-jnp.abs(z)))) * (1.0 / GLA_GATE_TAU)


def _front_kernel(x_ref, gain_ref, w_ref, wlr_ref, up_ref, bias_ref, h_ref, xn_ref, la_ref, mx_ref):
    full_max = lambda v: jnp.broadcast_to(jnp.max(jnp.max(v, axis=1, keepdims=True), axis=0, keepdims=True),
                                          (1, LANES))

    @pl.when(pl.program_id(1) == 0)
    def _():
        w = w_ref[...]
        n_blocks = x_ref.shape[0] // GLA_ROWS
        step_max, block_max = [], []
        for r in range(n_blocks):
            largest, total = None, None
            for s in range(GLA_ROWS // FRONT_SUB_ROWS):
                rows = pl.ds(r * GLA_ROWS + s * FRONT_SUB_ROWS, FRONT_SUB_ROWS)
                x = x_ref[rows, :]
                ms = jnp.mean(x * x, axis=-1, keepdims=True)
                xn = (x * lax.rsqrt(ms + NORM_EPS) * gain_ref[...]).astype(BF16)
                xn_ref[rows, :] = xn
                h_ref[rows, :] = jnp.dot(xn, w, preferred_element_type=F32).astype(BF16)
                log_a = _log_decay(xn, wlr_ref, up_ref, bias_ref)
                la_ref[rows, :] = log_a
                piece_max = jnp.max(-log_a, axis=0, keepdims=True)
                piece_sum = jnp.sum(-log_a, axis=0, keepdims=True)
                largest = piece_max if largest is None else jnp.maximum(largest, piece_max)
                total = piece_sum if total is None else total + piece_sum
            step_max.append(full_max(largest))
            block_max.append(full_max(total))
        zero = [jnp.zeros((1, LANES), F32)] * (SUBLANES // 2 - n_blocks)
        mx_ref[0] = jnp.concatenate(step_max + zero + block_max + zero, axis=0)

    @pl.when(pl.program_id(1) > 0)
    def _():
        _dot_epilogue(xn_ref, w_ref, h_ref, lambda acc, rows: acc)


def _front(x2, gain, w_qkv, w_lr, up2, bias2):
    t = x2.shape[0]
    n = w_qkv.shape[1]
    tm = min(FRONT_TM, t)
    n_dec = 2 * GLA_QK_WIDTH
    const = lambda shape: pl.BlockSpec(shape, lambda i, j: (0, 0))
    h_qkv, xn, log_a, mx = pl.pallas_call(
        _front_kernel,
        grid=(t // tm, n // FRONT_TN),
        in_specs=[pl.BlockSpec((tm, D_MODEL), lambda i, j: (i, 0)),
                  const((1, D_MODEL)),
                  pl.BlockSpec((D_MODEL, FRONT_TN), lambda i, j: (0, j)),
                  const((D_MODEL, LANES)), const((LANES, n_dec)), const((1, n_dec))],
        out_specs=[pl.BlockSpec((tm, FRONT_TN), lambda i, j: (i, j)),
                   pl.BlockSpec((tm, D_MODEL), lambda i, j: (i, 0)),
                   pl.BlockSpec((tm, n_dec), lambda i, j: (i, 0)),
                   pl.BlockSpec((1, SUBLANES, LANES), lambda i, j: (i, 0, 0))],
        out_shape=[jax.ShapeDtypeStruct((t, n), BF16),
                   jax.ShapeDtypeStruct((t, D_MODEL), BF16),
                   jax.ShapeDtypeStruct((t, n_dec), F32),
                   jax.ShapeDtypeStruct((t // tm, SUBLANES, LANES), F32)],
        compiler_params=_params("parallel", "arbitrary"),
        name="front_norm_qkv_decay",
    )(x2, gain.reshape(1, D_MODEL), w_qkv, w_lr, up2, bias2)
    n_blocks = tm // GLA_ROWS
    step_max = mx[:, :n_blocks, 0].reshape(t // GLA_ROWS)
    block_max = mx[:, SUBLANES // 2:SUBLANES // 2 + n_blocks, 0].reshape(t // GLA_ROWS)
    path = jnp.where(block_max <= GLA_FAST_MAX_DECAY, GLA_PATH_WIDE,
                     jnp.where(step_max <= GLA_FAST_MAX_DECAY / GLA_CHUNK, GLA_PATH_CHUNKED, GLA_PATH_EXACT))
    return h_qkv, xn, log_a, path.astype(jnp.int32)


def _gla_head_rows(q, k, v, b, state, mask, chunk_rows, reverse, exact_scratch):
    n_chunks = q.shape[0] // chunk_rows
    chunk = lambda x, c: x[c * chunk_rows:(c + 1) * chunk_rows]
    edge = 0 if reverse else chunk_rows - 1
    b_tot = [chunk(b, c)[edge:edge + 1, :] for c in range(n_chunks)]
    dec_row = [jnp.exp(bt) for bt in b_tot]
    qb = (q * jnp.exp(b)).astype(BF16)

    if exact_scratch is None:
        k_inv = k * jnp.exp(-b)
        s = lax.dot_general(qb, k_inv.astype(BF16), (((1,), (1,)), ((), ())), preferred_element_type=F32)
        o = jnp.dot(jnp.where(mask, s, 0.0).astype(BF16), v, preferred_element_type=F32)
        kb = [chunk(k_inv, c) * dec_row[c] for c in range(n_chunks)]
    else:
        b_scr, k_scr = exact_scratch
        col = lax.broadcasted_iota(jnp.int32, (chunk_rows, chunk_rows), 1)
        o_chunks, kb = [], []
        for c in range(n_chunks):
            qc, bc = chunk(q, c), chunk(b, c)
            b_scr[...] = bc
            k_scr[...] = chunk(k, c)

            def column(j, acc, qc=qc, bc=bc):
                decay = jnp.exp(jnp.minimum(bc - b_scr[pl.ds(j, 1), :], 0.0))
                w = qc * decay * k_scr[pl.ds(j, 1), :]
                return jnp.where(col == j, jnp.sum(w, axis=-1, keepdims=True), acc)

            s = lax.fori_loop(0, chunk_rows, column, jnp.zeros((chunk_rows, chunk_rows), F32))
            lo = c * chunk_rows
            p = jnp.where(mask[lo:lo + chunk_rows, lo:lo + chunk_rows], s, 0.0)
            o_chunks.append(jnp.dot(p.astype(BF16), chunk(v, c), preferred_element_type=F32))
            kb.append(chunk(k, c) * jnp.exp(b_tot[c] - bc))
        o = jnp.concatenate(o_chunks, axis=0)

    o_state = [None] * n_chunks
    for c in (range(n_chunks - 1, -1, -1) if reverse else range(n_chunks)):
        o_state[c] = jnp.dot(chunk(qb, c), state.astype(BF16), preferred_element_type=F32)
        upd = lax.dot_general(kb[c].astype(BF16), chunk(v, c), (((0,), (0,)), ((), ())),
                              preferred_element_type=F32)
        dec = jnp.transpose(jnp.broadcast_to(dec_row[c], (GLA_DK, GLA_DK)))
        state = jnp.concatenate([dec] * (GLA_DV // GLA_DK), axis=1) * state + upd
    return o + jnp.concatenate(o_state, axis=0), state


def _gla_scan_rows(q_ref, k_ref, v_ref, g_ref, state_ref, chunk_rows, reverse, emit, exact_scratch):
    row = lax.broadcasted_iota(jnp.int32, (GLA_ROWS, GLA_ROWS), 0)
    col = lax.broadcasted_iota(jnp.int32, (GLA_ROWS, GLA_ROWS), 1)
    shift = chunk_rows.bit_length() - 1
    mask = jnp.logical_and((col >= row) if reverse else (col <= row),
                           lax.shift_right_logical(row, shift) == lax.shift_right_logical(col, shift))
    tri_b = mask.astype(BF16)
    rest = g_ref[...]
    b_all = None
    for _ in range(3):
        term = rest.astype(BF16)
        rest = rest - term.astype(F32)
        part = jnp.dot(tri_b, term, preferred_element_type=F32)
        b_all = part if b_all is None else b_all + part
    for h in range(GLA_HEADS_PER_STEP):
        kc = pl.ds(h * GLA_DK, GLA_DK)
        vc = pl.ds(h * GLA_DV, GLA_DV)
        q = q_ref[:, kc].astype(F32) * (GLA_DK ** -0.5)
        k = k_ref[:, kc].astype(F32)
        o, new_state = _gla_head_rows(q, k, v_ref[:, vc], b_all[:, h * GLA_DK:(h + 1) * GLA_DK], state_ref[h],
                                      mask, chunk_rows, reverse, exact_scratch)
        state_ref[h] = new_state
        emit(vc, o)


def _gla_scan_step(path_ref, row_block, refs, state_ref, b_scr, k_scr, reverse, emit):
    @pl.when(pl.program_id(2) == 0)
    def _():
        state_ref[...] = jnp.zeros_like(state_ref)

    path = path_ref[row_block]

    @pl.when(path == GLA_PATH_WIDE)
    def _():
        _gla_scan_rows(*refs, state_ref, GLA_ROWS, reverse, emit, None)

    @pl.when(path == GLA_PATH_CHUNKED)
    def _():
        _gla_scan_rows(*refs, state_ref, GLA_CHUNK, reverse, emit, None)

    @pl.when(path == GLA_PATH_EXACT)
    def _():
        _gla_scan_rows(*refs, state_ref, GLA_CHUNK, reverse, emit, (b_scr, k_scr))


def _gla_fwd_kernel(exact_ref, q_ref, k_ref, v_ref, g_ref, o_ref, state_ref, b_scr, k_scr, *, nb):
    def emit(cols, o):
        o_ref[:, cols] = o

    row_block = pl.program_id(0) * nb + pl.program_id(2)
    _gla_scan_step(exact_ref, row_block, (q_ref, k_ref, v_ref, g_ref), state_ref, b_scr, k_scr, False, emit)


def _gla_bwd_kernel(exact_ref, q_ref, k_ref, v_ref, g_ref, fwd_ref, gate_ref, gain_ref, o_ref,
                    state_ref, b_scr, k_scr, *, nb):
    def emit(cols, o):
        tot = fwd_ref[:, cols] + o
        ms = jnp.mean(tot * tot, axis=-1, keepdims=True)
        y = tot * lax.rsqrt(ms + NORM_EPS) * gain_ref[...]
        o_ref[:, cols] = (y * gate_ref[:, cols].astype(F32)).astype(BF16)

    row_block = pl.program_id(0) * nb + (nb - 1 - pl.program_id(2))
    _gla_scan_step(exact_ref, row_block, (q_ref, k_ref, v_ref, g_ref), state_ref, b_scr, k_scr, True, emit)


def _gla(h_qkv, h_rest, log_a, exact, gain, batch, seq):
    t = batch * seq
    nb = seq // GLA_ROWS
    hp = GLA_HEADS_PER_STEP
    qk_w, v_w = hp * GLA_DK, hp * GLA_DV
    k_off = GLA_QK_WIDTH // qk_w
    v_off = 2 * GLA_QK_WIDTH // v_w
    gate_off = H_GLA_GATE * GLA_WIDTH // v_w
    dir_off = GLA_QK_WIDTH // qk_w
    grid = (batch, GLA_HEADS // hp, nb)
    scratch = [pltpu.VMEM((hp, GLA_DK, GLA_DV), F32),
               pltpu.VMEM((GLA_CHUNK, GLA_DK), F32), pltpu.VMEM((GLA_CHUNK, GLA_DK), F32)]

    def specs(rowmap, direction):
        return [pl.BlockSpec((GLA_ROWS, qk_w), lambda b, h, n, e: (rowmap(b, n), h)),
                pl.BlockSpec((GLA_ROWS, qk_w), lambda b, h, n, e: (rowmap(b, n), k_off + h)),
                pl.BlockSpec((GLA_ROWS, v_w), lambda b, h, n, e: (rowmap(b, n), v_off + h)),
                pl.BlockSpec((GLA_ROWS, qk_w), lambda b, h, n, e: (rowmap(b, n), direction * dir_off + h))]

    fmap = lambda b, n: b * nb + n
    rmap = lambda b, n: b * nb + (nb - 1 - n)
    fwd = pl.pallas_call(
        functools.partial(_gla_fwd_kernel, nb=nb),
        grid_spec=pltpu.PrefetchScalarGridSpec(
            num_scalar_prefetch=1,
            grid=grid,
            in_specs=specs(fmap, 0),
            out_specs=pl.BlockSpec((GLA_ROWS, v_w), lambda b, h, n, e: (fmap(b, n), h)),
            scratch_shapes=scratch),
        out_shape=jax.ShapeDtypeStruct((t, GLA_WIDTH), F32),
        compiler_params=_params("parallel", "parallel", "arbitrary"),
        name="gla_fwd",
    )(exact, h_qkv, h_qkv, h_qkv, log_a)
    return pl.pallas_call(
        functools.partial(_gla_bwd_kernel, nb=nb),
        grid_spec=pltpu.PrefetchScalarGridSpec(
            num_scalar_prefetch=1,
            grid=grid,
            in_specs=specs(rmap, 1) + [
                pl.BlockSpec((GLA_ROWS, v_w), lambda b, h, n, e: (rmap(b, n), h)),
                pl.BlockSpec((GLA_ROWS, v_w), lambda b, h, n, e: (rmap(b, n), gate_off + h)),
                pl.BlockSpec((1, GLA_DV), lambda b, h, n, e: (0, 0))],
            out_specs=pl.BlockSpec((GLA_ROWS, v_w), lambda b, h, n, e: (rmap(b, n), h)),
            scratch_shapes=scratch),
        out_shape=jax.ShapeDtypeStruct((t, GLA_WIDTH), BF16),
        compiler_params=_params("parallel", "parallel", "arbitrary"),
        name="gla_bwd",
    )(exact, h_qkv, h_qkv, h_qkv, log_a, fwd, h_rest, gain.reshape(1, GLA_DV))


def _sgu_kernel(u_ref, v_ref, gate_ref, lng_ref, lnb_ref, w_ref, bt_ref, o_ref):
    for c in range(SGU_ROWS // SGU_CHUNK):
        rows = pl.ds(c * SGU_CHUNK, SGU_CHUNK)
        x = v_ref[rows, :].astype(F32)
        mu = jnp.mean(x, axis=-1, keepdims=True)
        xc = x - mu
        var = jnp.mean(xc * xc, axis=-1, keepdims=True)
        y = (xc * lax.rsqrt(var + LN_EPS) * lng_ref[...] + lnb_ref[...]).astype(BF16)
        for g in range(SGU_GROUPS):
            cols = pl.ds(g * LANES, LANES)
            mixed = jnp.dot(w_ref[g], y[:, g * LANES:(g + 1) * LANES], preferred_element_type=F32)
            mixed = mixed + bt_ref[:, g:g + 1]
            o_ref[rows, cols] = (u_ref[rows, cols].astype(F32) * mixed
                                 * gate_ref[rows, cols].astype(F32)).astype(BF16)


def _sgu(h_rest, ln_gain, ln_bias, w, b):
    t = h_rest.shape[0]
    blk = lambda c: pl.BlockSpec((SGU_ROWS, SGU_WIDTH), lambda i: (i, c))
    return pl.pallas_call(
        _sgu_kernel,
        grid=(t // SGU_ROWS,),
        in_specs=[blk(H_SGU_U), blk(H_SGU_V), blk(H_SGU_GATE),
                  pl.BlockSpec((1, SGU_WIDTH), lambda i: (0, 0)),
                  pl.BlockSpec((1, SGU_WIDTH), lambda i: (0, 0)),
                  pl.BlockSpec((SGU_GROUPS, SGU_CHUNK, SGU_CHUNK), lambda i: (0, 0, 0)),
                  pl.BlockSpec((SGU_CHUNK, SGU_GROUPS), lambda i: (0, 0))],
        out_specs=pl.BlockSpec((SGU_ROWS, SGU_WIDTH), lambda i: (i, 0)),
        out_shape=jax.ShapeDtypeStruct((t, SGU_WIDTH), BF16),
        compiler_params=_params("parallel"),
        name="sgu",
    )(h_rest, h_rest, h_rest, ln_gain.reshape(1, SGU_WIDTH), ln_bias.reshape(1, SGU_WIDTH),
      w.astype(BF16), jnp.transpose(b))


def _attn_kernel(sink_ref, q_ref, gate_ref, kp_ref, kc_ref, kn_ref, vp_ref, vc_ref, vn_ref, o_ref, *, n_steps):
    n = pl.program_id(1)
    log2e = math.log2(math.e)
    qpos = lax.broadcasted_iota(jnp.int32, (WINDOW, 3 * WINDOW), 0)
    kpos = lax.broadcasted_iota(jnp.int32, (WINDOW, 3 * WINDOW), 1) - WINDOW
    band = jnp.abs(kpos - qpos) <= WINDOW
    full = slice(None)
    block = lambda c: pl.ds(c * WINDOW, WINDOW)

    items = []
    for a in range(ATTN_BLOCKS):
        prev = (kp_ref, vp_ref, full) if a == 0 else (kc_ref, vc_ref, block(a - 1))
        nxt = (kn_ref, vn_ref, full) if a == ATTN_BLOCKS - 1 else (kc_ref, vc_ref, block(a + 1))
        keys = [(prev[0], prev[2]), (kc_ref, block(a)), (nxt[0], nxt[2])]
        vals = [(prev[1], prev[2]), (vc_ref, block(a)), (nxt[1], nxt[2])]
        valid = band
        if a == 0:
            valid = jnp.logical_and(valid, jnp.logical_or(kpos >= 0, n > 0))
        if a == ATTN_BLOCKS - 1:
            valid = jnp.logical_and(valid, jnp.logical_or(kpos < WINDOW, n < n_steps - 1))
        valid = jnp.concatenate([valid] * ATT_GROUP, axis=0)
        items += [(block(a), keys, vals, valid, hk) for hk in range(ATT_KV_HEADS)]

    def scores(item):
        rows, keys, _, _, hk = item
        kcols = pl.ds(hk * HEAD_DIM, HEAD_DIM)
        q = jnp.concatenate([q_ref[rows, pl.ds((hk * ATT_GROUP + g) * HEAD_DIM, HEAD_DIM)]
                             for g in range(ATT_GROUP)], axis=0)
        k = jnp.concatenate([ref[r, kcols] for ref, r in keys], axis=0)
        return lax.dot_general(q, k, (((1,), (1,)), ((), ())), preferred_element_type=F32)

    def finish(item, s):
        rows, _, vals, valid, hk = item
        kcols = pl.ds(hk * HEAD_DIM, HEAD_DIM)
        v = jnp.concatenate([ref[r, kcols] for ref, r in vals], axis=0)
        s = jnp.where(valid, s * (HEAD_DIM ** -0.5 * log2e), -jnp.inf)
        sink = jnp.concatenate([jnp.full((WINDOW, 1), sink_ref[hk * ATT_GROUP + g] * log2e, F32)
                                for g in range(ATT_GROUP)], axis=0)
        m = jnp.maximum(jnp.max(s, axis=-1, keepdims=True), sink)
        e = jnp.exp2(s - m)
        denom = jnp.sum(e, axis=-1, keepdims=True) + jnp.exp2(sink - m)
        o = jnp.dot(e.astype(BF16), v, preferred_element_type=F32) / denom
        for g in range(ATT_GROUP):
            cols = pl.ds((hk * ATT_GROUP + g) * HEAD_DIM, HEAD_DIM)
            o_ref[rows, cols] = (o[g * WINDOW:(g + 1) * WINDOW, :] * gate_ref[rows, cols].astype(F32)).astype(BF16)

    pending = [scores(item) for item in items[:ATTN_LOOKAHEAD]]
    for i, item in enumerate(items):
        if i + ATTN_LOOKAHEAD < len(items):
            pending.append(scores(items[i + ATTN_LOOKAHEAD]))
        finish(item, pending.pop(0))


def _attention(h_rest, sink, batch, seq):
    t = batch * seq
    rows = ATTN_BLOCKS * WINDOW
    n_steps = seq // rows
    nb = seq // WINDOW
    cur = lambda b, n: b * n_steps + n
    prv = lambda b, n: b * nb + jnp.maximum(n * ATTN_BLOCKS - 1, 0)
    nxt = lambda b, n: b * nb + jnp.minimum((n + 1) * ATTN_BLOCKS, nb - 1)
    wide = lambda blk: pl.BlockSpec((rows, ATT_WIDTH), lambda b, n: (cur(b, n), blk))
    kv_cur = lambda blk: pl.BlockSpec((rows, ATT_KV_WIDTH), lambda b, n: (cur(b, n), blk))
    kv_edge = lambda rowmap, blk: pl.BlockSpec((WINDOW, ATT_KV_WIDTH), lambda b, n: (rowmap(b, n), blk))
    return pl.pallas_call(
        functools.partial(_attn_kernel, n_steps=n_steps),
        grid=(batch, n_steps),
        in_specs=[pl.BlockSpec(memory_space=pltpu.SMEM),
                  wide(H_ATT_Q), wide(H_ATT_GATE),
                  kv_edge(prv, H_ATT_K), kv_cur(H_ATT_K), kv_edge(nxt, H_ATT_K),
                  kv_edge(prv, H_ATT_V), kv_cur(H_ATT_V), kv_edge(nxt, H_ATT_V)],
        out_specs=pl.BlockSpec((rows, ATT_WIDTH), lambda b, n: (cur(b, n), 0)),
        out_shape=jax.ShapeDtypeStruct((t, ATT_WIDTH), BF16),
        compiler_params=_params("parallel", "parallel"),
        name="window_attention",
    )(sink, h_rest, h_rest, h_rest, h_rest, h_rest, h_rest, h_rest, h_rest)


def _merge_kernel(x_ref, oa_ref, ob_ref, oc_ref, g0_ref, g1_ref, g2_ref, wbr_ref, bias_ref, o_ref):
    x = x_ref[...]
    acc = None
    row0 = 0
    for i, (g_ref, o_in) in enumerate(((g0_ref, oa_ref), (g1_ref, ob_ref), (g2_ref, oc_ref))):
        width = o_in.shape[1]
        gate = jax.nn.sigmoid(jnp.dot(x, g_ref[...], preferred_element_type=F32) + bias_ref[i:i + 1, :])
        term = gate * jnp.dot(o_in[...], wbr_ref[row0:row0 + width, :], preferred_element_type=F32)
        acc = term if acc is None else acc + term
        row0 += width
    o_ref[...] = acc.astype(BF16)


def _merge(xn, o_a, o_b, o_c, w_rest, w_br, gate_bias):
    t = xn.shape[0]
    tm = min(MERGE_TM, t)
    nj = D_MODEL // MERGE_TN
    g0 = REST_ACT_WIDTH // MERGE_TN
    row = lambda width: pl.BlockSpec((tm, width), lambda i, j: (i, 0))
    gcol = lambda br: pl.BlockSpec((D_MODEL, MERGE_TN), lambda i, j: (0, g0 + br * nj + j))
    return pl.pallas_call(
        _merge_kernel,
        grid=(t // tm, nj),
        in_specs=[row(D_MODEL), row(GLA_WIDTH), row(SGU_WIDTH), row(ATT_WIDTH),
                  gcol(0), gcol(1), gcol(2),
                  pl.BlockSpec((GLA_WIDTH + SGU_WIDTH + ATT_WIDTH, MERGE_TN), lambda i, j: (0, j)),
                  pl.BlockSpec((N_BRANCH, MERGE_TN), lambda i, j: (0, j))],
        out_specs=pl.BlockSpec((tm, MERGE_TN), lambda i, j: (i, j)),
        out_shape=jax.ShapeDtypeStruct((t, D_MODEL), BF16),
        compiler_params=_params("parallel", "arbitrary"),
        name="gated_merge",
    )(xn, o_a, o_b, o_c, w_rest, w_rest, w_rest, w_br, gate_bias)


def _out_kernel(m_ref, w_ref, x_ref, o_ref):
    o_ref[...] = x_ref[...] + jnp.dot(m_ref[...], w_ref[...], preferred_element_type=F32)


def _out_proj(merged, w_out, x2):
    t = x2.shape[0]
    tm = min(OUT_TM, t)
    return pl.pallas_call(
        _out_kernel,
        grid=(t // tm, D_MODEL // OUT_TN),
        in_specs=[pl.BlockSpec((tm, D_MODEL), lambda i, j: (i, 0)),
                  pl.BlockSpec((D_MODEL, OUT_TN), lambda i, j: (0, j)),
                  pl.BlockSpec((tm, OUT_TN), lambda i, j: (i, j))],
        out_specs=pl.BlockSpec((tm, OUT_TN), lambda i, j: (i, j)),
        out_shape=jax.ShapeDtypeStruct((t, D_MODEL), F32),
        compiler_params=_params("parallel", "arbitrary"),
        name="out_proj",
    )(merged, w_out, x2)


def _cast_kernel(x_ref, o_ref):
    o_ref[...] = x_ref[...].astype(BF16)


def _cast_bf16(w, layer, name):
    rows, cols = w.shape[1], w.shape[2]
    return pl.pallas_call(
        _cast_kernel,
        grid=(rows // CAST_ROWS, cols // CAST_TN),
        in_specs=[pl.BlockSpec((None, CAST_ROWS, CAST_TN), lambda r, j: (layer, r, j))],
        out_specs=pl.BlockSpec((CAST_ROWS, CAST_TN), lambda r, j: (r, j)),
        out_shape=jax.ShapeDtypeStruct((rows, cols), BF16),
        compiler_params=_params("parallel", "parallel"),
        name=name,
    )(w)


def _cast_t_kernel(x_ref, o_ref):
    o_ref[...] = jnp.transpose(x_ref[...]).astype(BF16)


def _cast_t_shift_kernel(a_ref, b_ref, o_ref):
    both = jnp.concatenate([a_ref[CAST_SHIFT:, :], b_ref[...]], axis=0)
    o_ref[...] = jnp.transpose(both).astype(BF16)


def _cast_t_head_kernel(x_ref, o_ref):
    xt = jnp.transpose(x_ref[...])
    lane = lax.broadcasted_iota(jnp.int32, xt.shape, 1)
    o_ref[...] = jnp.where(lane < CAST_SHIFT, xt, 0.0).astype(BF16)


def _cast_t_bf16(wt, layer, row0, n_rows, name):
    k = wt.shape[2]
    tile0 = row0 // CAST_TN
    return pl.pallas_call(
        _cast_t_kernel,
        grid=(k // CAST_ROWS, n_rows // CAST_TN),
        in_specs=[pl.BlockSpec((None, CAST_TN, CAST_ROWS), lambda r, j: (layer, tile0 + j, r))],
        out_specs=pl.BlockSpec((CAST_ROWS, CAST_TN), lambda r, j: (r, j)),
        out_shape=jax.ShapeDtypeStruct((k, n_rows), BF16),
        compiler_params=_params("parallel", "parallel"),
        name=name,
    )(wt)


def _cast_t_bf16_shifted(wt, layer, row0, name):
    n, k = wt.shape[1], wt.shape[2]
    n_rows = n - row0 - CAST_SHIFT
    tile0 = row0 // CAST_TN
    shifts_per_tile = CAST_TN // CAST_SHIFT
    return pl.pallas_call(
        _cast_t_shift_kernel,
        grid=(k // CAST_ROWS, n_rows // CAST_TN),
        in_specs=[pl.BlockSpec((None, CAST_TN, CAST_ROWS), lambda r, j: (layer, tile0 + j, r)),
                  pl.BlockSpec((None, CAST_SHIFT, CAST_ROWS),
                               lambda r, j: (layer, (tile0 + j + 1) * shifts_per_tile, r))],
        out_specs=pl.BlockSpec((CAST_ROWS, CAST_TN), lambda r, j: (r, j)),
        out_shape=jax.ShapeDtypeStruct((k, n_rows), BF16),
        compiler_params=_params("parallel", "parallel"),
        name=name,
    )(wt, wt)


def _cast_t_bf16_head(wt, layer, row0, name):
    k = wt.shape[2]
    return pl.pallas_call(
        _cast_t_head_kernel,
        grid=(k // CAST_ROWS,),
        in_specs=[pl.BlockSpec((None, LANES, CAST_ROWS), lambda r: (layer, row0 // LANES, r))],
        out_specs=pl.BlockSpec((CAST_ROWS, LANES), lambda r: (r, 0)),
        out_shape=jax.ShapeDtypeStruct((k, LANES), BF16),
        compiler_params=_params("parallel"),
        name=name,
    )(wt)


def _prepare_layer(layer, norm_gain, w_in, gla_gate_up, gla_gate_bias, gla_norm_gain, sgu_ln_gain, sgu_ln_bias,
                   sgu_w, sgu_b, q_norm_gain, k_norm_gain, sink, gate_bias, w_br, w_out):
    lr0 = 2 * GLA_QK_WIDTH + GLA_WIDTH
    lr1 = lr0 + 2 * GLA_GATE_RANK
    assert lr1 - lr0 == CAST_SHIFT and lr0 % CAST_TN == 0
    w_in_t = jnp.swapaxes(w_in, 1, 2)
    up2 = jnp.zeros((LANES, 2 * GLA_QK_WIDTH), F32)
    up2 = up2.at[:GLA_GATE_RANK, :GLA_QK_WIDTH].set(gla_gate_up[layer, 0])
    up2 = up2.at[GLA_GATE_RANK:2 * GLA_GATE_RANK, GLA_QK_WIDTH:].set(gla_gate_up[layer, 1])
    return dict(
        norm_gain=norm_gain[layer],
        w_qkv=_cast_t_bf16(w_in_t, layer, 0, lr0, "cast_w_qkv"),
        w_rest=_cast_t_bf16_shifted(w_in_t, layer, lr0, "cast_w_rest"),
        w_lr=_cast_t_bf16_head(w_in_t, layer, lr0, "cast_w_lr"), up2=up2.astype(BF16),
        bias2=gla_gate_bias[layer].reshape(1, 2 * GLA_QK_WIDTH), gla_norm_gain=gla_norm_gain[layer],
        sgu_ln_gain=sgu_ln_gain[layer], sgu_ln_bias=sgu_ln_bias[layer], sgu_w=sgu_w[layer], sgu_b=sgu_b[layer],
        q_gain=q_norm_gain[layer].reshape(1, HEAD_DIM), k_gain=k_norm_gain[layer].reshape(1, HEAD_DIM),
        sink=sink[layer], gate_bias=gate_bias[layer],
        w_br=_cast_bf16(w_br, layer, "cast_w_br"),
        w_out=_cast_bf16(w_out, layer, "cast_w_out"))


def _layer(x, p):
    batch, seq, _ = x.shape
    t = batch * seq
    x2 = x.reshape(t, D_MODEL)
    h_qkv, xn, log_a, gla_exact = _front(x2, p["norm_gain"], p["w_qkv"], p["w_lr"], p["up2"], p["bias2"])
    h_rest = _proj_rest(xn, p["w_rest"], p["q_gain"], p["k_gain"], seq)

    o_a = _gla(h_qkv, h_rest, log_a, gla_exact, p["gla_norm_gain"], batch, seq)
    o_b = _sgu(h_rest, p["sgu_ln_gain"], p["sgu_ln_bias"], p["sgu_w"], p["sgu_b"])
    o_c = _attention(h_rest, p["sink"], batch, seq)

    merged = _merge(xn, o_a, o_b, o_c, p["w_rest"], p["w_br"], p["gate_bias"])
    return _out_proj(merged, p["w_out"], x2).reshape(batch, seq, D_MODEL)


def kernel(x_prompt, x_sample, norm_gain, w_in, gla_gate_up, gla_gate_bias, gla_norm_gain, sgu_ln_gain,
           sgu_ln_bias, sgu_w, sgu_b, q_norm_gain, k_norm_gain, sink, gate_bias, w_br, w_out):
    y_prompt, y_sample = x_prompt, x_sample
    for l in range(norm_gain.shape[0]):
        p = _prepare_layer(l, norm_gain, w_in, gla_gate_up, gla_gate_bias, gla_norm_gain, sgu_ln_gain,
                           sgu_ln_bias, sgu_w, sgu_b, q_norm_gain, k_norm_gain, sink, gate_bias, w_br, w_out)
        y_prompt = _layer(y_prompt, p)
        y_sample = _layer(y_sample, p)
    return (y_prompt, y_sample)
```

```python
import functools
import math

import jax
import jax.numpy as jnp
from jax import lax
from jax.experimental import pallas as pl
from jax.experimental.pallas import tpu as pltpu

F32 = jnp.float32
BF16 = jnp.bfloat16

D_MODEL = 4096
HEAD_DIM = 128
GLA_WIDTH = 1536
GLA_DV = 256
GLA_HEADS = 6
GLA_DK = 128
GLA_QK_WIDTH = 768
GLA_GATE_RANK = 16
GLA_GATE_TAU = 16.0
SGU_WIDTH = 1024
SGU_CHUNK = 128
SGU_GROUPS = 8
ATT_WIDTH = 1536
ATT_Q_HEADS = 12
ATT_KV_HEADS = 4
ATT_GROUP = 3
ATT_KV_WIDTH = 512
WINDOW = 128
ROPE_THETA = 500000.0
ROPE_DIMS = 32
N_BRANCH = 3
NORM_EPS = 1e-6
LN_EPS = 1e-5

VMEM_LIMIT_BYTES = 56 * 1024 * 1024
LANES = 128
SUBLANES = 8

NORM_ROWS = 256
PROJ_TM = 1024
PROJ_REST_TM = 2048
PROJ_TN = 512
PROJ_RAW_TN = 1024
PROJ_SUB_ROWS = 256
GLA_ROWS = 256
GLA_STEP_BLOCKS = 2
GLA_HEADS_PER_STEP = 6
GLA_CHUNK = 64
GLA_FAST_MAX_DECAY = 40.0
GLA_PATH_WIDE, GLA_PATH_CHUNKED, GLA_PATH_EXACT = 0, 1, 2
SGU_ROWS = 512
ATTN_BLOCKS = 4
ATTN_LOOKAHEAD = 2
MERGE_TM = 512
MERGE_TN = 512
OUT_TM = 1024
OUT_TN = 1024
CAST_ROWS = 2048
CAST_TN = 512
CAST_SHIFT = 2 * GLA_GATE_RANK

KIND_SILU, KIND_GELU, KIND_QNORM, KIND_KNORM, KIND_RAW = range(5)
_REST_KIND = ([KIND_SILU] * 3 + [KIND_GELU] * 4 + [KIND_SILU] * 2 + [KIND_QNORM] * 3 + [KIND_KNORM, KIND_RAW]
              + [KIND_SILU] * 3)
_REST_OUT_TILE = [6, 7, 8, 0, 1, 2, 3, 4, 5, 9, 10, 11, 15, 16, 12, 13, 14]
REST_ACT_WIDTH = len(_REST_KIND) * PROJ_TN
H_SGU_U, H_SGU_V, H_SGU_GATE = 0, 1, 2
H_GLA_GATE, H_ATT_Q, H_ATT_GATE = 2, 3, 4
H_ATT_K, H_ATT_V = 15, 16


def _params(*sem):
    return pltpu.CompilerParams(dimension_semantics=sem, vmem_limit_bytes=VMEM_LIMIT_BYTES)


def _silu(x):
    return x * jax.nn.sigmoid(x)


def _gelu_tanh(x):
    c = math.sqrt(2.0 / math.pi)
    return x * (0.5 * (1.0 + jnp.tanh(c * (x + 0.044715 * (x * x * x)))))


def _norm_kernel(x_ref, g_ref, o_ref):
    x = x_ref[...]
    ms = jnp.mean(x * x, axis=-1, keepdims=True)
    o_ref[...] = (x * lax.rsqrt(ms + NORM_EPS) * g_ref[...]).astype(BF16)


def _pre_norm(x2, gain):
    t = x2.shape[0]
    return pl.pallas_call(
        _norm_kernel,
        grid=(t // NORM_ROWS,),
        in_specs=[pl.BlockSpec((NORM_ROWS, D_MODEL), lambda i: (i, 0)),
                  pl.BlockSpec((1, D_MODEL), lambda i: (0, 0))],
        out_specs=pl.BlockSpec((NORM_ROWS, D_MODEL), lambda i: (i, 0)),
        out_shape=jax.ShapeDtypeStruct((t, D_MODEL), BF16),
        compiler_params=_params("parallel"),
        name="pre_norm",
    )(x2, gain.reshape(1, D_MODEL))


def _dot_epilogue(x_ref, w_ref, o_ref, epilogue):
    w = w_ref[...]
    for r in range(x_ref.shape[0] // PROJ_SUB_ROWS):
        rows = pl.ds(r * PROJ_SUB_ROWS, PROJ_SUB_ROWS)
        acc = jnp.dot(x_ref[rows, :], w, preferred_element_type=F32)
        o_ref[rows, :] = epilogue(acc, rows).astype(BF16)


def _proj_raw_kernel(x_ref, w_ref, o_ref):
    _dot_epilogue(x_ref, w_ref, o_ref, lambda acc, rows: acc)


def _proj_raw(xn, w, name):
    t = xn.shape[0]
    n = w.shape[1]
    tm = min(PROJ_TM, t)
    return pl.pallas_call(
        _proj_raw_kernel,
        grid=(t // tm, n // PROJ_RAW_TN),
        in_specs=[pl.BlockSpec((tm, D_MODEL), lambda i, j: (i, 0)),
                  pl.BlockSpec((D_MODEL, PROJ_RAW_TN), lambda i, j: (0, j))],
        out_specs=pl.BlockSpec((tm, PROJ_RAW_TN), lambda i, j: (i, j)),
        out_shape=jax.ShapeDtypeStruct((t, n), BF16),
        compiler_params=_params("parallel", "arbitrary"),
        name=name,
    )(xn, w)


def _head_norm_rope(acc, gain, cos_t, sin_lo, sin_hi):
    outs = []
    for h in range(PROJ_TN // HEAD_DIM):
        xh = acc[:, h * HEAD_DIM:(h + 1) * HEAD_DIM]
        ms = jnp.mean(xh * xh, axis=-1, keepdims=True)
        y = xh * lax.rsqrt(ms + NORM_EPS) * gain
        half = ROPE_DIMS // 2
        up = pltpu.roll(y, HEAD_DIM - half, axis=1)
        dn = pltpu.roll(y, half, axis=1)
        outs.append(y * cos_t + up * sin_lo + dn * sin_hi)
    return jnp.concatenate(outs, axis=1)


def _proj_rest_kernel(kind_ref, tile_ref, x_ref, w_ref, qg_ref, kg_ref, cos_ref, slo_ref, shi_ref, o_ref):
    del tile_ref
    kind = kind_ref[pl.program_id(1)]

    def norm_rope(gain_ref):
        return lambda acc, rows: _head_norm_rope(acc, gain_ref[...], cos_ref[rows, :], slo_ref[rows, :],
                                                 shi_ref[rows, :])

    epilogues = {
        KIND_SILU: lambda acc, rows: _silu(acc),
        KIND_GELU: lambda acc, rows: _gelu_tanh(acc),
        KIND_QNORM: norm_rope(qg_ref),
        KIND_KNORM: norm_rope(kg_ref),
        KIND_RAW: lambda acc, rows: acc,
    }
    for k, epilogue in epilogues.items():
        pl.when(kind == k)(functools.partial(_dot_epilogue, x_ref, w_ref, o_ref, epilogue))


def _proj_rest(xn, w_rest, q_gain, k_gain, seq):
    t = xn.shape[0]
    tm = min(PROJ_REST_TM, t)
    blocks_per_seq = seq // tm if seq >= tm else 1
    tables = _rope_tables(seq)
    if seq < tm:
        tables = tuple(jnp.tile(tb, (tm // seq, 1)) for tb in tables)
    vec = pl.BlockSpec((1, HEAD_DIM), lambda i, j, kind, tile: (0, 0))
    tab = pl.BlockSpec((tm, HEAD_DIM), lambda i, j, kind, tile: (i % blocks_per_seq, 0))
    return pl.pallas_call(
        _proj_rest_kernel,
        grid_spec=pltpu.PrefetchScalarGridSpec(
            num_scalar_prefetch=2,
            grid=(t // tm, len(_REST_KIND)),
            in_specs=[pl.BlockSpec((tm, D_MODEL), lambda i, j, kind, tile: (i, 0)),
                      pl.BlockSpec((D_MODEL, PROJ_TN), lambda i, j, kind, tile: (0, j)),
                      vec, vec, tab, tab, tab],
            out_specs=pl.BlockSpec((tm, PROJ_TN), lambda i, j, kind, tile: (i, tile[j]))),
        out_shape=jax.ShapeDtypeStruct((t, REST_ACT_WIDTH), BF16),
        compiler_params=_params("parallel", "arbitrary"),
        name="proj_rest",
    )(jnp.asarray(_REST_KIND, jnp.int32), jnp.asarray(_REST_OUT_TILE, jnp.int32),
      xn, w_rest, q_gain, k_gain, *tables)


def _rope_tables(seq):
    inv_freq = ROPE_THETA ** (-jnp.arange(0, ROPE_DIMS, 2, dtype=F32) / ROPE_DIMS)
    ang = jnp.arange(seq, dtype=F32)[:, None] * inv_freq[None, :]
    cos, sin = jnp.cos(ang), jnp.sin(ang)
    half = ROPE_DIMS // 2
    ones = jnp.ones((seq, HEAD_DIM - ROPE_DIMS), F32)
    zeros = jnp.zeros((seq, HEAD_DIM - ROPE_DIMS), F32)
    zh = jnp.zeros((seq, half), F32)
    cos_t = jnp.concatenate([cos, cos, ones], axis=1)
    sin_lo = jnp.concatenate([-sin, zh, zeros], axis=1)
    sin_hi = jnp.concatenate([zh, sin, zeros], axis=1)
    return cos_t, sin_lo, sin_hi


def _decay_kernel(x_ref, w_ref, up_ref, b_ref, o_ref, mx_ref):
    lr = jnp.dot(x_ref[...], w_ref[...], preferred_element_type=F32)
    z = jnp.dot(lr.astype(BF16), up_ref[...], preferred_element_type=F32) + b_ref[...]
    log_a = (jnp.minimum(z, 0.0) - jnp.log(1.0 + jnp.exp(-jnp.abs(z)))) * (1.0 / GLA_GATE_TAU)
    o_ref[...] = log_a
    n_blocks = log_a.shape[0] // GLA_ROWS
    full_max = lambda x: jnp.broadcast_to(jnp.max(jnp.max(x, axis=1, keepdims=True), axis=0, keepdims=True),
                                          (1, LANES))
    blocks = [log_a[r * GLA_ROWS:(r + 1) * GLA_ROWS, :] for r in range(n_blocks)]
    zero = [jnp.zeros((1, LANES), F32)] * (SUBLANES // 2 - n_blocks)
    rows = ([full_max(-blk) for blk in blocks] + zero
            + [full_max(-jnp.sum(blk, axis=0, keepdims=True)) for blk in blocks] + zero)
    mx_ref[0] = jnp.concatenate(rows, axis=0)


def _gla_log_decay(xn, w_lr, up2, bias2):
    t = xn.shape[0]
    tm = min(PROJ_TM, t)
    n = 2 * GLA_QK_WIDTH
    log_a, mx = pl.pallas_call(
        _decay_kernel,
        grid=(t // tm,),
        in_specs=[pl.BlockSpec((tm, D_MODEL), lambda i: (i, 0)),
                  pl.BlockSpec((D_MODEL, LANES), lambda i: (0, 0)),
                  pl.BlockSpec((LANES, n), lambda i: (0, 0)),
                  pl.BlockSpec((1, n), lambda i: (0, 0))],
        out_specs=[pl.BlockSpec((tm, n), lambda i: (i, 0)),
                   pl.BlockSpec((1, SUBLANES, LANES), lambda i: (i, 0, 0))],
        out_shape=[jax.ShapeDtypeStruct((t, n), F32),
                   jax.ShapeDtypeStruct((t // tm, SUBLANES, LANES), F32)],
        compiler_params=_params("parallel"),
        name="gla_log_decay",
    )(xn, w_lr, up2, bias2)
    n_blocks = tm // GLA_ROWS
    step_max = mx[:, :n_blocks, 0].reshape(t // GLA_ROWS)
    block_max = mx[:, SUBLANES // 2:SUBLANES // 2 + n_blocks, 0].reshape(t // GLA_ROWS)
    path = jnp.where(block_max <= GLA_FAST_MAX_DECAY, GLA_PATH_WIDE,
                     jnp.where(step_max <= GLA_FAST_MAX_DECAY / GLA_CHUNK, GLA_PATH_CHUNKED, GLA_PATH_EXACT))
    return log_a, path.astype(jnp.int32)


def _gla_head_rows(q, k, v, b, state, mask, chunk_rows, reverse, exact_scratch):
    n_chunks = q.shape[0] // chunk_rows
    chunk = lambda x, c: x[c * chunk_rows:(c + 1) * chunk_rows]
    edge = 0 if reverse else chunk_rows - 1
    b_tot = [chunk(b, c)[edge:edge + 1, :] for c in range(n_chunks)]
    dec_row = [jnp.exp(bt) for bt in b_tot]
    qb = (q * jnp.exp(b)).astype(BF16)

    if exact_scratch is None:
        k_inv = k * jnp.exp(-b)
        s = lax.dot_general(qb, k_inv.astype(BF16), (((1,), (1,)), ((), ())), preferred_element_type=F32)
        o = jnp.dot(jnp.where(mask, s, 0.0).astype(BF16), v, preferred_element_type=F32)
        kb = [chunk(k_inv, c) * dec_row[c] for c in range(n_chunks)]
    else:
        b_scr, k_scr = exact_scratch
        col = lax.broadcasted_iota(jnp.int32, (chunk_rows, chunk_rows), 1)
        o_chunks, kb = [], []
        for c in range(n_chunks):
            qc, bc = chunk(q, c), chunk(b, c)
            b_scr[...] = bc
            k_scr[...] = chunk(k, c)

            def column(j, acc, qc=qc, bc=bc):
                decay = jnp.exp(jnp.minimum(bc - b_scr[pl.ds(j, 1), :], 0.0))
                w = qc * decay * k_scr[pl.ds(j, 1), :]
                return jnp.where(col == j, jnp.sum(w, axis=-1, keepdims=True), acc)

            s = lax.fori_loop(0, chunk_rows, column, jnp.zeros((chunk_rows, chunk_rows), F32))
            lo = c * chunk_rows
            p = jnp.where(mask[lo:lo + chunk_rows, lo:lo + chunk_rows], s, 0.0)
            o_chunks.append(jnp.dot(p.astype(BF16), chunk(v, c), preferred_element_type=F32))
            kb.append(chunk(k, c) * jnp.exp(b_tot[c] - bc))
        o = jnp.concatenate(o_chunks, axis=0)

    o_state = [None] * n_chunks
    for c in (range(n_chunks - 1, -1, -1) if reverse else range(n_chunks)):
        o_state[c] = jnp.dot(chunk(qb, c), state.astype(BF16), preferred_element_type=F32)
        upd = lax.dot_general(kb[c].astype(BF16), chunk(v, c), (((0,), (0,)), ((), ())),
                              preferred_element_type=F32)
        dec = jnp.transpose(jnp.broadcast_to(dec_row[c], (GLA_DK, GLA_DK)))
        state = jnp.concatenate([dec] * (GLA_DV // GLA_DK), axis=1) * state + upd
    return o + jnp.concatenate(o_state, axis=0), state


def _gla_scan_rows(q_ref, k_ref, v_ref, g_ref, state_ref, chunk_rows, reverse, emit, exact_scratch):
    row = lax.broadcasted_iota(jnp.int32, (GLA_ROWS, GLA_ROWS), 0)
    col = lax.broadcasted_iota(jnp.int32, (GLA_ROWS, GLA_ROWS), 1)
    shift = chunk_rows.bit_length() - 1
    mask = jnp.logical_and((col >= row) if reverse else (col <= row),
                           lax.shift_right_logical(row, shift) == lax.shift_right_logical(col, shift))
    tri_b = mask.astype(BF16)
    rest = g_ref[...]
    b_all = None
    for _ in range(3):
        term = rest.astype(BF16)
        rest = rest - term.astype(F32)
        part = jnp.dot(tri_b, term, preferred_element_type=F32)
        b_all = part if b_all is None else b_all + part
    for h in range(GLA_HEADS_PER_STEP):
        kc = pl.ds(h * GLA_DK, GLA_DK)
        vc = pl.ds(h * GLA_DV, GLA_DV)
        q = q_ref[:, kc].astype(F32) * (GLA_DK ** -0.5)
        k = k_ref[:, kc].astype(F32)
        o, new_state = _gla_head_rows(q, k, v_ref[:, vc], b_all[:, h * GLA_DK:(h + 1) * GLA_DK], state_ref[h],
                                      mask, chunk_rows, reverse, exact_scratch)
        state_ref[h] = new_state
        emit(vc, o)


def _gla_scan_step(path_ref, step_block, refs, state_ref, b_scr, k_scr, reverse, emit):
    @pl.when(pl.program_id(2) == 0)
    def _():
        state_ref[...] = jnp.zeros_like(state_ref)

    for sb in (range(GLA_STEP_BLOCKS - 1, -1, -1) if reverse else range(GLA_STEP_BLOCKS)):
        rows = pl.ds(sb * GLA_ROWS, GLA_ROWS)
        sub = tuple(r.at[rows, :] for r in refs)
        emit_rows = functools.partial(emit, rows)
        path = path_ref[step_block * GLA_STEP_BLOCKS + sb]
        pl.when(path == GLA_PATH_WIDE)(
            functools.partial(_gla_scan_rows, *sub, state_ref, GLA_ROWS, reverse, emit_rows, None))
        pl.when(path == GLA_PATH_CHUNKED)(
            functools.partial(_gla_scan_rows, *sub, state_ref, GLA_CHUNK, reverse, emit_rows, None))
        pl.when(path == GLA_PATH_EXACT)(
            functools.partial(_gla_scan_rows, *sub, state_ref, GLA_CHUNK, reverse, emit_rows, (b_scr, k_scr)))


def _gla_fwd_kernel(exact_ref, q_ref, k_ref, v_ref, g_ref, o_ref, state_ref, b_scr, k_scr, *, nb):
    def emit(rows, cols, o):
        o_ref[rows, cols] = o

    step_block = pl.program_id(0) * nb + pl.program_id(2)
    _gla_scan_step(exact_ref, step_block, (q_ref, k_ref, v_ref, g_ref), state_ref, b_scr, k_scr, False, emit)


def _gla_bwd_kernel(exact_ref, q_ref, k_ref, v_ref, g_ref, fwd_ref, gate_ref, gain_ref, o_ref,
                    state_ref, b_scr, k_scr, *, nb):
    def emit(rows, cols, o):
        tot = fwd_ref[rows, cols] + o
        ms = jnp.mean(tot * tot, axis=-1, keepdims=True)
        y = tot * lax.rsqrt(ms + NORM_EPS) * gain_ref[...]
        o_ref[rows, cols] = (y * gate_ref[rows, cols].astype(F32)).astype(BF16)

    step_block = pl.program_id(0) * nb + (nb - 1 - pl.program_id(2))
    _gla_scan_step(exact_ref, step_block, (q_ref, k_ref, v_ref, g_ref), state_ref, b_scr, k_scr, True, emit)


def _gla(h_qkv, h_rest, log_a, exact, gain, batch, seq):
    t = batch * seq
    step_rows = GLA_STEP_BLOCKS * GLA_ROWS
    nb = seq // step_rows
    hp = GLA_HEADS_PER_STEP
    qk_w, v_w = hp * GLA_DK, hp * GLA_DV
    k_off = GLA_QK_WIDTH // qk_w
    v_off = 2 * GLA_QK_WIDTH // v_w
    gate_off = H_GLA_GATE * GLA_WIDTH // v_w
    dir_off = GLA_QK_WIDTH // qk_w
    grid = (batch, GLA_HEADS // hp, nb)
    scratch = [pltpu.VMEM((hp, GLA_DK, GLA_DV), F32),
               pltpu.VMEM((GLA_CHUNK, GLA_DK), F32), pltpu.VMEM((GLA_CHUNK, GLA_DK), F32)]

    def specs(rowmap, direction):
        return [pl.BlockSpec((step_rows, qk_w), lambda b, h, n, e: (rowmap(b, n), h)),
                pl.BlockSpec((step_rows, qk_w), lambda b, h, n, e: (rowmap(b, n), k_off + h)),
                pl.BlockSpec((step_rows, v_w), lambda b, h, n, e: (rowmap(b, n), v_off + h)),
                pl.BlockSpec((step_rows, qk_w), lambda b, h, n, e: (rowmap(b, n), direction * dir_off + h))]

    fmap = lambda b, n: b * nb + n
    rmap = lambda b, n: b * nb + (nb - 1 - n)
    fwd = pl.pallas_call(
        functools.partial(_gla_fwd_kernel, nb=nb),
        grid_spec=pltpu.PrefetchScalarGridSpec(
            num_scalar_prefetch=1,
            grid=grid,
            in_specs=specs(fmap, 0),
            out_specs=pl.BlockSpec((step_rows, v_w), lambda b, h, n, e: (fmap(b, n), h)),
            scratch_shapes=scratch),
        out_shape=jax.ShapeDtypeStruct((t, GLA_WIDTH), F32),
        compiler_params=_params("parallel", "parallel", "arbitrary"),
        name="gla_fwd",
    )(exact, h_qkv, h_qkv, h_qkv, log_a)
    return pl.pallas_call(
        functools.partial(_gla_bwd_kernel, nb=nb),
        grid_spec=pltpu.PrefetchScalarGridSpec(
            num_scalar_prefetch=1,
            grid=grid,
            in_specs=specs(rmap, 1) + [
                pl.BlockSpec((step_rows, v_w), lambda b, h, n, e: (rmap(b, n), h)),
                pl.BlockSpec((step_rows, v_w), lambda b, h, n, e: (rmap(b, n), gate_off + h)),
                pl.BlockSpec((1, GLA_DV), lambda b, h, n, e: (0, 0))],
            out_specs=pl.BlockSpec((step_rows, v_w), lambda b, h, n, e: (rmap(b, n), h)),
            scratch_shapes=scratch),
        out_shape=jax.ShapeDtypeStruct((t, GLA_WIDTH), BF16),
        compiler_params=_params("parallel", "parallel", "arbitrary"),
        name="gla_bwd",
    )(exact, h_qkv, h_qkv, h_qkv, log_a, fwd, h_rest, gain.reshape(1, GLA_DV))


def _sgu_kernel(u_ref, v_ref, gate_ref, lng_ref, lnb_ref, w_ref, bt_ref, o_ref):
    for c in range(SGU_ROWS // SGU_CHUNK):
        rows = pl.ds(c * SGU_CHUNK, SGU_CHUNK)
        x = v_ref[rows, :].astype(F32)
        mu = jnp.mean(x, axis=-1, keepdims=True)
        xc = x - mu
        var = jnp.mean(xc * xc, axis=-1, keepdims=True)
        y = (xc * lax.rsqrt(var + LN_EPS) * lng_ref[...] + lnb_ref[...]).astype(BF16)
        for g in range(SGU_GROUPS):
            cols = pl.ds(g * LANES, LANES)
            mixed = jnp.dot(w_ref[g], y[:, g * LANES:(g + 1) * LANES], preferred_element_type=F32)
            mixed = mixed + bt_ref[:, g:g + 1]
            o_ref[rows, cols] = (u_ref[rows, cols].astype(F32) * mixed
                                 * gate_ref[rows, cols].astype(F32)).astype(BF16)


def _sgu(h_rest, ln_gain, ln_bias, w, b):
    t = h_rest.shape[0]
    blk = lambda c: pl.BlockSpec((SGU_ROWS, SGU_WIDTH), lambda i: (i, c))
    return pl.pallas_call(
        _sgu_kernel,
        grid=(t // SGU_ROWS,),
        in_specs=[blk(H_SGU_U), blk(H_SGU_V), blk(H_SGU_GATE),
                  pl.BlockSpec((1, SGU_WIDTH), lambda i: (0, 0)),
                  pl.BlockSpec((1, SGU_WIDTH), lambda i: (0, 0)),
                  pl.BlockSpec((SGU_GROUPS, SGU_CHUNK, SGU_CHUNK), lambda i: (0, 0, 0)),
                  pl.BlockSpec((SGU_CHUNK, SGU_GROUPS), lambda i: (0, 0))],
        out_specs=pl.BlockSpec((SGU_ROWS, SGU_WIDTH), lambda i: (i, 0)),
        out_shape=jax.ShapeDtypeStruct((t, SGU_WIDTH), BF16),
        compiler_params=_params("parallel"),
        name="sgu",
    )(h_rest, h_rest, h_rest, ln_gain.reshape(1, SGU_WIDTH), ln_bias.reshape(1, SGU_WIDTH),
      w.astype(BF16), jnp.transpose(b))


def _attn_kernel(sink_ref, q_ref, gate_ref, kp_ref, kc_ref, kn_ref, vp_ref, vc_ref, vn_ref, o_ref, *, n_steps):
    n = pl.program_id(1)
    log2e = math.log2(math.e)
    qpos = lax.broadcasted_iota(jnp.int32, (WINDOW, 3 * WINDOW), 0)
    kpos = lax.broadcasted_iota(jnp.int32, (WINDOW, 3 * WINDOW), 1) - WINDOW
    band = jnp.abs(kpos - qpos) <= WINDOW
    full = slice(None)
    block = lambda c: pl.ds(c * WINDOW, WINDOW)

    items = []
    for a in range(ATTN_BLOCKS):
        prev = (kp_ref, vp_ref, full) if a == 0 else (kc_ref, vc_ref, block(a - 1))
        nxt = (kn_ref, vn_ref, full) if a == ATTN_BLOCKS - 1 else (kc_ref, vc_ref, block(a + 1))
        keys = [(prev[0], prev[2]), (kc_ref, block(a)), (nxt[0], nxt[2])]
        vals = [(prev[1], prev[2]), (vc_ref, block(a)), (nxt[1], nxt[2])]
        valid = band
        if a == 0:
            valid = jnp.logical_and(valid, jnp.logical_or(kpos >= 0, n > 0))
        if a == ATTN_BLOCKS - 1:
            valid = jnp.logical_and(valid, jnp.logical_or(kpos < WINDOW, n < n_steps - 1))
        valid = jnp.concatenate([valid] * ATT_GROUP, axis=0)
        items += [(block(a), keys, vals, valid, hk) for hk in range(ATT_KV_HEADS)]

    def scores(item):
        rows, keys, _, _, hk = item
        kcols = pl.ds(hk * HEAD_DIM, HEAD_DIM)
        q = jnp.concatenate([q_ref[rows, pl.ds((hk * ATT_GROUP + g) * HEAD_DIM, HEAD_DIM)]
                             for g in range(ATT_GROUP)], axis=0)
        k = jnp.concatenate([ref[r, kcols] for ref, r in keys], axis=0)
        return lax.dot_general(q, k, (((1,), (1,)), ((), ())), preferred_element_type=F32)

    def finish(item, s):
        rows, _, vals, valid, hk = item
        kcols = pl.ds(hk * HEAD_DIM, HEAD_DIM)
        v = jnp.concatenate([ref[r, kcols] for ref, r in vals], axis=0)
        s = jnp.where(valid, s * (HEAD_DIM ** -0.5 * log2e), -jnp.inf)
        sink = jnp.concatenate([jnp.full((WINDOW, 1), sink_ref[hk * ATT_GROUP + g] * log2e, F32)
                                for g in range(ATT_GROUP)], axis=0)
        m = jnp.maximum(jnp.max(s, axis=-1, keepdims=True), sink)
        e = jnp.exp2(s - m)
        denom = jnp.sum(e, axis=-1, keepdims=True) + jnp.exp2(sink - m)
        o = jnp.dot(e.astype(BF16), v, preferred_element_type=F32) / denom
        for g in range(ATT_GROUP):
            cols = pl.ds((hk * ATT_GROUP + g) * HEAD_DIM, HEAD_DIM)
            o_ref[rows, cols] = (o[g * WINDOW:(g + 1) * WINDOW, :] * gate_ref[rows, cols].astype(F32)).astype(BF16)

    pending = [scores(item) for item in items[:ATTN_LOOKAHEAD]]
    for i, item in enumerate(items):
        if i + ATTN_LOOKAHEAD < len(items):
            pending.append(scores(items[i + ATTN_LOOKAHEAD]))
        finish(item, pending.pop(0))


def _attention(h_rest, sink, batch, seq):
    t = batch * seq
    rows = ATTN_BLOCKS * WINDOW
    n_steps = seq // rows
    nb = seq // WINDOW
    cur = lambda b, n: b * n_steps + n
    prv = lambda b, n: b * nb + jnp.maximum(n * ATTN_BLOCKS - 1, 0)
    nxt = lambda b, n: b * nb + jnp.minimum((n + 1) * ATTN_BLOCKS, nb - 1)
    wide = lambda blk: pl.BlockSpec((rows, ATT_WIDTH), lambda b, n: (cur(b, n), blk))
    kv_cur = lambda blk: pl.BlockSpec((rows, ATT_KV_WIDTH), lambda b, n: (cur(b, n), blk))
    kv_edge = lambda rowmap, blk: pl.BlockSpec((WINDOW, ATT_KV_WIDTH), lambda b, n: (rowmap(b, n), blk))
    return pl.pallas_call(
        functools.partial(_attn_kernel, n_steps=n_steps),
        grid=(batch, n_steps),
        in_specs=[pl.BlockSpec(memory_space=pltpu.SMEM),
                  wide(H_ATT_Q), wide(H_ATT_GATE),
                  kv_edge(prv, H_ATT_K), kv_cur(H_ATT_K), kv_edge(nxt, H_ATT_K),
                  kv_edge(prv, H_ATT_V), kv_cur(H_ATT_V), kv_edge(nxt, H_ATT_V)],
        out_specs=pl.BlockSpec((rows, ATT_WIDTH), lambda b, n: (cur(b, n), 0)),
        out_shape=jax.ShapeDtypeStruct((t, ATT_WIDTH), BF16),
        compiler_params=_params("parallel", "parallel"),
        name="window_attention",
    )(sink, h_rest, h_rest, h_rest, h_rest, h_rest, h_rest, h_rest, h_rest)


def _merge_kernel(x_ref, oa_ref, ob_ref, oc_ref, g0_ref, g1_ref, g2_ref, wbr_ref, bias_ref, o_ref):
    x = x_ref[...]
    acc = None
    row0 = 0
    for i, (g_ref, o_in) in enumerate(((g0_ref, oa_ref), (g1_ref, ob_ref), (g2_ref, oc_ref))):
        width = o_in.shape[1]
        gate = jax.nn.sigmoid(jnp.dot(x, g_ref[...], preferred_element_type=F32) + bias_ref[i:i + 1, :])
        term = gate * jnp.dot(o_in[...], wbr_ref[row0:row0 + width, :], preferred_element_type=F32)
        acc = term if acc is None else acc + term
        row0 += width
    o_ref[...] = acc.astype(BF16)


def _merge(xn, o_a, o_b, o_c, w_rest, w_br, gate_bias):
    t = xn.shape[0]
    tm = min(MERGE_TM, t)
    nj = D_MODEL // MERGE_TN
    g0 = REST_ACT_WIDTH // MERGE_TN
    row = lambda width: pl.BlockSpec((tm, width), lambda i, j: (i, 0))
    gcol = lambda br: pl.BlockSpec((D_MODEL, MERGE_TN), lambda i, j: (0, g0 + br * nj + j))
    return pl.pallas_call(
        _merge_kernel,
        grid=(t // tm, nj),
        in_specs=[row(D_MODEL), row(GLA_WIDTH), row(SGU_WIDTH), row(ATT_WIDTH),
                  gcol(0), gcol(1), gcol(2),
                  pl.BlockSpec((GLA_WIDTH + SGU_WIDTH + ATT_WIDTH, MERGE_TN), lambda i, j: (0, j)),
                  pl.BlockSpec((N_BRANCH, MERGE_TN), lambda i, j: (0, j))],
        out_specs=pl.BlockSpec((tm, MERGE_TN), lambda i, j: (i, j)),
        out_shape=jax.ShapeDtypeStruct((t, D_MODEL), BF16),
        compiler_params=_params("parallel", "arbitrary"),
        name="gated_merge",
    )(xn, o_a, o_b, o_c, w_rest, w_rest, w_rest, w_br, gate_bias)


def _out_kernel(m_ref, w_ref, x_ref, o_ref):
    o_ref[...] = x_ref[...] + jnp.dot(m_ref[...], w_ref[...], preferred_element_type=F32)


def _out_proj(merged, w_out, x2):
    t = x2.shape[0]
    tm = min(OUT_TM, t)
    return pl.pallas_call(
        _out_kernel,
        grid=(t // tm, D_MODEL // OUT_TN),
        in_specs=[pl.BlockSpec((tm, D_MODEL), lambda i, j: (i, 0)),
                  pl.BlockSpec((D_MODEL, OUT_TN), lambda i, j: (0, j)),
                  pl.BlockSpec((tm, OUT_TN), lambda i, j: (i, j))],
        out_specs=pl.BlockSpec((tm, OUT_TN), lambda i, j: (i, j)),
        out_shape=jax.ShapeDtypeStruct((t, D_MODEL), F32),
        compiler_params=_params("parallel", "arbitrary"),
        name="out_proj",
    )(merged, w_out, x2)


def _cast_kernel(x_ref, o_ref):
    o_ref[...] = x_ref[...].astype(BF16)


def _cast_bf16(w, layer, name):
    rows, cols = w.shape[1], w.shape[2]
    return pl.pallas_call(
        _cast_kernel,
        grid=(rows // CAST_ROWS, cols // CAST_TN),
        in_specs=[pl.BlockSpec((None, CAST_ROWS, CAST_TN), lambda r, j: (layer, r, j))],
        out_specs=pl.BlockSpec((CAST_ROWS, CAST_TN), lambda r, j: (r, j)),
        out_shape=jax.ShapeDtypeStruct((rows, cols), BF16),
        compiler_params=_params("parallel", "parallel"),
        name=name,
    )(w)


def _cast_t_kernel(x_ref, o_ref):
    o_ref[...] = jnp.transpose(x_ref[...]).astype(BF16)


def _cast_t_shift_kernel(a_ref, b_ref, o_ref):
    both = jnp.concatenate([a_ref[CAST_SHIFT:, :], b_ref[...]], axis=0)
    o_ref[...] = jnp.transpose(both).astype(BF16)


def _cast_t_head_kernel(x_ref, o_ref):
    xt = jnp.transpose(x_ref[...])
    lane = lax.broadcasted_iota(jnp.int32, xt.shape, 1)
    o_ref[...] = jnp.where(lane < CAST_SHIFT, xt, 0.0).astype(BF16)


def _cast_t_bf16(wt, layer, row0, n_rows, name):
    k = wt.shape[2]
    tile0 = row0 // CAST_TN
    return pl.pallas_call(
        _cast_t_kernel,
        grid=(k // CAST_ROWS, n_rows // CAST_TN),
        in_specs=[pl.BlockSpec((None, CAST_TN, CAST_ROWS), lambda r, j: (layer, tile0 + j, r))],
        out_specs=pl.BlockSpec((CAST_ROWS, CAST_TN), lambda r, j: (r, j)),
        out_shape=jax.ShapeDtypeStruct((k, n_rows), BF16),
        compiler_params=_params("parallel", "parallel"),
        name=name,
    )(wt)


def _cast_t_bf16_shifted(wt, layer, row0, name):
    n, k = wt.shape[1], wt.shape[2]
    n_rows = n - row0 - CAST_SHIFT
    tile0 = row0 // CAST_TN
    shifts_per_tile = CAST_TN // CAST_SHIFT
    return pl.pallas_call(
        _cast_t_shift_kernel,
        grid=(k // CAST_ROWS, n_rows // CAST_TN),
        in_specs=[pl.BlockSpec((None, CAST_TN, CAST_ROWS), lambda r, j: (layer, tile0 + j, r)),
                  pl.BlockSpec((None, CAST_SHIFT, CAST_ROWS),
                               lambda r, j: (layer, (tile0 + j + 1) * shifts_per_tile, r))],
        out_specs=pl.BlockSpec((CAST_ROWS, CAST_TN), lambda r, j: (r, j)),
        out_shape=jax.ShapeDtypeStruct((k, n_rows), BF16),
        compiler_params=_params("parallel", "parallel"),
        name=name,
    )(wt, wt)


def _cast_t_bf16_head(wt, layer, row0, name):
    k = wt.shape[2]
    return pl.pallas_call(
        _cast_t_head_kernel,
        grid=(k // CAST_ROWS,),
        in_specs=[pl.BlockSpec((None, LANES, CAST_ROWS), lambda r: (layer, row0 // LANES, r))],
        out_specs=pl.BlockSpec((CAST_ROWS, LANES), lambda r: (r, 0)),
        out_shape=jax.ShapeDtypeStruct((k, LANES), BF16),
        compiler_params=_params("parallel"),
        name=name,
    )(wt)


def _prepare_layer(layer, norm_gain, w_in, gla_gate_up, gla_gate_bias, gla_norm_gain, sgu_ln_gain, sgu_ln_bias,
                   sgu_w, sgu_b, q_norm_gain, k_norm_gain, sink, gate_bias, w_br, w_out):
    lr0 = 2 * GLA_QK_WIDTH + GLA_WIDTH
    lr1 = lr0 + 2 * GLA_GATE_RANK
    assert lr1 - lr0 == CAST_SHIFT and lr0 % CAST_TN == 0
    w_in_t = jnp.swapaxes(w_in, 1, 2)
    up2 = jnp.zeros((LANES, 2 * GLA_QK_WIDTH), F32)
    up2 = up2.at[:GLA_GATE_RANK, :GLA_QK_WIDTH].set(gla_gate_up[layer, 0])
    up2 = up2.at[GLA_GATE_RANK:2 * GLA_GATE_RANK, GLA_QK_WIDTH:].set(gla_gate_up[layer, 1])
    return dict(
        norm_gain=norm_gain[layer],
        w_qkv=_cast_t_bf16(w_in_t, layer, 0, lr0, "cast_w_qkv"),
        w_rest=_cast_t_bf16_shifted(w_in_t, layer, lr0, "cast_w_rest"),
        w_lr=_cast_t_bf16_head(w_in_t, layer, lr0, "cast_w_lr"), up2=up2.astype(BF16),
        bias2=gla_gate_bias[layer].reshape(1, 2 * GLA_QK_WIDTH), gla_norm_gain=gla_norm_gain[layer],
        sgu_ln_gain=sgu_ln_gain[layer], sgu_ln_bias=sgu_ln_bias[layer], sgu_w=sgu_w[layer], sgu_b=sgu_b[layer],
        q_gain=q_norm_gain[layer].reshape(1, HEAD_DIM), k_gain=k_norm_gain[layer].reshape(1, HEAD_DIM),
        sink=sink[layer], gate_bias=gate_bias[layer],
        w_br=_cast_bf16(w_br, layer, "cast_w_br"),
        w_out=_cast_bf16(w_out, layer, "cast_w_out"))


def _layer(x, p):
    batch, seq, _ = x.shape
    t = batch * seq
    x2 = x.reshape(t, D_MODEL)
    xn = _pre_norm(x2, p["norm_gain"])

    h_qkv = _proj_raw(xn, p["w_qkv"], "proj_gla_qkv")
    h_rest = _proj_rest(xn, p["w_rest"], p["q_gain"], p["k_gain"], seq)
    log_a, gla_exact = _gla_log_decay(xn, p["w_lr"], p["up2"], p["bias2"])

    o_a = _gla(h_qkv, h_rest, log_a, gla_exact, p["gla_norm_gain"], batch, seq)
    o_b = _sgu(h_rest, p["sgu_ln_gain"], p["sgu_ln_bias"], p["sgu_w"], p["sgu_b"])
    o_c = _attention(h_rest, p["sink"], batch, seq)

    merged = _merge(xn, o_a, o_b, o_c, p["w_rest"], p["w_br"], p["gate_bias"])
    return _out_proj(merged, p["w_out"], x2).reshape(batch, seq, D_MODEL)


def kernel(x_prompt, x_sample, norm_gain, w_in, gla_gate_up, gla_gate_bias, gla_norm_gain, sgu_ln_gain,
           sgu_ln_bias, sgu_w, sgu_b, q_norm_gain, k_norm_gain, sink, gate_bias, w_br, w_out):
    y_prompt, y_sample = x_prompt, x_sample
    for l in range(norm_gain.shape[0]):
        p = _prepare_layer(l, norm_gain, w_in, gla_gate_up, gla_gate_bias, gla_norm_gain, sgu_ln_gain,
                           sgu_ln_bias, sgu_w, sgu_b, q_norm_gain, k_norm_gain, sink, gate_bias, w_br, w_out)
        y_prompt = _layer(y_prompt, p)
        y_sample = _layer(y_sample, p)
    return (y_prompt, y_sample)
```

```python
import functools
import math

import jax
import jax.numpy as jnp
from jax import lax
from jax.experimental import pallas as pl
from jax.experimental.pallas import tpu as pltpu

F32 = jnp.float32
BF16 = jnp.bfloat16

D_MODEL = 4096
HEAD_DIM = 128
GLA_WIDTH = 1536
GLA_DV = 256
GLA_HEADS = 6
GLA_DK = 128
GLA_QK_WIDTH = 768
GLA_GATE_RANK = 16
GLA_GATE_TAU = 16.0
SGU_WIDTH = 1024
SGU_CHUNK = 128
SGU_GROUPS = 8
ATT_WIDTH = 1536
ATT_Q_HEADS = 12
ATT_KV_HEADS = 4
ATT_GROUP = 3
ATT_KV_WIDTH = 512
WINDOW = 128
ROPE_THETA = 500000.0
ROPE_DIMS = 32
N_BRANCH = 3
NORM_EPS = 1e-6
LN_EPS = 1e-5

VMEM_LIMIT_BYTES = 56 * 1024 * 1024
LANES = 128
SUBLANES = 8

NORM_ROWS = 512
PROJ_TM = 1024
PROJ_REST_TM = 2048
PROJ_TN = 512
PROJ_RAW_TN = 1024
PROJ_SUB_ROWS = 256
GLA_ROWS = 256
GLA_STEP_BLOCKS = 4
GLA_HEADS_PER_STEP = 6
GLA_CHUNK = 64
GLA_FAST_MAX_DECAY = 40.0
GLA_PATH_WIDE, GLA_PATH_CHUNKED, GLA_PATH_EXACT = 0, 1, 2
SGU_ROWS = 1024
ATTN_BLOCKS = 8
ATTN_LOOKAHEAD = 2
MERGE_TM = 512
MERGE_TN = 512
OUT_TM = 1024
OUT_TN = 1024
CAST_ROWS = 2048
CAST_TN = 512
CAST_SHIFT = 2 * GLA_GATE_RANK

KIND_SILU, KIND_GELU, KIND_QNORM, KIND_KNORM, KIND_RAW = range(5)
_REST_KIND = ([KIND_SILU] * 3 + [KIND_GELU] * 4 + [KIND_SILU] * 2 + [KIND_QNORM] * 3 + [KIND_KNORM, KIND_RAW]
              + [KIND_SILU] * 3)
_REST_OUT_TILE = [6, 7, 8, 0, 1, 2, 3, 4, 5, 9, 10, 11, 15, 16, 12, 13, 14]
REST_ACT_WIDTH = len(_REST_KIND) * PROJ_TN
H_SGU_U, H_SGU_V, H_SGU_GATE = 0, 1, 2
H_GLA_GATE, H_ATT_Q, H_ATT_GATE = 2, 3, 4
H_ATT_K, H_ATT_V = 15, 16


def _params(*sem):
    return pltpu.CompilerParams(dimension_semantics=sem, vmem_limit_bytes=VMEM_LIMIT_BYTES)


def _silu(x):
    return x * jax.nn.sigmoid(x)


def _gelu_tanh(x):
    c = math.sqrt(2.0 / math.pi)
    return x * (0.5 * (1.0 + jnp.tanh(c * (x + 0.044715 * (x * x * x)))))


def _norm_kernel(x_ref, g_ref, o_ref):
    x = x_ref[...]
    ms = jnp.mean(x * x, axis=-1, keepdims=True)
    o_ref[...] = (x * lax.rsqrt(ms + NORM_EPS) * g_ref[...]).astype(BF16)


def _pre_norm(x2, gain):
    t = x2.shape[0]
    return pl.pallas_call(
        _norm_kernel,
        grid=(t // NORM_ROWS,),
        in_specs=[pl.BlockSpec((NORM_ROWS, D_MODEL), lambda i: (i, 0)),
                  pl.BlockSpec((1, D_MODEL), lambda i: (0, 0))],
        out_specs=pl.BlockSpec((NORM_ROWS, D_MODEL), lambda i: (i, 0)),
        out_shape=jax.ShapeDtypeStruct((t, D_MODEL), BF16),
        compiler_params=_params("parallel"),
        name="pre_norm",
    )(x2, gain.reshape(1, D_MODEL))


def _dot_epilogue(x_ref, w_ref, o_ref, epilogue):
    w = w_ref[...]
    for r in range(x_ref.shape[0] // PROJ_SUB_ROWS):
        rows = pl.ds(r * PROJ_SUB_ROWS, PROJ_SUB_ROWS)
        acc = jnp.dot(x_ref[rows, :], w, preferred_element_type=F32)
        o_ref[rows, :] = epilogue(acc, rows).astype(BF16)


def _proj_raw_kernel(x_ref, w_ref, o_ref):
    _dot_epilogue(x_ref, w_ref, o_ref, lambda acc, rows: acc)


def _proj_raw(xn, w, name):
    t = xn.shape[0]
    n = w.shape[1]
    tm = min(PROJ_TM, t)
    return pl.pallas_call(
        _proj_raw_kernel,
        grid=(t // tm, n // PROJ_RAW_TN),
        in_specs=[pl.BlockSpec((tm, D_MODEL), lambda i, j: (i, 0)),
                  pl.BlockSpec((D_MODEL, PROJ_RAW_TN), lambda i, j: (0, j))],
        out_specs=pl.BlockSpec((tm, PROJ_RAW_TN), lambda i, j: (i, j)),
        out_shape=jax.ShapeDtypeStruct((t, n), BF16),
        compiler_params=_params("parallel", "arbitrary"),
        name=name,
    )(xn, w)


def _head_norm_rope(acc, gain, cos_t, sin_lo, sin_hi):
    outs = []
    for h in range(PROJ_TN // HEAD_DIM):
        xh = acc[:, h * HEAD_DIM:(h + 1) * HEAD_DIM]
        ms = jnp.mean(xh * xh, axis=-1, keepdims=True)
        y = xh * lax.rsqrt(ms + NORM_EPS) * gain
        half = ROPE_DIMS // 2
        up = pltpu.roll(y, HEAD_DIM - half, axis=1)
        dn = pltpu.roll(y, half, axis=1)
        outs.append(y * cos_t + up * sin_lo + dn * sin_hi)
    return jnp.concatenate(outs, axis=1)


def _proj_rest_kernel(kind_ref, tile_ref, x_ref, w_ref, qg_ref, kg_ref, cos_ref, slo_ref, shi_ref, o_ref):
    del tile_ref
    kind = kind_ref[pl.program_id(1)]

    def norm_rope(gain_ref):
        return lambda acc, rows: _head_norm_rope(acc, gain_ref[...], cos_ref[rows, :], slo_ref[rows, :],
                                                 shi_ref[rows, :])

    epilogues = {
        KIND_SILU: lambda acc, rows: _silu(acc),
        KIND_GELU: lambda acc, rows: _gelu_tanh(acc),
        KIND_QNORM: norm_rope(qg_ref),
        KIND_KNORM: norm_rope(kg_ref),
        KIND_RAW: lambda acc, rows: acc,
    }
    for k, epilogue in epilogues.items():
        pl.when(kind == k)(functools.partial(_dot_epilogue, x_ref, w_ref, o_ref, epilogue))


def _proj_rest(xn, w_rest, q_gain, k_gain, seq):
    t = xn.shape[0]
    tm = min(PROJ_REST_TM, t)
    blocks_per_seq = seq // tm if seq >= tm else 1
    tables = _rope_tables(seq)
    if seq < tm:
        tables = tuple(jnp.tile(tb, (tm // seq, 1)) for tb in tables)
    vec = pl.BlockSpec((1, HEAD_DIM), lambda i, j, kind, tile: (0, 0))
    tab = pl.BlockSpec((tm, HEAD_DIM), lambda i, j, kind, tile: (i % blocks_per_seq, 0))
    return pl.pallas_call(
        _proj_rest_kernel,
        grid_spec=pltpu.PrefetchScalarGridSpec(
            num_scalar_prefetch=2,
            grid=(t // tm, len(_REST_KIND)),
            in_specs=[pl.BlockSpec((tm, D_MODEL), lambda i, j, kind, tile: (i, 0)),
                      pl.BlockSpec((D_MODEL, PROJ_TN), lambda i, j, kind, tile: (0, j)),
                      vec, vec, tab, tab, tab],
            out_specs=pl.BlockSpec((tm, PROJ_TN), lambda i, j, kind, tile: (i, tile[j]))),
        out_shape=jax.ShapeDtypeStruct((t, REST_ACT_WIDTH), BF16),
        compiler_params=_params("parallel", "arbitrary"),
        name="proj_rest",
    )(jnp.asarray(_REST_KIND, jnp.int32), jnp.asarray(_REST_OUT_TILE, jnp.int32),
      xn, w_rest, q_gain, k_gain, *tables)


def _rope_tables(seq):
    inv_freq = ROPE_THETA ** (-jnp.arange(0, ROPE_DIMS, 2, dtype=F32) / ROPE_DIMS)
    ang = jnp.arange(seq, dtype=F32)[:, None] * inv_freq[None, :]
    cos, sin = jnp.cos(ang), jnp.sin(ang)
    half = ROPE_DIMS // 2
    ones = jnp.ones((seq, HEAD_DIM - ROPE_DIMS), F32)
    zeros = jnp.zeros((seq, HEAD_DIM - ROPE_DIMS), F32)
    zh = jnp.zeros((seq, half), F32)
    cos_t = jnp.concatenate([cos, cos, ones], axis=1)
    sin_lo = jnp.concatenate([-sin, zh, zeros], axis=1)
    sin_hi = jnp.concatenate([zh, sin, zeros], axis=1)
    return cos_t, sin_lo, sin_hi


def _decay_kernel(x_ref, w_ref, up_ref, b_ref, o_ref, mx_ref):
    full_max = lambda v: jnp.broadcast_to(jnp.max(jnp.max(v, axis=1, keepdims=True), axis=0, keepdims=True),
                                          (1, LANES))
    n_blocks = x_ref.shape[0] // GLA_ROWS
    step_max, block_max = [], []
    for r in range(n_blocks):
        rows = pl.ds(r * GLA_ROWS, GLA_ROWS)
        lr = jnp.dot(x_ref[rows, :], w_ref[...], preferred_element_type=F32)
        z = jnp.dot(lr.astype(BF16), up_ref[...], preferred_element_type=F32) + b_ref[...]
        log_a = (jnp.minimum(z, 0.0) - jnp.log(1.0 + jnp.exp(-jnp.abs(z)))) * (1.0 / GLA_GATE_TAU)
        o_ref[rows, :] = log_a
        step_max.append(full_max(-log_a))
        block_max.append(full_max(-jnp.sum(log_a, axis=0, keepdims=True)))
    zero = [jnp.zeros((1, LANES), F32)] * (SUBLANES // 2 - n_blocks)
    mx_ref[0] = jnp.concatenate(step_max + zero + block_max + zero, axis=0)


def _gla_log_decay(xn, w_lr, up2, bias2):
    t = xn.shape[0]
    tm = min(PROJ_TM, t)
    n = 2 * GLA_QK_WIDTH
    log_a, mx = pl.pallas_call(
        _decay_kernel,
        grid=(t // tm,),
        in_specs=[pl.BlockSpec((tm, D_MODEL), lambda i: (i, 0)),
                  pl.BlockSpec((D_MODEL, LANES), lambda i: (0, 0)),
                  pl.BlockSpec((LANES, n), lambda i: (0, 0)),
                  pl.BlockSpec((1, n), lambda i: (0, 0))],
        out_specs=[pl.BlockSpec((tm, n), lambda i: (i, 0)),
                   pl.BlockSpec((1, SUBLANES, LANES), lambda i: (i, 0, 0))],
        out_shape=[jax.ShapeDtypeStruct((t, n), F32),
                   jax.ShapeDtypeStruct((t // tm, SUBLANES, LANES), F32)],
        compiler_params=_params("parallel"),
        name="gla_log_decay",
    )(xn, w_lr, up2, bias2)
    n_blocks = tm // GLA_ROWS
    step_max = mx[:, :n_blocks, 0].reshape(t // GLA_ROWS)
    block_max = mx[:, SUBLANES // 2:SUBLANES // 2 + n_blocks, 0].reshape(t // GLA_ROWS)
    path = jnp.where(block_max <= GLA_FAST_MAX_DECAY, GLA_PATH_WIDE,
                     jnp.where(step_max <= GLA_FAST_MAX_DECAY / GLA_CHUNK, GLA_PATH_CHUNKED, GLA_PATH_EXACT))
    return log_a, path.astype(jnp.int32)


def _gla_head_rows(q, k, v, b, state, mask, chunk_rows, reverse, exact_scratch):
    n_chunks = q.shape[0] // chunk_rows
    chunk = lambda x, c: x[c * chunk_rows:(c + 1) * chunk_rows]
    edge = 0 if reverse else chunk_rows - 1
    b_tot = [chunk(b, c)[edge:edge + 1, :] for c in range(n_chunks)]
    dec_row = [jnp.exp(bt) for bt in b_tot]
    qb = (q * jnp.exp(b)).astype(BF16)

    if exact_scratch is None:
        k_inv = k * jnp.exp(-b)
        s = lax.dot_general(qb, k_inv.astype(BF16), (((1,), (1,)), ((), ())), preferred_element_type=F32)
        o = jnp.dot(jnp.where(mask, s, 0.0).astype(BF16), v, preferred_element_type=F32)
        kb = [chunk(k_inv, c) * dec_row[c] for c in range(n_chunks)]
    else:
        b_scr, k_scr = exact_scratch
        col = lax.broadcasted_iota(jnp.int32, (chunk_rows, chunk_rows), 1)
        o_chunks, kb = [], []
        for c in range(n_chunks):
            qc, bc = chunk(q, c), chunk(b, c)
            b_scr[...] = bc
            k_scr[...] = chunk(k, c)

            def column(j, acc, qc=qc, bc=bc):
                decay = jnp.exp(jnp.minimum(bc - b_scr[pl.ds(j, 1), :], 0.0))
                w = qc * decay * k_scr[pl.ds(j, 1), :]
                return jnp.where(col == j, jnp.sum(w, axis=-1, keepdims=True), acc)

            s = lax.fori_loop(0, chunk_rows, column, jnp.zeros((chunk_rows, chunk_rows), F32))
            lo = c * chunk_rows
            p = jnp.where(mask[lo:lo + chunk_rows, lo:lo + chunk_rows], s, 0.0)
            o_chunks.append(jnp.dot(p.astype(BF16), chunk(v, c), preferred_element_type=F32))
            kb.append(chunk(k, c) * jnp.exp(b_tot[c] - bc))
        o = jnp.concatenate(o_chunks, axis=0)

    o_state = [None] * n_chunks
    for c in (range(n_chunks - 1, -1, -1) if reverse else range(n_chunks)):
        o_state[c] = jnp.dot(chunk(qb, c), state.astype(BF16), preferred_element_type=F32)
        upd = lax.dot_general(kb[c].astype(BF16), chunk(v, c), (((0,), (0,)), ((), ())),
                              preferred_element_type=F32)
        dec = jnp.transpose(jnp.broadcast_to(dec_row[c], (GLA_DK, GLA_DK)))
        state = jnp.concatenate([dec] * (GLA_DV // GLA_DK), axis=1) * state + upd
    return o + jnp.concatenate(o_state, axis=0), state


def _gla_scan_rows(q_ref, k_ref, v_ref, g_ref, state_ref, chunk_rows, reverse, emit, exact_scratch):
    row = lax.broadcasted_iota(jnp.int32, (GLA_ROWS, GLA_ROWS), 0)
    col = lax.broadcasted_iota(jnp.int32, (GLA_ROWS, GLA_ROWS), 1)
    shift = chunk_rows.bit_length() - 1
    mask = jnp.logical_and((col >= row) if reverse else (col <= row),
                           lax.shift_right_logical(row, shift) == lax.shift_right_logical(col, shift))
    tri_b = mask.astype(BF16)
    rest = g_ref[...]
    b_all = None
    for _ in range(3):
        term = rest.astype(BF16)
        rest = rest - term.astype(F32)
        part = jnp.dot(tri_b, term, preferred_element_type=F32)
        b_all = part if b_all is None else b_all + part
    for h in range(GLA_HEADS_PER_STEP):
        kc = pl.ds(h * GLA_DK, GLA_DK)
        vc = pl.ds(h * GLA_DV, GLA_DV)
        q = q_ref[:, kc].astype(F32) * (GLA_DK ** -0.5)
        k = k_ref[:, kc].astype(F32)
        o, new_state = _gla_head_rows(q, k, v_ref[:, vc], b_all[:, h * GLA_DK:(h + 1) * GLA_DK], state_ref[h],
                                      mask, chunk_rows, reverse, exact_scratch)
        state_ref[h] = new_state
        emit(vc, o)


def _gla_scan_step(path_ref, step_block, refs, state_ref, b_scr, k_scr, reverse, emit):
    @pl.when(pl.program_id(2) == 0)
    def _():
        state_ref[...] = jnp.zeros_like(state_ref)

    for sb in (range(GLA_STEP_BLOCKS - 1, -1, -1) if reverse else range(GLA_STEP_BLOCKS)):
        rows = pl.ds(sb * GLA_ROWS, GLA_ROWS)
        sub = tuple(r.at[rows, :] for r in refs)
        emit_rows = functools.partial(emit, rows)
        path = path_ref[step_block * GLA_STEP_BLOCKS + sb]
        pl.when(path == GLA_PATH_WIDE)(
            functools.partial(_gla_scan_rows, *sub, state_ref, GLA_ROWS, reverse, emit_rows, None))
        pl.when(path == GLA_PATH_CHUNKED)(
            functools.partial(_gla_scan_rows, *sub, state_ref, GLA_CHUNK, reverse, emit_rows, None))
        pl.when(path == GLA_PATH_EXACT)(
            functools.partial(_gla_scan_rows, *sub, state_ref, GLA_CHUNK, reverse, emit_rows, (b_scr, k_scr)))


def _gla_fwd_kernel(exact_ref, q_ref, k_ref, v_ref, g_ref, o_ref, state_ref, b_scr, k_scr, *, nb):
    def emit(rows, cols, o):
        o_ref[rows, cols] = o

    step_block = pl.program_id(0) * nb + pl.program_id(2)
    _gla_scan_step(exact_ref, step_block, (q_ref, k_ref, v_ref, g_ref), state_ref, b_scr, k_scr, False, emit)


def _gla_bwd_kernel(exact_ref, q_ref, k_ref, v_ref, g_ref, fwd_ref, gate_ref, gain_ref, o_ref,
                    state_ref, b_scr, k_scr, *, nb):
    def emit(rows, cols, o):
        tot = fwd_ref[rows, cols] + o
        ms = jnp.mean(tot * tot, axis=-1, keepdims=True)
        y = tot * lax.rsqrt(ms + NORM_EPS) * gain_ref[...]
        o_ref[rows, cols] = (y * gate_ref[rows, cols].astype(F32)).astype(BF16)

    step_block = pl.program_id(0) * nb + (nb - 1 - pl.program_id(2))
    _gla_scan_step(exact_ref, step_block, (q_ref, k_ref, v_ref, g_ref), state_ref, b_scr, k_scr, True, emit)


def _gla(h_qkv, h_rest, log_a, exact, gain, batch, seq):
    t = batch * seq
    step_rows = GLA_STEP_BLOCKS * GLA_ROWS
    assert seq % step_rows == 0, (seq, step_rows)
    nb = seq // step_rows
    hp = GLA_HEADS_PER_STEP
    qk_w, v_w = hp * GLA_DK, hp * GLA_DV
    k_off = GLA_QK_WIDTH // qk_w
    v_off = 2 * GLA_QK_WIDTH // v_w
    gate_off = H_GLA_GATE * GLA_WIDTH // v_w
    dir_off = GLA_QK_WIDTH // qk_w
    grid = (batch, GLA_HEADS // hp, nb)
    scratch = [pltpu.VMEM((hp, GLA_DK, GLA_DV), F32),
               pltpu.VMEM((GLA_CHUNK, GLA_DK), F32), pltpu.VMEM((GLA_CHUNK, GLA_DK), F32)]

    def specs(rowmap, direction):
        return [pl.BlockSpec((step_rows, qk_w), lambda b, h, n, e: (rowmap(b, n), h)),
                pl.BlockSpec((step_rows, qk_w), lambda b, h, n, e: (rowmap(b, n), k_off + h)),
                pl.BlockSpec((step_rows, v_w), lambda b, h, n, e: (rowmap(b, n), v_off + h)),
                pl.BlockSpec((step_rows, qk_w), lambda b, h, n, e: (rowmap(b, n), direction * dir_off + h))]

    fmap = lambda b, n: b * nb + n
    rmap = lambda b, n: b * nb + (nb - 1 - n)
    fwd = pl.pallas_call(
        functools.partial(_gla_fwd_kernel, nb=nb),
        grid_spec=pltpu.PrefetchScalarGridSpec(
            num_scalar_prefetch=1,
            grid=grid,
            in_specs=specs(fmap, 0),
            out_specs=pl.BlockSpec((step_rows, v_w), lambda b, h, n, e: (fmap(b, n), h)),
            scratch_shapes=scratch),
        out_shape=jax.ShapeDtypeStruct((t, GLA_WIDTH), F32),
        compiler_params=_params("parallel", "parallel", "arbitrary"),
        name="gla_fwd",
    )(exact, h_qkv, h_qkv, h_qkv, log_a)
    return pl.pallas_call(
        functools.partial(_gla_bwd_kernel, nb=nb),
        grid_spec=pltpu.PrefetchScalarGridSpec(
            num_scalar_prefetch=1,
            grid=grid,
            in_specs=specs(rmap, 1) + [
                pl.BlockSpec((step_rows, v_w), lambda b, h, n, e: (rmap(b, n), h)),
                pl.BlockSpec((step_rows, v_w), lambda b, h, n, e: (rmap(b, n), gate_off + h)),
                pl.BlockSpec((1, GLA_DV), lambda b, h, n, e: (0, 0))],
            out_specs=pl.BlockSpec((step_rows, v_w), lambda b, h, n, e: (rmap(b, n), h)),
            scratch_shapes=scratch),
        out_shape=jax.ShapeDtypeStruct((t, GLA_WIDTH), BF16),
        compiler_params=_params("parallel", "parallel", "arbitrary"),
        name="gla_bwd",
    )(exact, h_qkv, h_qkv, h_qkv, log_a, fwd, h_rest, gain.reshape(1, GLA_DV))


def _sgu_kernel(u_ref, v_ref, gate_ref, lng_ref, lnb_ref, w_ref, bt_ref, o_ref):
    for c in range(SGU_ROWS // SGU_CHUNK):
        rows = pl.ds(c * SGU_CHUNK, SGU_CHUNK)
        x = v_ref[rows, :].astype(F32)
        mu = jnp.mean(x, axis=-1, keepdims=True)
        xc = x - mu
        var = jnp.mean(xc * xc, axis=-1, keepdims=True)
        y = (xc * lax.rsqrt(var + LN_EPS) * lng_ref[...] + lnb_ref[...]).astype(BF16)
        for g in range(SGU_GROUPS):
            cols = pl.ds(g * LANES, LANES)
            mixed = jnp.dot(w_ref[g], y[:, g * LANES:(g + 1) * LANES], preferred_element_type=F32)
            mixed = mixed + bt_ref[:, g:g + 1]
            o_ref[rows, cols] = (u_ref[rows, cols].astype(F32) * mixed
                                 * gate_ref[rows, cols].astype(F32)).astype(BF16)


def _sgu(h_rest, ln_gain, ln_bias, w, b):
    t = h_rest.shape[0]
    blk = lambda c: pl.BlockSpec((SGU_ROWS, SGU_WIDTH), lambda i: (i, c))
    return pl.pallas_call(
        _sgu_kernel,
        grid=(t // SGU_ROWS,),
        in_specs=[blk(H_SGU_U), blk(H_SGU_V), blk(H_SGU_GATE),
                  pl.BlockSpec((1, SGU_WIDTH), lambda i: (0, 0)),
                  pl.BlockSpec((1, SGU_WIDTH), lambda i: (0, 0)),
                  pl.BlockSpec((SGU_GROUPS, SGU_CHUNK, SGU_CHUNK), lambda i: (0, 0, 0)),
                  pl.BlockSpec((SGU_CHUNK, SGU_GROUPS), lambda i: (0, 0))],
        out_specs=pl.BlockSpec((SGU_ROWS, SGU_WIDTH), lambda i: (i, 0)),
        out_shape=jax.ShapeDtypeStruct((t, SGU_WIDTH), BF16),
        compiler_params=_params("parallel"),
        name="sgu",
    )(h_rest, h_rest, h_rest, ln_gain.reshape(1, SGU_WIDTH), ln_bias.reshape(1, SGU_WIDTH),
      w.astype(BF16), jnp.transpose(b))


def _attn_kernel(sink_ref, q_ref, gate_ref, kp_ref, kc_ref, kn_ref, vp_ref, vc_ref, vn_ref, o_ref, *, n_steps):
    n = pl.program_id(1)
    log2e = math.log2(math.e)
    qpos = lax.broadcasted_iota(jnp.int32, (WINDOW, 3 * WINDOW), 0)
    kpos = lax.broadcasted_iota(jnp.int32, (WINDOW, 3 * WINDOW), 1) - WINDOW
    band = jnp.abs(kpos - qpos) <= WINDOW
    full = slice(None)
    block = lambda c: pl.ds(c * WINDOW, WINDOW)

    items = []
    for a in range(ATTN_BLOCKS):
        prev = (kp_ref, vp_ref, full) if a == 0 else (kc_ref, vc_ref, block(a - 1))
        nxt = (kn_ref, vn_ref, full) if a == ATTN_BLOCKS - 1 else (kc_ref, vc_ref, block(a + 1))
        keys = [(prev[0], prev[2]), (kc_ref, block(a)), (nxt[0], nxt[2])]
        vals = [(prev[1], prev[2]), (vc_ref, block(a)), (nxt[1], nxt[2])]
        valid = band
        if a == 0:
            valid = jnp.logical_and(valid, jnp.logical_or(kpos >= 0, n > 0))
        if a == ATTN_BLOCKS - 1:
            valid = jnp.logical_and(valid, jnp.logical_or(kpos < WINDOW, n < n_steps - 1))
        valid = jnp.concatenate([valid] * ATT_GROUP, axis=0)
        items += [(block(a), keys, vals, valid, hk) for hk in range(ATT_KV_HEADS)]

    def scores(item):
        rows, keys, _, _, hk = item
        kcols = pl.ds(hk * HEAD_DIM, HEAD_DIM)
        q = jnp.concatenate([q_ref[rows, pl.ds((hk * ATT_GROUP + g) * HEAD_DIM, HEAD_DIM)]
                             for g in range(ATT_GROUP)], axis=0)
        k = jnp.concatenate([ref[r, kcols] for ref, r in keys], axis=0)
        return lax.dot_general(q, k, (((1,), (1,)), ((), ())), preferred_element_type=F32)

    def finish(item, s):
        rows, _, vals, valid, hk = item
        kcols = pl.ds(hk * HEAD_DIM, HEAD_DIM)
        v = jnp.concatenate([ref[r, kcols] for ref, r in vals], axis=0)
        s = jnp.where(valid, s * (HEAD_DIM ** -0.5 * log2e), -jnp.inf)
        sink = jnp.concatenate([jnp.full((WINDOW, 1), sink_ref[hk * ATT_GROUP + g] * log2e, F32)
                                for g in range(ATT_GROUP)], axis=0)
        m = jnp.maximum(jnp.max(s, axis=-1, keepdims=True), sink)
        e = jnp.exp2(s - m)
        denom = jnp.sum(e, axis=-1, keepdims=True) + jnp.exp2(sink - m)
        o = jnp.dot(e.astype(BF16), v, preferred_element_type=F32) / denom
        for g in range(ATT_GROUP):
            cols = pl.ds((hk * ATT_GROUP + g) * HEAD_DIM, HEAD_DIM)
            o_ref[rows, cols] = (o[g * WINDOW:(g + 1) * WINDOW, :] * gate_ref[rows, cols].astype(F32)).astype(BF16)

    pending = [scores(item) for item in items[:ATTN_LOOKAHEAD]]
    for i, item in enumerate(items):
        if i + ATTN_LOOKAHEAD < len(items):
            pending.append(scores(items[i + ATTN_LOOKAHEAD]))
        finish(item, pending.pop(0))


def _attention(h_rest, sink, batch, seq):
    t = batch * seq
    rows = ATTN_BLOCKS * WINDOW
    assert seq % rows == 0, (seq, rows)
    n_steps = seq // rows
    nb = seq // WINDOW
    cur = lambda b, n: b * n_steps + n
    prv = lambda b, n: b * nb + jnp.maximum(n * ATTN_BLOCKS - 1, 0)
    nxt = lambda b, n: b * nb + jnp.minimum((n + 1) * ATTN_BLOCKS, nb - 1)
    wide = lambda blk: pl.BlockSpec((rows, ATT_WIDTH), lambda b, n: (cur(b, n), blk))
    kv_cur = lambda blk: pl.BlockSpec((rows, ATT_KV_WIDTH), lambda b, n: (cur(b, n), blk))
    kv_edge = lambda rowmap, blk: pl.BlockSpec((WINDOW, ATT_KV_WIDTH), lambda b, n: (rowmap(b, n), blk))
    return pl.pallas_call(
        functools.partial(_attn_kernel, n_steps=n_steps),
        grid=(batch, n_steps),
        in_specs=[pl.BlockSpec(memory_space=pltpu.SMEM),
                  wide(H_ATT_Q), wide(H_ATT_GATE),
                  kv_edge(prv, H_ATT_K), kv_cur(H_ATT_K), kv_edge(nxt, H_ATT_K),
                  kv_edge(prv, H_ATT_V), kv_cur(H_ATT_V), kv_edge(nxt, H_ATT_V)],
        out_specs=pl.BlockSpec((rows, ATT_WIDTH), lambda b, n: (cur(b, n), 0)),
        out_shape=jax.ShapeDtypeStruct((t, ATT_WIDTH), BF16),
        compiler_params=_params("parallel", "parallel"),
        name="window_attention",
    )(sink, h_rest, h_rest, h_rest, h_rest, h_rest, h_rest, h_rest, h_rest)


def _merge_kernel(x_ref, oa_ref, ob_ref, oc_ref, g0_ref, g1_ref, g2_ref, wbr_ref, bias_ref, o_ref):
    x = x_ref[...]
    acc = None
    row0 = 0
    for i, (g_ref, o_in) in enumerate(((g0_ref, oa_ref), (g1_ref, ob_ref), (g2_ref, oc_ref))):
        width = o_in.shape[1]
        gate = jax.nn.sigmoid(jnp.dot(x, g_ref[...], preferred_element_type=F32) + bias_ref[i:i + 1, :])
        term = gate * jnp.dot(o_in[...], wbr_ref[row0:row0 + width, :], preferred_element_type=F32)
        acc = term if acc is None else acc + term
        row0 += width
    o_ref[...] = acc.astype(BF16)


def _merge(xn, o_a, o_b, o_c, w_rest, w_br, gate_bias):
    t = xn.shape[0]
    tm = min(MERGE_TM, t)
    nj = D_MODEL // MERGE_TN
    g0 = REST_ACT_WIDTH // MERGE_TN
    row = lambda width: pl.BlockSpec((tm, width), lambda i, j: (i, 0))
    gcol = lambda br: pl.BlockSpec((D_MODEL, MERGE_TN), lambda i, j: (0, g0 + br * nj + j))
    return pl.pallas_call(
        _merge_kernel,
        grid=(t // tm, nj),
        in_specs=[row(D_MODEL), row(GLA_WIDTH), row(SGU_WIDTH), row(ATT_WIDTH),
                  gcol(0), gcol(1), gcol(2),
                  pl.BlockSpec((GLA_WIDTH + SGU_WIDTH + ATT_WIDTH, MERGE_TN), lambda i, j: (0, j)),
                  pl.BlockSpec((N_BRANCH, MERGE_TN), lambda i, j: (0, j))],
        out_specs=pl.BlockSpec((tm, MERGE_TN), lambda i, j: (i, j)),
        out_shape=jax.ShapeDtypeStruct((t, D_MODEL), BF16),
        compiler_params=_params("parallel", "arbitrary"),
        name="gated_merge",
    )(xn, o_a, o_b, o_c, w_rest, w_rest, w_rest, w_br, gate_bias)


def _out_kernel(m_ref, w_ref, x_ref, o_ref):
    o_ref[...] = x_ref[...] + jnp.dot(m_ref[...], w_ref[...], preferred_element_type=F32)


def _out_proj(merged, w_out, x2):
    t = x2.shape[0]
    tm = min(OUT_TM, t)
    return pl.pallas_call(
        _out_kernel,
        grid=(t // tm, D_MODEL // OUT_TN),
        in_specs=[pl.BlockSpec((tm, D_MODEL), lambda i, j: (i, 0)),
                  pl.BlockSpec((D_MODEL, OUT_TN), lambda i, j: (0, j)),
                  pl.BlockSpec((tm, OUT_TN), lambda i, j: (i, j))],
        out_specs=pl.BlockSpec((tm, OUT_TN), lambda i, j: (i, j)),
        out_shape=jax.ShapeDtypeStruct((t, D_MODEL), F32),
        compiler_params=_params("parallel", "arbitrary"),
        name="out_proj",
    )(merged, w_out, x2)


def _cast_kernel(x_ref, o_ref):
    o_ref[...] = x_ref[...].astype(BF16)


def _cast_bf16(w, layer, name):
    rows, cols = w.shape[1], w.shape[2]
    return pl.pallas_call(
        _cast_kernel,
        grid=(rows // CAST_ROWS, cols // CAST_TN),
        in_specs=[pl.BlockSpec((None, CAST_ROWS, CAST_TN), lambda r, j: (layer, r, j))],
        out_specs=pl.BlockSpec((CAST_ROWS, CAST_TN), lambda r, j: (r, j)),
        out_shape=jax.ShapeDtypeStruct((rows, cols), BF16),
        compiler_params=_params("parallel", "parallel"),
        name=name,
    )(w)


def _cast_t_kernel(x_ref, o_ref):
    o_ref[...] = jnp.transpose(x_ref[...]).astype(BF16)


def _cast_t_shift_kernel(a_ref, b_ref, o_ref):
    both = jnp.concatenate([a_ref[CAST_SHIFT:, :], b_ref[...]], axis=0)
    o_ref[...] = jnp.transpose(both).astype(BF16)


def _cast_t_head_kernel(x_ref, o_ref):
    xt = jnp.transpose(x_ref[...])
    lane = lax.broadcasted_iota(jnp.int32, xt.shape, 1)
    o_ref[...] = jnp.where(lane < CAST_SHIFT, xt, 0.0).astype(BF16)


def _cast_t_bf16(wt, layer, row0, n_rows, name):
    k = wt.shape[2]
    tile0 = row0 // CAST_TN
    return pl.pallas_call(
        _cast_t_kernel,
        grid=(k // CAST_ROWS, n_rows // CAST_TN),
        in_specs=[pl.BlockSpec((None, CAST_TN, CAST_ROWS), lambda r, j: (layer, tile0 + j, r))],
        out_specs=pl.BlockSpec((CAST_ROWS, CAST_TN), lambda r, j: (r, j)),
        out_shape=jax.ShapeDtypeStruct((k, n_rows), BF16),
        compiler_params=_params("parallel", "parallel"),
        name=name,
    )(wt)


def _cast_t_bf16_shifted(wt, layer, row0, name):
    n, k = wt.shape[1], wt.shape[2]
    n_rows = n - row0 - CAST_SHIFT
    tile0 = row0 // CAST_TN
    shifts_per_tile = CAST_TN // CAST_SHIFT
    return pl.pallas_call(
        _cast_t_shift_kernel,
        grid=(k // CAST_ROWS, n_rows // CAST_TN),
        in_specs=[pl.BlockSpec((None, CAST_TN, CAST_ROWS), lambda r, j: (layer, tile0 + j, r)),
                  pl.BlockSpec((None, CAST_SHIFT, CAST_ROWS),
                               lambda r, j: (layer, (tile0 + j + 1) * shifts_per_tile, r))],
        out_specs=pl.BlockSpec((CAST_ROWS, CAST_TN), lambda r, j: (r, j)),
        out_shape=jax.ShapeDtypeStruct((k, n_rows), BF16),
        compiler_params=_params("parallel", "parallel"),
        name=name,
    )(wt, wt)


def _cast_t_bf16_head(wt, layer, row0, name):
    k = wt.shape[2]
    return pl.pallas_call(
        _cast_t_head_kernel,
        grid=(k // CAST_ROWS,),
        in_specs=[pl.BlockSpec((None, LANES, CAST_ROWS), lambda r: (layer, row0 // LANES, r))],
        out_specs=pl.BlockSpec((CAST_ROWS, LANES), lambda r: (r, 0)),
        out_shape=jax.ShapeDtypeStruct((k, LANES), BF16),
        compiler_params=_params("parallel"),
        name=name,
    )(wt)


def _prepare_layer(layer, norm_gain, w_in, gla_gate_up, gla_gate_bias, gla_norm_gain, sgu_ln_gain, sgu_ln_bias,
                   sgu_w, sgu_b, q_norm_gain, k_norm_gain, sink, gate_bias, w_br, w_out):
    lr0 = 2 * GLA_QK_WIDTH + GLA_WIDTH
    lr1 = lr0 + 2 * GLA_GATE_RANK
    assert lr1 - lr0 == CAST_SHIFT and lr0 % CAST_TN == 0
    w_in_t = jnp.swapaxes(w_in, 1, 2)
    up2 = jnp.zeros((LANES, 2 * GLA_QK_WIDTH), F32)
    up2 = up2.at[:GLA_GATE_RANK, :GLA_QK_WIDTH].set(gla_gate_up[layer, 0])
    up2 = up2.at[GLA_GATE_RANK:2 * GLA_GATE_RANK, GLA_QK_WIDTH:].set(gla_gate_up[layer, 1])
    return dict(
        norm_gain=norm_gain[layer],
        w_qkv=_cast_t_bf16(w_in_t, layer, 0, lr0, "cast_w_qkv"),
        w_rest=_cast_t_bf16_shifted(w_in_t, layer, lr0, "cast_w_rest"),
        w_lr=_cast_t_bf16_head(w_in_t, layer, lr0, "cast_w_lr"), up2=up2.astype(BF16),
        bias2=gla_gate_bias[layer].reshape(1, 2 * GLA_QK_WIDTH), gla_norm_gain=gla_norm_gain[layer],
        sgu_ln_gain=sgu_ln_gain[layer], sgu_ln_bias=sgu_ln_bias[layer], sgu_w=sgu_w[layer], sgu_b=sgu_b[layer],
        q_gain=q_norm_gain[layer].reshape(1, HEAD_DIM), k_gain=k_norm_gain[layer].reshape(1, HEAD_DIM),
        sink=sink[layer], gate_bias=gate_bias[layer],
        w_br=_cast_bf16(w_br, layer, "cast_w_br"),
        w_out=_cast_bf16(w_out, layer, "cast_w_out"))


def _layer(x, p):
    batch, seq, _ = x.shape
    t = batch * seq
    x2 = x.reshape(t, D_MODEL)
    xn = _pre_norm(x2, p["norm_gain"])

    h_qkv = _proj_raw(xn, p["w_qkv"], "proj_gla_qkv")
    h_rest = _proj_rest(xn, p["w_rest"], p["q_gain"], p["k_gain"], seq)
    log_a, gla_exact = _gla_log_decay(xn, p["w_lr"], p["up2"], p["bias2"])

    o_a = _gla(h_qkv, h_rest, log_a, gla_exact, p["gla_norm_gain"], batch, seq)
    o_b = _sgu(h_rest, p["sgu_ln_gain"], p["sgu_ln_bias"], p["sgu_w"], p["sgu_b"])
    o_c = _attention(h_rest, p["sink"], batch, seq)

    merged = _merge(xn, o_a, o_b, o_c, p["w_rest"], p["w_br"], p["gate_bias"])
    return _out_proj(merged, p["w_out"], x2).reshape(batch, seq, D_MODEL)


def kernel(x_prompt, x_sample, norm_gain, w_in, gla_gate_up, gla_gate_bias, gla_norm_gain, sgu_ln_gain,
           sgu_ln_bias, sgu_w, sgu_b, q_norm_gain, k_norm_gain, sink, gate_bias, w_br, w_out):
    y_prompt, y_sample = x_prompt, x_sample
    for l in range(norm_gain.shape[0]):
        p = _prepare_layer(l, norm_gain, w_in, gla_gate_up, gla_gate_bias, gla_norm_gain, sgu_ln_gain,
                           sgu_ln_bias, sgu_w, sgu_b, q_norm_gain, k_norm_gain, sink, gate_bias, w_br, w_out)
        y_prompt = _layer(y_prompt, p)
        y_sample = _layer(y_sample, p)
    return (y_prompt, y_sample)
```

```python
import functools
import math

import jax
import jax.numpy as jnp
from jax import lax
from jax.experimental import pallas as pl
from jax.experimental.pallas import tpu as pltpu

F32 = jnp.float32
BF16 = jnp.bfloat16

D_MODEL = 4096
HEAD_DIM = 128
GLA_WIDTH = 1536
GLA_DV = 256
GLA_HEADS = 6
GLA_DK = 128
GLA_QK_WIDTH = 768
GLA_GATE_RANK = 16
GLA_GATE_TAU = 16.0
SGU_WIDTH = 1024
SGU_CHUNK = 128
SGU_GROUPS = 8
ATT_WIDTH = 1536
ATT_Q_HEADS = 12
ATT_KV_HEADS = 4
ATT_GROUP = 3
ATT_KV_WIDTH = 512
WINDOW = 128
ROPE_THETA = 500000.0
ROPE_DIMS = 32
N_BRANCH = 3
NORM_EPS = 1e-6
LN_EPS = 1e-5

VMEM_LIMIT_BYTES = 56 * 1024 * 1024
LANES = 128
SUBLANES = 8

NORM_ROWS = 512
PROJ_TM = 1024
PROJ_REST_TM = 2048
PROJ_TN = 512
PROJ_RAW_TN = 1024
PROJ_SUB_ROWS = 256
GLA_ROWS = 256
GLA_STEP_BLOCKS = 2
GLA_HEADS_PER_STEP = 6
GLA_CHUNK = 64
GLA_FAST_MAX_DECAY = 40.0
GLA_PATH_WIDE, GLA_PATH_CHUNKED, GLA_PATH_EXACT = 0, 1, 2
SGU_ROWS = 1024
ATTN_BLOCKS = 8
ATTN_LOOKAHEAD = 2
MERGE_TM = 512
MERGE_TN = 512
OUT_TM = 1024
OUT_TN = 1024
CAST_ROWS = 2048
CAST_TN = 512
CAST_SHIFT = 2 * GLA_GATE_RANK

KIND_SILU, KIND_GELU, KIND_QNORM, KIND_KNORM, KIND_RAW = range(5)
_REST_KIND = ([KIND_SILU] * 3 + [KIND_GELU] * 4 + [KIND_SILU] * 2 + [KIND_QNORM] * 3 + [KIND_KNORM, KIND_RAW]
              + [KIND_SILU] * 3)
_REST_OUT_TILE = [6, 7, 8, 0, 1, 2, 3, 4, 5, 9, 10, 11, 15, 16, 12, 13, 14]
REST_ACT_WIDTH = len(_REST_KIND) * PROJ_TN
H_SGU_U, H_SGU_V, H_SGU_GATE = 0, 1, 2
H_GLA_GATE, H_ATT_Q, H_ATT_GATE = 2, 3, 4
H_ATT_K, H_ATT_V = 15, 16


def _params(*sem):
    return pltpu.CompilerParams(dimension_semantics=sem, vmem_limit_bytes=VMEM_LIMIT_BYTES)


def _silu(x):
    return x * jax.nn.sigmoid(x)


def _gelu_tanh(x):
    c = math.sqrt(2.0 / math.pi)
    return x * (0.5 * (1.0 + jnp.tanh(c * (x + 0.044715 * (x * x * x)))))


def _norm_kernel(x_ref, g_ref, o_ref):
    x = x_ref[...]
    ms = jnp.mean(x * x, axis=-1, keepdims=True)
    o_ref[...] = (x * lax.rsqrt(ms + NORM_EPS) * g_ref[...]).astype(BF16)


def _pre_norm(x2, gain):
    t = x2.shape[0]
    return pl.pallas_call(
        _norm_kernel,
        grid=(t // NORM_ROWS,),
        in_specs=[pl.BlockSpec((NORM_ROWS, D_MODEL), lambda i: (i, 0)),
                  pl.BlockSpec((1, D_MODEL), lambda i: (0, 0))],
        out_specs=pl.BlockSpec((NORM_ROWS, D_MODEL), lambda i: (i, 0)),
        out_shape=jax.ShapeDtypeStruct((t, D_MODEL), BF16),
        compiler_params=_params("parallel"),
        name="pre_norm",
    )(x2, gain.reshape(1, D_MODEL))


def _dot_epilogue(x_ref, w_ref, o_ref, epilogue):
    w = w_ref[...]
    for r in range(x_ref.shape[0] // PROJ_SUB_ROWS):
        rows = pl.ds(r * PROJ_SUB_ROWS, PROJ_SUB_ROWS)
        acc = jnp.dot(x_ref[rows, :], w, preferred_element_type=F32)
        o_ref[rows, :] = epilogue(acc, rows).astype(BF16)


def _proj_raw_kernel(x_ref, w_ref, o_ref):
    _dot_epilogue(x_ref, w_ref, o_ref, lambda acc, rows: acc)


def _proj_raw(xn, w, name):
    t = xn.shape[0]
    n = w.shape[1]
    tm = min(PROJ_TM, t)
    return pl.pallas_call(
        _proj_raw_kernel,
        grid=(t // tm, n // PROJ_RAW_TN),
        in_specs=[pl.BlockSpec((tm, D_MODEL), lambda i, j: (i, 0)),
                  pl.BlockSpec((D_MODEL, PROJ_RAW_TN), lambda i, j: (0, j))],
        out_specs=pl.BlockSpec((tm, PROJ_RAW_TN), lambda i, j: (i, j)),
        out_shape=jax.ShapeDtypeStruct((t, n), BF16),
        compiler_params=_params("parallel", "arbitrary"),
        name=name,
    )(xn, w)


def _head_norm_rope(acc, gain, cos_t, sin_lo, sin_hi):
    outs = []
    for h in range(PROJ_TN // HEAD_DIM):
        xh = acc[:, h * HEAD_DIM:(h + 1) * HEAD_DIM]
        ms = jnp.mean(xh * xh, axis=-1, keepdims=True)
        y = xh * lax.rsqrt(ms + NORM_EPS) * gain
        half = ROPE_DIMS // 2
        up = pltpu.roll(y, HEAD_DIM - half, axis=1)
        dn = pltpu.roll(y, half, axis=1)
        outs.append(y * cos_t + up * sin_lo + dn * sin_hi)
    return jnp.concatenate(outs, axis=1)


def _proj_rest_kernel(kind_ref, tile_ref, x_ref, w_ref, qg_ref, kg_ref, cos_ref, slo_ref, shi_ref, o_ref):
    del tile_ref
    kind = kind_ref[pl.program_id(1)]

    def norm_rope(gain_ref):
        return lambda acc, rows: _head_norm_rope(acc, gain_ref[...], cos_ref[rows, :], slo_ref[rows, :],
                                                 shi_ref[rows, :])

    epilogues = {
        KIND_SILU: lambda acc, rows: _silu(acc),
        KIND_GELU: lambda acc, rows: _gelu_tanh(acc),
        KIND_QNORM: norm_rope(qg_ref),
        KIND_KNORM: norm_rope(kg_ref),
        KIND_RAW: lambda acc, rows: acc,
    }
    for k, epilogue in epilogues.items():
        pl.when(kind == k)(functools.partial(_dot_epilogue, x_ref, w_ref, o_ref, epilogue))


def _proj_rest(xn, w_rest, q_gain, k_gain, seq):
    t = xn.shape[0]
    tm = min(PROJ_REST_TM, t)
    blocks_per_seq = seq // tm if seq >= tm else 1
    tables = _rope_tables(seq)
    if seq < tm:
        tables = tuple(jnp.tile(tb, (tm // seq, 1)) for tb in tables)
    vec = pl.BlockSpec((1, HEAD_DIM), lambda i, j, kind, tile: (0, 0))
    tab = pl.BlockSpec((tm, HEAD_DIM), lambda i, j, kind, tile: (i % blocks_per_seq, 0))
    return pl.pallas_call(
        _proj_rest_kernel,
        grid_spec=pltpu.PrefetchScalarGridSpec(
            num_scalar_prefetch=2,
            grid=(t // tm, len(_REST_KIND)),
            in_specs=[pl.BlockSpec((tm, D_MODEL), lambda i, j, kind, tile: (i, 0)),
                      pl.BlockSpec((D_MODEL, PROJ_TN), lambda i, j, kind, tile: (0, j)),
                      vec, vec, tab, tab, tab],
            out_specs=pl.BlockSpec((tm, PROJ_TN), lambda i, j, kind, tile: (i, tile[j]))),
        out_shape=jax.ShapeDtypeStruct((t, REST_ACT_WIDTH), BF16),
        compiler_params=_params("parallel", "arbitrary"),
        name="proj_rest",
    )(jnp.asarray(_REST_KIND, jnp.int32), jnp.asarray(_REST_OUT_TILE, jnp.int32),
      xn, w_rest, q_gain, k_gain, *tables)


def _rope_tables(seq):
    inv_freq = ROPE_THETA ** (-jnp.arange(0, ROPE_DIMS, 2, dtype=F32) / ROPE_DIMS)
    ang = jnp.arange(seq, dtype=F32)[:, None] * inv_freq[None, :]
    cos, sin = jnp.cos(ang), jnp.sin(ang)
    half = ROPE_DIMS // 2
    ones = jnp.ones((seq, HEAD_DIM - ROPE_DIMS), F32)
    zeros = jnp.zeros((seq, HEAD_DIM - ROPE_DIMS), F32)
    zh = jnp.zeros((seq, half), F32)
    cos_t = jnp.concatenate([cos, cos, ones], axis=1)
    sin_lo = jnp.concatenate([-sin, zh, zeros], axis=1)
    sin_hi = jnp.concatenate([zh, sin, zeros], axis=1)
    return cos_t, sin_lo, sin_hi


def _decay_kernel(x_ref, w_ref, up_ref, b_ref, o_ref, mx_ref):
    full_max = lambda v: jnp.broadcast_to(jnp.max(jnp.max(v, axis=1, keepdims=True), axis=0, keepdims=True),
                                          (1, LANES))
    n_blocks = x_ref.shape[0] // GLA_ROWS
    step_max, block_max = [], []
    for r in range(n_blocks):
        rows = pl.ds(r * GLA_ROWS, GLA_ROWS)
        lr = jnp.dot(x_ref[rows, :], w_ref[...], preferred_element_type=F32)
        z = jnp.dot(lr.astype(BF16), up_ref[...], preferred_element_type=F32) + b_ref[...]
        log_a = (jnp.minimum(z, 0.0) - jnp.log(1.0 + jnp.exp(-jnp.abs(z)))) * (1.0 / GLA_GATE_TAU)
        o_ref[rows, :] = log_a
        step_max.append(full_max(-log_a))
        block_max.append(full_max(-jnp.sum(log_a, axis=0, keepdims=True)))
    zero = [jnp.zeros((1, LANES), F32)] * (SUBLANES // 2 - n_blocks)
    mx_ref[0] = jnp.concatenate(step_max + zero + block_max + zero, axis=0)


def _gla_log_decay(xn, w_lr, up2, bias2):
    t = xn.shape[0]
    tm = min(PROJ_TM, t)
    n = 2 * GLA_QK_WIDTH
    log_a, mx = pl.pallas_call(
        _decay_kernel,
        grid=(t // tm,),
        in_specs=[pl.BlockSpec((tm, D_MODEL), lambda i: (i, 0)),
                  pl.BlockSpec((D_MODEL, LANES), lambda i: (0, 0)),
                  pl.BlockSpec((LANES, n), lambda i: (0, 0)),
                  pl.BlockSpec((1, n), lambda i: (0, 0))],
        out_specs=[pl.BlockSpec((tm, n), lambda i: (i, 0)),
                   pl.BlockSpec((1, SUBLANES, LANES), lambda i: (i, 0, 0))],
        out_shape=[jax.ShapeDtypeStruct((t, n), F32),
                   jax.ShapeDtypeStruct((t // tm, SUBLANES, LANES), F32)],
        compiler_params=_params("parallel"),
        name="gla_log_decay",
    )(xn, w_lr, up2, bias2)
    n_blocks = tm // GLA_ROWS
    step_max = mx[:, :n_blocks, 0].reshape(t // GLA_ROWS)
    block_max = mx[:, SUBLANES // 2:SUBLANES // 2 + n_blocks, 0].reshape(t // GLA_ROWS)
    path = jnp.where(block_max <= GLA_FAST_MAX_DECAY, GLA_PATH_WIDE,
                     jnp.where(step_max <= GLA_FAST_MAX_DECAY / GLA_CHUNK, GLA_PATH_CHUNKED, GLA_PATH_EXACT))
    return log_a, path.astype(jnp.int32)


def _gla_head_rows(q, k, v, b, state, mask, chunk_rows, reverse, exact_scratch):
    n_chunks = q.shape[0] // chunk_rows
    chunk = lambda x, c: x[c * chunk_rows:(c + 1) * chunk_rows]
    edge = 0 if reverse else chunk_rows - 1
    b_tot = [chunk(b, c)[edge:edge + 1, :] for c in range(n_chunks)]
    dec_row = [jnp.exp(bt) for bt in b_tot]
    qb = (q * jnp.exp(b)).astype(BF16)

    if exact_scratch is None:
        k_inv = k * jnp.exp(-b)
        s = lax.dot_general(qb, k_inv.astype(BF16), (((1,), (1,)), ((), ())), preferred_element_type=F32)
        o = jnp.dot(jnp.where(mask, s, 0.0).astype(BF16), v, preferred_element_type=F32)
        kb = [chunk(k_inv, c) * dec_row[c] for c in range(n_chunks)]
    else:
        b_scr, k_scr = exact_scratch
        col = lax.broadcasted_iota(jnp.int32, (chunk_rows, chunk_rows), 1)
        o_chunks, kb = [], []
        for c in range(n_chunks):
            qc, bc = chunk(q, c), chunk(b, c)
            b_scr[...] = bc
            k_scr[...] = chunk(k, c)

            def column(j, acc, qc=qc, bc=bc):
                decay = jnp.exp(jnp.minimum(bc - b_scr[pl.ds(j, 1), :], 0.0))
                w = qc * decay * k_scr[pl.ds(j, 1), :]
                return jnp.where(col == j, jnp.sum(w, axis=-1, keepdims=True), acc)

            s = lax.fori_loop(0, chunk_rows, column, jnp.zeros((chunk_rows, chunk_rows), F32))
            lo = c * chunk_rows
            p = jnp.where(mask[lo:lo + chunk_rows, lo:lo + chunk_rows], s, 0.0)
            o_chunks.append(jnp.dot(p.astype(BF16), chunk(v, c), preferred_element_type=F32))
            kb.append(chunk(k, c) * jnp.exp(b_tot[c] - bc))
        o = jnp.concatenate(o_chunks, axis=0)

    o_state = [None] * n_chunks
    for c in (range(n_chunks - 1, -1, -1) if reverse else range(n_chunks)):
        o_state[c] = jnp.dot(chunk(qb, c), state.astype(BF16), preferred_element_type=F32)
        upd = lax.dot_general(kb[c].astype(BF16), chunk(v, c), (((0,), (0,)), ((), ())),
                              preferred_element_type=F32)
        dec = jnp.transpose(jnp.broadcast_to(dec_row[c], (GLA_DK, GLA_DK)))
        state = jnp.concatenate([dec] * (GLA_DV // GLA_DK), axis=1) * state + upd
    return o + jnp.concatenate(o_state, axis=0), state


def _gla_scan_rows(q_ref, k_ref, v_ref, g_ref, state_ref, chunk_rows, reverse, emit, exact_scratch):
    row = lax.broadcasted_iota(jnp.int32, (GLA_ROWS, GLA_ROWS), 0)
    col = lax.broadcasted_iota(jnp.int32, (GLA_ROWS, GLA_ROWS), 1)
    shift = chunk_rows.bit_length() - 1
    mask = jnp.logical_and((col >= row) if reverse else (col <= row),
                           lax.shift_right_logical(row, shift) == lax.shift_right_logical(col, shift))
    tri_b = mask.astype(BF16)
    rest = g_ref[...]
    b_all = None
    for _ in range(3):
        term = rest.astype(BF16)
        rest = rest - term.astype(F32)
        part = jnp.dot(tri_b, term, preferred_element_type=F32)
        b_all = part if b_all is None else b_all + part
    for h in range(GLA_HEADS_PER_STEP):
        kc = pl.ds(h * GLA_DK, GLA_DK)
        vc = pl.ds(h * GLA_DV, GLA_DV)
        q = q_ref[:, kc].astype(F32) * (GLA_DK ** -0.5)
        k = k_ref[:, kc].astype(F32)
        o, new_state = _gla_head_rows(q, k, v_ref[:, vc], b_all[:, h * GLA_DK:(h + 1) * GLA_DK], state_ref[h],
                                      mask, chunk_rows, reverse, exact_scratch)
        state_ref[h] = new_state
        emit(vc, o)


def _gla_scan_step(path_ref, step_block, refs, state_ref, b_scr, k_scr, reverse, emit):
    @pl.when(pl.program_id(2) == 0)
    def _():
        state_ref[...] = jnp.zeros_like(state_ref)

    for sb in (range(GLA_STEP_BLOCKS - 1, -1, -1) if reverse else range(GLA_STEP_BLOCKS)):
        rows = pl.ds(sb * GLA_ROWS, GLA_ROWS)
        sub = tuple(r.at[rows, :] for r in refs)
        emit_rows = functools.partial(emit, rows)
        path = path_ref[step_block * GLA_STEP_BLOCKS + sb]
        pl.when(path == GLA_PATH_WIDE)(
            functools.partial(_gla_scan_rows, *sub, state_ref, GLA_ROWS, reverse, emit_rows, None))
        pl.when(path == GLA_PATH_CHUNKED)(
            functools.partial(_gla_scan_rows, *sub, state_ref, GLA_CHUNK, reverse, emit_rows, None))
        pl.when(path == GLA_PATH_EXACT)(
            functools.partial(_gla_scan_rows, *sub, state_ref, GLA_CHUNK, reverse, emit_rows, (b_scr, k_scr)))


def _gla_fwd_kernel(exact_ref, q_ref, k_ref, v_ref, g_ref, o_ref, state_ref, b_scr, k_scr, *, nb):
    def emit(rows, cols, o):
        o_ref[rows, cols] = o

    step_block = pl.program_id(0) * nb + pl.program_id(2)
    _gla_scan_step(exact_ref, step_block, (q_ref, k_ref, v_ref, g_ref), state_ref, b_scr, k_scr, False, emit)


def _gla_bwd_kernel(exact_ref, q_ref, k_ref, v_ref, g_ref, fwd_ref, gate_ref, gain_ref, o_ref,
                    state_ref, b_scr, k_scr, *, nb):
    def emit(rows, cols, o):
        tot = fwd_ref[rows, cols] + o
        ms = jnp.mean(tot * tot, axis=-1, keepdims=True)
        y = tot * lax.rsqrt(ms + NORM_EPS) * gain_ref[...]
        o_ref[rows, cols] = (y * gate_ref[rows, cols].astype(F32)).astype(BF16)

    step_block = pl.program_id(0) * nb + (nb - 1 - pl.program_id(2))
    _gla_scan_step(exact_ref, step_block, (q_ref, k_ref, v_ref, g_ref), state_ref, b_scr, k_scr, True, emit)


def _gla(h_qkv, h_rest, log_a, exact, gain, batch, seq):
    t = batch * seq
    step_rows = GLA_STEP_BLOCKS * GLA_ROWS
    assert seq % step_rows == 0, (seq, step_rows)
    nb = seq // step_rows
    hp = GLA_HEADS_PER_STEP
    qk_w, v_w = hp * GLA_DK, hp * GLA_DV
    k_off = GLA_QK_WIDTH // qk_w
    v_off = 2 * GLA_QK_WIDTH // v_w
    gate_off = H_GLA_GATE * GLA_WIDTH // v_w
    dir_off = GLA_QK_WIDTH // qk_w
    grid = (batch, GLA_HEADS // hp, nb)
    scratch = [pltpu.VMEM((hp, GLA_DK, GLA_DV), F32),
               pltpu.VMEM((GLA_CHUNK, GLA_DK), F32), pltpu.VMEM((GLA_CHUNK, GLA_DK), F32)]

    def specs(rowmap, direction):
        return [pl.BlockSpec((step_rows, qk_w), lambda b, h, n, e: (rowmap(b, n), h)),
                pl.BlockSpec((step_rows, qk_w), lambda b, h, n, e: (rowmap(b, n), k_off + h)),
                pl.BlockSpec((step_rows, v_w), lambda b, h, n, e: (rowmap(b, n), v_off + h)),
                pl.BlockSpec((step_rows, qk_w), lambda b, h, n, e: (rowmap(b, n), direction * dir_off + h))]

    fmap = lambda b, n: b * nb + n
    rmap = lambda b, n: b * nb + (nb - 1 - n)
    fwd = pl.pallas_call(
        functools.partial(_gla_fwd_kernel, nb=nb),
        grid_spec=pltpu.PrefetchScalarGridSpec(
            num_scalar_prefetch=1,
            grid=grid,
            in_specs=specs(fmap, 0),
            out_specs=pl.BlockSpec((step_rows, v_w), lambda b, h, n, e: (fmap(b, n), h)),
            scratch_shapes=scratch),
        out_shape=jax.ShapeDtypeStruct((t, GLA_WIDTH), F32),
        compiler_params=_params("parallel", "parallel", "arbitrary"),
        name="gla_fwd",
    )(exact, h_qkv, h_qkv, h_qkv, log_a)
    return pl.pallas_call(
        functools.partial(_gla_bwd_kernel, nb=nb),
        grid_spec=pltpu.PrefetchScalarGridSpec(
            num_scalar_prefetch=1,
            grid=grid,
            in_specs=specs(rmap, 1) + [
                pl.BlockSpec((step_rows, v_w), lambda b, h, n, e: (rmap(b, n), h)),
                pl.BlockSpec((step_rows, v_w), lambda b, h, n, e: (rmap(b, n), gate_off + h)),
                pl.BlockSpec((1, GLA_DV), lambda b, h, n, e: (0, 0))],
            out_specs=pl.BlockSpec((step_rows, v_w), lambda b, h, n, e: (rmap(b, n), h)),
            scratch_shapes=scratch),
        out_shape=jax.ShapeDtypeStruct((t, GLA_WIDTH), BF16),
        compiler_params=_params("parallel", "parallel", "arbitrary"),
        name="gla_bwd",
    )(exact, h_qkv, h_qkv, h_qkv, log_a, fwd, h_rest, gain.reshape(1, GLA_DV))


def _sgu_kernel(u_ref, v_ref, gate_ref, lng_ref, lnb_ref, w_ref, bt_ref, o_ref):
    for c in range(SGU_ROWS // SGU_CHUNK):
        rows = pl.ds(c * SGU_CHUNK, SGU_CHUNK)
        x = v_ref[rows, :].astype(F32)
        mu = jnp.mean(x, axis=-1, keepdims=True)
        xc = x - mu
        var = jnp.mean(xc * xc, axis=-1, keepdims=True)
        y = (xc * lax.rsqrt(var + LN_EPS) * lng_ref[...] + lnb_ref[...]).astype(BF16)
        for g in range(SGU_GROUPS):
            cols = pl.ds(g * LANES, LANES)
            mixed = jnp.dot(w_ref[g], y[:, g * LANES:(g + 1) * LANES], preferred_element_type=F32)
            mixed = mixed + bt_ref[:, g:g + 1]
            o_ref[rows, cols] = (u_ref[rows, cols].astype(F32) * mixed
                                 * gate_ref[rows, cols].astype(F32)).astype(BF16)


def _sgu(h_rest, ln_gain, ln_bias, w, b):
    t = h_rest.shape[0]
    blk = lambda c: pl.BlockSpec((SGU_ROWS, SGU_WIDTH), lambda i: (i, c))
    return pl.pallas_call(
        _sgu_kernel,
        grid=(t // SGU_ROWS,),
        in_specs=[blk(H_SGU_U), blk(H_SGU_V), blk(H_SGU_GATE),
                  pl.BlockSpec((1, SGU_WIDTH), lambda i: (0, 0)),
                  pl.BlockSpec((1, SGU_WIDTH), lambda i: (0, 0)),
                  pl.BlockSpec((SGU_GROUPS, SGU_CHUNK, SGU_CHUNK), lambda i: (0, 0, 0)),
                  pl.BlockSpec((SGU_CHUNK, SGU_GROUPS), lambda i: (0, 0))],
        out_specs=pl.BlockSpec((SGU_ROWS, SGU_WIDTH), lambda i: (i, 0)),
        out_shape=jax.ShapeDtypeStruct((t, SGU_WIDTH), BF16),
        compiler_params=_params("parallel"),
        name="sgu",
    )(h_rest, h_rest, h_rest, ln_gain.reshape(1, SGU_WIDTH), ln_bias.reshape(1, SGU_WIDTH),
      w.astype(BF16), jnp.transpose(b))


def _attn_kernel(sink_ref, q_ref, gate_ref, kp_ref, kc_ref, kn_ref, vp_ref, vc_ref, vn_ref, o_ref, *, n_steps):
    n = pl.program_id(1)
    log2e = math.log2(math.e)
    qpos = lax.broadcasted_iota(jnp.int32, (WINDOW, 3 * WINDOW), 0)
    kpos = lax.broadcasted_iota(jnp.int32, (WINDOW, 3 * WINDOW), 1) - WINDOW
    band = jnp.abs(kpos - qpos) <= WINDOW
    full = slice(None)
    block = lambda c: pl.ds(c * WINDOW, WINDOW)

    items = []
    for a in range(ATTN_BLOCKS):
        prev = (kp_ref, vp_ref, full) if a == 0 else (kc_ref, vc_ref, block(a - 1))
        nxt = (kn_ref, vn_ref, full) if a == ATTN_BLOCKS - 1 else (kc_ref, vc_ref, block(a + 1))
        keys = [(prev[0], prev[2]), (kc_ref, block(a)), (nxt[0], nxt[2])]
        vals = [(prev[1], prev[2]), (vc_ref, block(a)), (nxt[1], nxt[2])]
        valid = band
        if a == 0:
            valid = jnp.logical_and(valid, jnp.logical_or(kpos >= 0, n > 0))
        if a == ATTN_BLOCKS - 1:
            valid = jnp.logical_and(valid, jnp.logical_or(kpos < WINDOW, n < n_steps - 1))
        valid = jnp.concatenate([valid] * ATT_GROUP, axis=0)
        items += [(block(a), keys, vals, valid, hk) for hk in range(ATT_KV_HEADS)]

    def scores(item):
        rows, keys, _, _, hk = item
        kcols = pl.ds(hk * HEAD_DIM, HEAD_DIM)
        q = jnp.concatenate([q_ref[rows, pl.ds((hk * ATT_GROUP + g) * HEAD_DIM, HEAD_DIM)]
                             for g in range(ATT_GROUP)], axis=0)
        k = jnp.concatenate([ref[r, kcols] for ref, r in keys], axis=0)
        return lax.dot_general(q, k, (((1,), (1,)), ((), ())), preferred_element_type=F32)

    def finish(item, s):
        rows, _, vals, valid, hk = item
        kcols = pl.ds(hk * HEAD_DIM, HEAD_DIM)
        v = jnp.concatenate([ref[r, kcols] for ref, r in vals], axis=0)
        s = jnp.where(valid, s * (HEAD_DIM ** -0.5 * log2e), -jnp.inf)
        sink = jnp.concatenate([jnp.full((WINDOW, 1), sink_ref[hk * ATT_GROUP + g] * log2e, F32)
                                for g in range(ATT_GROUP)], axis=0)
        m = jnp.maximum(jnp.max(s, axis=-1, keepdims=True), sink)
        e = jnp.exp2(s - m)
        denom = jnp.sum(e, axis=-1, keepdims=True) + jnp.exp2(sink - m)
        o = jnp.dot(e.astype(BF16), v, preferred_element_type=F32) / denom
        for g in range(ATT_GROUP):
            cols = pl.ds((hk * ATT_GROUP + g) * HEAD_DIM, HEAD_DIM)
            o_ref[rows, cols] = (o[g * WINDOW:(g + 1) * WINDOW, :] * gate_ref[rows, cols].astype(F32)).astype(BF16)

    pending = [scores(item) for item in items[:ATTN_LOOKAHEAD]]
    for i, item in enumerate(items):
        if i + ATTN_LOOKAHEAD < len(items):
            pending.append(scores(items[i + ATTN_LOOKAHEAD]))
        finish(item, pending.pop(0))


def _attention(h_rest, sink, batch, seq):
    t = batch * seq
    rows = ATTN_BLOCKS * WINDOW
    assert seq % rows == 0, (seq, rows)
    n_steps = seq // rows
    nb = seq // WINDOW
    cur = lambda b, n: b * n_steps + n
    prv = lambda b, n: b * nb + jnp.maximum(n * ATTN_BLOCKS - 1, 0)
    nxt = lambda b, n: b * nb + jnp.minimum((n + 1) * ATTN_BLOCKS, nb - 1)
    wide = lambda blk: pl.BlockSpec((rows, ATT_WIDTH), lambda b, n: (cur(b, n), blk))
    kv_cur = lambda blk: pl.BlockSpec((rows, ATT_KV_WIDTH), lambda b, n: (cur(b, n), blk))
    kv_edge = lambda rowmap, blk: pl.BlockSpec((WINDOW, ATT_KV_WIDTH), lambda b, n: (rowmap(b, n), blk))
    return pl.pallas_call(
        functools.partial(_attn_kernel, n_steps=n_steps),
        grid=(batch, n_steps),
        in_specs=[pl.BlockSpec(memory_space=pltpu.SMEM),
                  wide(H_ATT_Q), wide(H_ATT_GATE),
                  kv_edge(prv, H_ATT_K), kv_cur(H_ATT_K), kv_edge(nxt, H_ATT_K),
                  kv_edge(prv, H_ATT_V), kv_cur(H_ATT_V), kv_edge(nxt, H_ATT_V)],
        out_specs=pl.BlockSpec((rows, ATT_WIDTH), lambda b, n: (cur(b, n), 0)),
        out_shape=jax.ShapeDtypeStruct((t, ATT_WIDTH), BF16),
        compiler_params=_params("parallel", "parallel"),
        name="window_attention",
    )(sink, h_rest, h_rest, h_rest, h_rest, h_rest, h_rest, h_rest, h_rest)


def _merge_kernel(x_ref, oa_ref, ob_ref, oc_ref, g0_ref, g1_ref, g2_ref, wbr_ref, bias_ref, o_ref):
    x = x_ref[...]
    acc = None
    row0 = 0
    for i, (g_ref, o_in) in enumerate(((g0_ref, oa_ref), (g1_ref, ob_ref), (g2_ref, oc_ref))):
        width = o_in.shape[1]
        gate = jax.nn.sigmoid(jnp.dot(x, g_ref[...], preferred_element_type=F32) + bias_ref[i:i + 1, :])
        term = gate * jnp.dot(o_in[...], wbr_ref[row0:row0 + width, :], preferred_element_type=F32)
        acc = term if acc is None else acc + term
        row0 += width
    o_ref[...] = acc.astype(BF16)


def _merge(xn, o_a, o_b, o_c, w_rest, w_br, gate_bias):
    t = xn.shape[0]
    tm = min(MERGE_TM, t)
    nj = D_MODEL // MERGE_TN
    g0 = REST_ACT_WIDTH // MERGE_TN
    row = lambda width: pl.BlockSpec((tm, width), lambda i, j: (i, 0))
    gcol = lambda br: pl.BlockSpec((D_MODEL, MERGE_TN), lambda i, j: (0, g0 + br * nj + j))
    return pl.pallas_call(
        _merge_kernel,
        grid=(t // tm, nj),
        in_specs=[row(D_MODEL), row(GLA_WIDTH), row(SGU_WIDTH), row(ATT_WIDTH),
                  gcol(0), gcol(1), gcol(2),
                  pl.BlockSpec((GLA_WIDTH + SGU_WIDTH + ATT_WIDTH, MERGE_TN), lambda i, j: (0, j)),
                  pl.BlockSpec((N_BRANCH, MERGE_TN), lambda i, j: (0, j))],
        out_specs=pl.BlockSpec((tm, MERGE_TN), lambda i, j: (i, j)),
        out_shape=jax.ShapeDtypeStruct((t, D_MODEL), BF16),
        compiler_params=_params("parallel", "arbitrary"),
        name="gated_merge",
    )(xn, o_a, o_b, o_c, w_rest, w_rest, w_rest, w_br, gate_bias)


def _out_kernel(m_ref, w_ref, x_ref, o_ref):
    o_ref[...] = x_ref[...] + jnp.dot(m_ref[...], w_ref[...], preferred_element_type=F32)


def _out_proj(merged, w_out, x2):
    t = x2.shape[0]
    tm = min(OUT_TM, t)
    return pl.pallas_call(
        _out_kernel,
        grid=(t // tm, D_MODEL // OUT_TN),
        in_specs=[pl.BlockSpec((tm, D_MODEL), lambda i, j: (i, 0)),
                  pl.BlockSpec((D_MODEL, OUT_TN), lambda i, j: (0, j)),
                  pl.BlockSpec((tm, OUT_TN), lambda i, j: (i, j))],
        out_specs=pl.BlockSpec((tm, OUT_TN), lambda i, j: (i, j)),
        out_shape=jax.ShapeDtypeStruct((t, D_MODEL), F32),
        compiler_params=_params("parallel", "arbitrary"),
        name="out_proj",
    )(merged, w_out, x2)


def _cast_kernel(x_ref, o_ref):
    o_ref[...] = x_ref[...].astype(BF16)


def _cast_bf16(w, layer, name):
    rows, cols = w.shape[1], w.shape[2]
    return pl.pallas_call(
        _cast_kernel,
        grid=(rows // CAST_ROWS, cols // CAST_TN),
        in_specs=[pl.BlockSpec((None, CAST_ROWS, CAST_TN), lambda r, j: (layer, r, j))],
        out_specs=pl.BlockSpec((CAST_ROWS, CAST_TN), lambda r, j: (r, j)),
        out_shape=jax.ShapeDtypeStruct((rows, cols), BF16),
        compiler_params=_params("parallel", "parallel"),
        name=name,
    )(w)


def _cast_t_kernel(x_ref, o_ref):
    o_ref[...] = jnp.transpose(x_ref[...]).astype(BF16)


def _cast_t_shift_kernel(a_ref, b_ref, o_ref):
    both = jnp.concatenate([a_ref[CAST_SHIFT:, :], b_ref[...]], axis=0)
    o_ref[...] = jnp.transpose(both).astype(BF16)


def _cast_t_head_kernel(x_ref, o_ref):
    xt = jnp.transpose(x_ref[...])
    lane = lax.broadcasted_iota(jnp.int32, xt.shape, 1)
    o_ref[...] = jnp.where(lane < CAST_SHIFT, xt, 0.0).astype(BF16)


def _cast_t_bf16(wt, layer, row0, n_rows, name):
    k = wt.shape[2]
    tile0 = row0 // CAST_TN
    return pl.pallas_call(
        _cast_t_kernel,
        grid=(k // CAST_ROWS, n_rows // CAST_TN),
        in_specs=[pl.BlockSpec((None, CAST_TN, CAST_ROWS), lambda r, j: (layer, tile0 + j, r))],
        out_specs=pl.BlockSpec((CAST_ROWS, CAST_TN), lambda r, j: (r, j)),
        out_shape=jax.ShapeDtypeStruct((k, n_rows), BF16),
        compiler_params=_params("parallel", "parallel"),
        name=name,
    )(wt)


def _cast_t_bf16_shifted(wt, layer, row0, name):
    n, k = wt.shape[1], wt.shape[2]
    n_rows = n - row0 - CAST_SHIFT
    tile0 = row0 // CAST_TN
    shifts_per_tile = CAST_TN // CAST_SHIFT
    return pl.pallas_call(
        _cast_t_shift_kernel,
        grid=(k // CAST_ROWS, n_rows // CAST_TN),
        in_specs=[pl.BlockSpec((None, CAST_TN, CAST_ROWS), lambda r, j: (layer, tile0 + j, r)),
                  pl.BlockSpec((None, CAST_SHIFT, CAST_ROWS),
                               lambda r, j: (layer, (tile0 + j + 1) * shifts_per_tile, r))],
        out_specs=pl.BlockSpec((CAST_ROWS, CAST_TN), lambda r, j: (r, j)),
        out_shape=jax.ShapeDtypeStruct((k, n_rows), BF16),
        compiler_params=_params("parallel", "parallel"),
        name=name,
    )(wt, wt)


def _cast_t_bf16_head(wt, layer, row0, name):
    k = wt.shape[2]
    return pl.pallas_call(
        _cast_t_head_kernel,
        grid=(k // CAST_ROWS,),
        in_specs=[pl.BlockSpec((None, LANES, CAST_ROWS), lambda r: (layer, row0 // LANES, r))],
        out_specs=pl.BlockSpec((CAST_ROWS, LANES), lambda r: (r, 0)),
        out_shape=jax.ShapeDtypeStruct((k, LANES), BF16),
        compiler_params=_params("parallel"),
        name=name,
    )(wt)


def _prepare_layer(layer, norm_gain, w_in, gla_gate_up, gla_gate_bias, gla_norm_gain, sgu_ln_gain, sgu_ln_bias,
                   sgu_w, sgu_b, q_norm_gain, k_norm_gain, sink, gate_bias, w_br, w_out):
    lr0 = 2 * GLA_QK_WIDTH + GLA_WIDTH
    lr1 = lr0 + 2 * GLA_GATE_RANK
    assert lr1 - lr0 == CAST_SHIFT and lr0 % CAST_TN == 0
    w_in_t = jnp.swapaxes(w_in, 1, 2)
    up2 = jnp.zeros((LANES, 2 * GLA_QK_WIDTH), F32)
    up2 = up2.at[:GLA_GATE_RANK, :GLA_QK_WIDTH].set(gla_gate_up[layer, 0])
    up2 = up2.at[GLA_GATE_RANK:2 * GLA_GATE_RANK, GLA_QK_WIDTH:].set(gla_gate_up[layer, 1])
    return dict(
        norm_gain=norm_gain[layer],
        w_qkv=_cast_t_bf16(w_in_t, layer, 0, lr0, "cast_w_qkv"),
        w_rest=_cast_t_bf16_shifted(w_in_t, layer, lr0, "cast_w_rest"),
        w_lr=_cast_t_bf16_head(w_in_t, layer, lr0, "cast_w_lr"), up2=up2.astype(BF16),
        bias2=gla_gate_bias[layer].reshape(1, 2 * GLA_QK_WIDTH), gla_norm_gain=gla_norm_gain[layer],
        sgu_ln_gain=sgu_ln_gain[layer], sgu_ln_bias=sgu_ln_bias[layer], sgu_w=sgu_w[layer], sgu_b=sgu_b[layer],
        q_gain=q_norm_gain[layer].reshape(1, HEAD_DIM), k_gain=k_norm_gain[layer].reshape(1, HEAD_DIM),
        sink=sink[layer], gate_bias=gate_bias[layer],
        w_br=_cast_bf16(w_br, layer, "cast_w_br"),
        w_out=_cast_bf16(w_out, layer, "cast_w_out"))


def _layer(x, p):
    batch, seq, _ = x.shape
    t = batch * seq
    x2 = x.reshape(t, D_MODEL)
    xn = _pre_norm(x2, p["norm_gain"])

    h_qkv = _proj_raw(xn, p["w_qkv"], "proj_gla_qkv")
    h_rest = _proj_rest(xn, p["w_rest"], p["q_gain"], p["k_gain"], seq)
    log_a, gla_exact = _gla_log_decay(xn, p["w_lr"], p["up2"], p["bias2"])

    o_a = _gla(h_qkv, h_rest, log_a, gla_exact, p["gla_norm_gain"], batch, seq)
    o_b = _sgu(h_rest, p["sgu_ln_gain"], p["sgu_ln_bias"], p["sgu_w"], p["sgu_b"])
    o_c = _attention(h_rest, p["sink"], batch, seq)

    merged = _merge(xn, o_a, o_b, o_c, p["w_rest"], p["w_br"], p["gate_bias"])
    return _out_proj(merged, p["w_out"], x2).reshape(batch, seq, D_MODEL)


def kernel(x_prompt, x_sample, norm_gain, w_in, gla_gate_up, gla_gate_bias, gla_norm_gain, sgu_ln_gain,
           sgu_ln_bias, sgu_w, sgu_b, q_norm_gain, k_norm_gain, sink, gate_bias, w_br, w_out):
    y_prompt, y_sample = x_prompt, x_sample
    for l in range(norm_gain.shape[0]):
        p = _prepare_layer(l, norm_gain, w_in, gla_gate_up, gla_gate_bias, gla_norm_gain, sgu_ln_gain,
                           sgu_ln_bias, sgu_w, sgu_b, q_norm_gain, k_norm_gain, sink, gate_bias, w_br, w_out)
        y_prompt = _layer(y_prompt, p)
        y_sample = _layer(y_sample, p)
    return (y_prompt, y_sample)
```

```python
import functools
import math

import jax
import jax.numpy as jnp
from jax import lax
from jax.experimental import pallas as pl
from jax.experimental.pallas import tpu as pltpu

F32 = jnp.float32
BF16 = jnp.bfloat16

D_MODEL = 4096
HEAD_DIM = 128
GLA_WIDTH = 1536
GLA_DV = 256
GLA_HEADS = 6
GLA_DK = 128
GLA_QK_WIDTH = 768
GLA_GATE_RANK = 16
GLA_GATE_TAU = 16.0
SGU_WIDTH = 1024
SGU_CHUNK = 128
SGU_GROUPS = 8
ATT_WIDTH = 1536
ATT_Q_HEADS = 12
ATT_KV_HEADS = 4
ATT_GROUP = 3
ATT_KV_WIDTH = 512
WINDOW = 128
ROPE_THETA = 500000.0
ROPE_DIMS = 32
N_BRANCH = 3
NORM_EPS = 1e-6
LN_EPS = 1e-5

VMEM_LIMIT_BYTES = 56 * 1024 * 1024
LANES = 128
SUBLANES = 8

NORM_ROWS = 512
PROJ_TM = 1024
PROJ_REST_TM = 2048
PROJ_TN = 512
PROJ_RAW_TN = 1024
PROJ_SUB_ROWS = 256
GLA_ROWS = 256
GLA_STEP_BLOCKS = 2
GLA_HEADS_PER_STEP = 6
GLA_CHUNK = 64
GLA_FAST_MAX_DECAY = 40.0
GLA_PATH_WIDE, GLA_PATH_CHUNKED, GLA_PATH_EXACT = 0, 1, 2
SGU_ROWS = 1024
ATTN_BLOCKS = 8
ATTN_LOOKAHEAD = 2
MERGE_TM = 512
MERGE_TN = 512
OUT_TM = 1024
OUT_TN = 1024
CAST_ROWS = 2048
CAST_TN = 512
CAST_SHIFT = 2 * GLA_GATE_RANK

KIND_SILU, KIND_GELU, KIND_QNORM, KIND_KNORM, KIND_RAW = range(5)
_REST_KIND = ([KIND_SILU] * 3 + [KIND_GELU] * 4 + [KIND_SILU] * 2 + [KIND_QNORM] * 3 + [KIND_KNORM, KIND_RAW]
              + [KIND_SILU] * 3)
_REST_OUT_TILE = [6, 7, 8, 0, 1, 2, 3, 4, 5, 9, 10, 11, 15, 16, 12, 13, 14]
REST_ACT_WIDTH = len(_REST_KIND) * PROJ_TN
H_SGU_U, H_SGU_V, H_SGU_GATE = 0, 1, 2
H_GLA_GATE, H_ATT_Q, H_ATT_GATE = 2, 3, 4
H_ATT_K, H_ATT_V = 15, 16


def _params(*sem):
    return pltpu.CompilerParams(dimension_semantics=sem, vmem_limit_bytes=VMEM_LIMIT_BYTES)


def _silu(x):
    return x * jax.nn.sigmoid(x)


def _gelu_tanh(x):
    c = math.sqrt(2.0 / math.pi)
    return x * (0.5 * (1.0 + jnp.tanh(c * (x + 0.044715 * (x * x * x)))))


def _dot_epilogue(x_ref, w_ref, o_ref, epilogue):
    w = w_ref[...]
    for r in range(x_ref.shape[0] // PROJ_SUB_ROWS):
        rows = pl.ds(r * PROJ_SUB_ROWS, PROJ_SUB_ROWS)
        acc = jnp.dot(x_ref[rows, :], w, preferred_element_type=F32)
        o_ref[rows, :] = epilogue(acc, rows).astype(BF16)


def _proj_raw_kernel(x_ref, w_ref, o_ref):
    _dot_epilogue(x_ref, w_ref, o_ref, lambda acc, rows: acc)


def _proj_raw(xn, w, name):
    t = xn.shape[0]
    n = w.shape[1]
    tm = min(PROJ_TM, t)
    return pl.pallas_call(
        _proj_raw_kernel,
        grid=(t // tm, n // PROJ_RAW_TN),
        in_specs=[pl.BlockSpec((tm, D_MODEL), lambda i, j: (i, 0)),
                  pl.BlockSpec((D_MODEL, PROJ_RAW_TN), lambda i, j: (0, j))],
        out_specs=pl.BlockSpec((tm, PROJ_RAW_TN), lambda i, j: (i, j)),
        out_shape=jax.ShapeDtypeStruct((t, n), BF16),
        compiler_params=_params("parallel", "arbitrary"),
        name=name,
    )(xn, w)


def _head_norm_rope(acc, gain, cos_t, sin_lo, sin_hi):
    outs = []
    for h in range(PROJ_TN // HEAD_DIM):
        xh = acc[:, h * HEAD_DIM:(h + 1) * HEAD_DIM]
        ms = jnp.mean(xh * xh, axis=-1, keepdims=True)
        y = xh * lax.rsqrt(ms + NORM_EPS) * gain
        half = ROPE_DIMS // 2
        up = pltpu.roll(y, HEAD_DIM - half, axis=1)
        dn = pltpu.roll(y, half, axis=1)
        outs.append(y * cos_t + up * sin_lo + dn * sin_hi)
    return jnp.concatenate(outs, axis=1)


def _proj_rest_kernel(kind_ref, tile_ref, x_ref, w_ref, qg_ref, kg_ref, cos_ref, slo_ref, shi_ref, o_ref):
    del tile_ref
    kind = kind_ref[pl.program_id(1)]

    def norm_rope(gain_ref):
        return lambda acc, rows: _head_norm_rope(acc, gain_ref[...], cos_ref[rows, :], slo_ref[rows, :],
                                                 shi_ref[rows, :])

    epilogues = {
        KIND_SILU: lambda acc, rows: _silu(acc),
        KIND_GELU: lambda acc, rows: _gelu_tanh(acc),
        KIND_QNORM: norm_rope(qg_ref),
        KIND_KNORM: norm_rope(kg_ref),
        KIND_RAW: lambda acc, rows: acc,
    }
    for k, epilogue in epilogues.items():
        pl.when(kind == k)(functools.partial(_dot_epilogue, x_ref, w_ref, o_ref, epilogue))


def _proj_rest(xn, w_rest, q_gain, k_gain, seq):
    t = xn.shape[0]
    tm = min(PROJ_REST_TM, t)
    blocks_per_seq = seq // tm if seq >= tm else 1
    tables = _rope_tables(seq)
    if seq < tm:
        tables = tuple(jnp.tile(tb, (tm // seq, 1)) for tb in tables)
    vec = pl.BlockSpec((1, HEAD_DIM), lambda i, j, kind, tile: (0, 0))
    tab = pl.BlockSpec((tm, HEAD_DIM), lambda i, j, kind, tile: (i % blocks_per_seq, 0))
    return pl.pallas_call(
        _proj_rest_kernel,
        grid_spec=pltpu.PrefetchScalarGridSpec(
            num_scalar_prefetch=2,
            grid=(t // tm, len(_REST_KIND)),
            in_specs=[pl.BlockSpec((tm, D_MODEL), lambda i, j, kind, tile: (i, 0)),
                      pl.BlockSpec((D_MODEL, PROJ_TN), lambda i, j, kind, tile: (0, j)),
                      vec, vec, tab, tab, tab],
            out_specs=pl.BlockSpec((tm, PROJ_TN), lambda i, j, kind, tile: (i, tile[j]))),
        out_shape=jax.ShapeDtypeStruct((t, REST_ACT_WIDTH), BF16),
        compiler_params=_params("parallel", "arbitrary"),
        name="proj_rest",
    )(jnp.asarray(_REST_KIND, jnp.int32), jnp.asarray(_REST_OUT_TILE, jnp.int32),
      xn, w_rest, q_gain, k_gain, *tables)


def _rope_tables(seq):
    inv_freq = ROPE_THETA ** (-jnp.arange(0, ROPE_DIMS, 2, dtype=F32) / ROPE_DIMS)
    ang = jnp.arange(seq, dtype=F32)[:, None] * inv_freq[None, :]
    cos, sin = jnp.cos(ang), jnp.sin(ang)
    half = ROPE_DIMS // 2
    ones = jnp.ones((seq, HEAD_DIM - ROPE_DIMS), F32)
    zeros = jnp.zeros((seq, HEAD_DIM - ROPE_DIMS), F32)
    zh = jnp.zeros((seq, half), F32)
    cos_t = jnp.concatenate([cos, cos, ones], axis=1)
    sin_lo = jnp.concatenate([-sin, zh, zeros], axis=1)
    sin_hi = jnp.concatenate([zh, sin, zeros], axis=1)
    return cos_t, sin_lo, sin_hi


def _norm_decay_kernel(x_ref, gain_ref, w_ref, up_ref, b_ref, xn_ref, o_ref, mx_ref):
    full_max = lambda v: jnp.broadcast_to(jnp.max(jnp.max(v, axis=1, keepdims=True), axis=0, keepdims=True),
                                          (1, LANES))
    n_blocks = x_ref.shape[0] // GLA_ROWS
    step_max, block_max = [], []
    for r in range(n_blocks):
        rows = pl.ds(r * GLA_ROWS, GLA_ROWS)
        x = x_ref[rows, :]
        ms = jnp.mean(x * x, axis=-1, keepdims=True)
        xn = (x * lax.rsqrt(ms + NORM_EPS) * gain_ref[...]).astype(BF16)
        xn_ref[rows, :] = xn
        lr = jnp.dot(xn, w_ref[...], preferred_element_type=F32)
        z = jnp.dot(lr.astype(BF16), up_ref[...], preferred_element_type=F32) + b_ref[...]
        log_a = (jnp.minimum(z, 0.0) - jnp.log(1.0 + jnp.exp(-jnp.abs(z)))) * (1.0 / GLA_GATE_TAU)
        o_ref[rows, :] = log_a
        step_max.append(full_max(-log_a))
        block_max.append(full_max(-jnp.sum(log_a, axis=0, keepdims=True)))
    zero = [jnp.zeros((1, LANES), F32)] * (SUBLANES // 2 - n_blocks)
    mx_ref[0] = jnp.concatenate(step_max + zero + block_max + zero, axis=0)


def _norm_decay(x2, gain, w_lr, up2, bias2):
    t = x2.shape[0]
    tm = min(NORM_ROWS, t)
    n = 2 * GLA_QK_WIDTH
    xn, log_a, mx = pl.pallas_call(
        _norm_decay_kernel,
        grid=(t // tm,),
        in_specs=[pl.BlockSpec((tm, D_MODEL), lambda i: (i, 0)),
                  pl.BlockSpec((1, D_MODEL), lambda i: (0, 0)),
                  pl.BlockSpec((D_MODEL, LANES), lambda i: (0, 0)),
                  pl.BlockSpec((LANES, n), lambda i: (0, 0)),
                  pl.BlockSpec((1, n), lambda i: (0, 0))],
        out_specs=[pl.BlockSpec((tm, D_MODEL), lambda i: (i, 0)),
                   pl.BlockSpec((tm, n), lambda i: (i, 0)),
                   pl.BlockSpec((1, SUBLANES, LANES), lambda i: (i, 0, 0))],
        out_shape=[jax.ShapeDtypeStruct((t, D_MODEL), BF16),
                   jax.ShapeDtypeStruct((t, n), F32),
                   jax.ShapeDtypeStruct((t // tm, SUBLANES, LANES), F32)],
        compiler_params=_params("parallel"),
        name="norm_decay",
    )(x2, gain.reshape(1, D_MODEL), w_lr, up2, bias2)
    n_blocks = tm // GLA_ROWS
    step_max = mx[:, :n_blocks, 0].reshape(t // GLA_ROWS)
    block_max = mx[:, SUBLANES // 2:SUBLANES // 2 + n_blocks, 0].reshape(t // GLA_ROWS)
    path = jnp.where(block_max <= GLA_FAST_MAX_DECAY, GLA_PATH_WIDE,
                     jnp.where(step_max <= GLA_FAST_MAX_DECAY / GLA_CHUNK, GLA_PATH_CHUNKED, GLA_PATH_EXACT))
    return xn, log_a, path.astype(jnp.int32)


def _gla_head_rows(q, k, v, b, state, mask, chunk_rows, reverse, exact_scratch):
    n_chunks = q.shape[0] // chunk_rows
    chunk = lambda x, c: x[c * chunk_rows:(c + 1) * chunk_rows]
    edge = 0 if reverse else chunk_rows - 1
    b_tot = [chunk(b, c)[edge:edge + 1, :] for c in range(n_chunks)]
    dec_row = [jnp.exp(bt) for bt in b_tot]
    qb = (q * jnp.exp(b)).astype(BF16)

    if exact_scratch is None:
        k_inv = k * jnp.exp(-b)
        s = lax.dot_general(qb, k_inv.astype(BF16), (((1,), (1,)), ((), ())), preferred_element_type=F32)
        o = jnp.dot(jnp.where(mask, s, 0.0).astype(BF16), v, preferred_element_type=F32)
        kb = [chunk(k_inv, c) * dec_row[c] for c in range(n_chunks)]
    else:
        b_scr, k_scr = exact_scratch
        col = lax.broadcasted_iota(jnp.int32, (chunk_rows, chunk_rows), 1)
        o_chunks, kb = [], []
        for c in range(n_chunks):
            qc, bc = chunk(q, c), chunk(b, c)
            b_scr[...] = bc
            k_scr[...] = chunk(k, c)

            def column(j, acc, qc=qc, bc=bc):
                decay = jnp.exp(jnp.minimum(bc - b_scr[pl.ds(j, 1), :], 0.0))
                w = qc * decay * k_scr[pl.ds(j, 1), :]
                return jnp.where(col == j, jnp.sum(w, axis=-1, keepdims=True), acc)

            s = lax.fori_loop(0, chunk_rows, column, jnp.zeros((chunk_rows, chunk_rows), F32))
            lo = c * chunk_rows
            p = jnp.where(mask[lo:lo + chunk_rows, lo:lo + chunk_rows], s, 0.0)
            o_chunks.append(jnp.dot(p.astype(BF16), chunk(v, c), preferred_element_type=F32))
            kb.append(chunk(k, c) * jnp.exp(b_tot[c] - bc))
        o = jnp.concatenate(o_chunks, axis=0)

    o_state = [None] * n_chunks
    for c in (range(n_chunks - 1, -1, -1) if reverse else range(n_chunks)):
        o_state[c] = jnp.dot(chunk(qb, c), state.astype(BF16), preferred_element_type=F32)
        upd = lax.dot_general(kb[c].astype(BF16), chunk(v, c), (((0,), (0,)), ((), ())),
                              preferred_element_type=F32)
        dec = jnp.transpose(jnp.broadcast_to(dec_row[c], (GLA_DK, GLA_DK)))
        state = jnp.concatenate([dec] * (GLA_DV // GLA_DK), axis=1) * state + upd
    return o + jnp.concatenate(o_state, axis=0), state


def _gla_scan_rows(q_ref, k_ref, v_ref, g_ref, state_ref, chunk_rows, reverse, emit, exact_scratch):
    row = lax.broadcasted_iota(jnp.int32, (GLA_ROWS, GLA_ROWS), 0)
    col = lax.broadcasted_iota(jnp.int32, (GLA_ROWS, GLA_ROWS), 1)
    shift = chunk_rows.bit_length() - 1
    mask = jnp.logical_and((col >= row) if reverse else (col <= row),
                           lax.shift_right_logical(row, shift) == lax.shift_right_logical(col, shift))
    tri_b = mask.astype(BF16)
    rest = g_ref[...]
    b_all = None
    for _ in range(3):
        term = rest.astype(BF16)
        rest = rest - term.astype(F32)
        part = jnp.dot(tri_b, term, preferred_element_type=F32)
        b_all = part if b_all is None else b_all + part
    for h in range(GLA_HEADS_PER_STEP):
        kc = pl.ds(h * GLA_DK, GLA_DK)
        vc = pl.ds(h * GLA_DV, GLA_DV)
        q = q_ref[:, kc].astype(F32) * (GLA_DK ** -0.5)
        k = k_ref[:, kc].astype(F32)
        o, new_state = _gla_head_rows(q, k, v_ref[:, vc], b_all[:, h * GLA_DK:(h + 1) * GLA_DK], state_ref[h],
                                      mask, chunk_rows, reverse, exact_scratch)
        state_ref[h] = new_state
        emit(vc, o)


def _gla_scan_step(path_ref, step_block, refs, state_ref, b_scr, k_scr, reverse, emit):
    @pl.when(pl.program_id(2) == 0)
    def _():
        state_ref[...] = jnp.zeros_like(state_ref)

    for sb in (range(GLA_STEP_BLOCKS - 1, -1, -1) if reverse else range(GLA_STEP_BLOCKS)):
        rows = pl.ds(sb * GLA_ROWS, GLA_ROWS)
        sub = tuple(r.at[rows, :] for r in refs)
        emit_rows = functools.partial(emit, rows)
        path = path_ref[step_block * GLA_STEP_BLOCKS + sb]
        pl.when(path == GLA_PATH_WIDE)(
            functools.partial(_gla_scan_rows, *sub, state_ref, GLA_ROWS, reverse, emit_rows, None))
        pl.when(path == GLA_PATH_CHUNKED)(
            functools.partial(_gla_scan_rows, *sub, state_ref, GLA_CHUNK, reverse, emit_rows, None))
        pl.when(path == GLA_PATH_EXACT)(
            functools.partial(_gla_scan_rows, *sub, state_ref, GLA_CHUNK, reverse, emit_rows, (b_scr, k_scr)))


def _gla_fwd_kernel(exact_ref, q_ref, k_ref, v_ref, g_ref, o_ref, state_ref, b_scr, k_scr, *, nb):
    def emit(rows, cols, o):
        o_ref[rows, cols] = o

    step_block = pl.program_id(0) * nb + pl.program_id(2)
    _gla_scan_step(exact_ref, step_block, (q_ref, k_ref, v_ref, g_ref), state_ref, b_scr, k_scr, False, emit)


def _gla_bwd_kernel(exact_ref, q_ref, k_ref, v_ref, g_ref, fwd_ref, gate_ref, gain_ref, o_ref,
                    state_ref, b_scr, k_scr, *, nb):
    def emit(rows, cols, o):
        tot = fwd_ref[rows, cols] + o
        ms = jnp.mean(tot * tot, axis=-1, keepdims=True)
        y = tot * lax.rsqrt(ms + NORM_EPS) * gain_ref[...]
        o_ref[rows, cols] = (y * gate_ref[rows, cols].astype(F32)).astype(BF16)

    step_block = pl.program_id(0) * nb + (nb - 1 - pl.program_id(2))
    _gla_scan_step(exact_ref, step_block, (q_ref, k_ref, v_ref, g_ref), state_ref, b_scr, k_scr, True, emit)


def _gla(h_qkv, h_rest, log_a, exact, gain, batch, seq):
    t = batch * seq
    step_rows = GLA_STEP_BLOCKS * GLA_ROWS
    assert seq % step_rows == 0, (seq, step_rows)
    nb = seq // step_rows
    hp = GLA_HEADS_PER_STEP
    qk_w, v_w = hp * GLA_DK, hp * GLA_DV
    k_off = GLA_QK_WIDTH // qk_w
    v_off = 2 * GLA_QK_WIDTH // v_w
    gate_off = H_GLA_GATE * GLA_WIDTH // v_w
    dir_off = GLA_QK_WIDTH // qk_w
    grid = (batch, GLA_HEADS // hp, nb)
    scratch = [pltpu.VMEM((hp, GLA_DK, GLA_DV), F32),
               pltpu.VMEM((GLA_CHUNK, GLA_DK), F32), pltpu.VMEM((GLA_CHUNK, GLA_DK), F32)]

    def specs(rowmap, direction):
        return [pl.BlockSpec((step_rows, qk_w), lambda b, h, n, e: (rowmap(b, n), h)),
                pl.BlockSpec((step_rows, qk_w), lambda b, h, n, e: (rowmap(b, n), k_off + h)),
                pl.BlockSpec((step_rows, v_w), lambda b, h, n, e: (rowmap(b, n), v_off + h)),
                pl.BlockSpec((step_rows, qk_w), lambda b, h, n, e: (rowmap(b, n), direction * dir_off + h))]

    fmap = lambda b, n: b * nb + n
    rmap = lambda b, n: b * nb + (nb - 1 - n)
    fwd = pl.pallas_call(
        functools.partial(_gla_fwd_kernel, nb=nb),
        grid_spec=pltpu.PrefetchScalarGridSpec(
            num_scalar_prefetch=1,
            grid=grid,
            in_specs=specs(fmap, 0),
            out_specs=pl.BlockSpec((step_rows, v_w), lambda b, h, n, e: (fmap(b, n), h)),
            scratch_shapes=scratch),
        out_shape=jax.ShapeDtypeStruct((t, GLA_WIDTH), F32),
        compiler_params=_params("parallel", "parallel", "arbitrary"),
        name="gla_fwd",
    )(exact, h_qkv, h_qkv, h_qkv, log_a)
    return pl.pallas_call(
        functools.partial(_gla_bwd_kernel, nb=nb),
        grid_spec=pltpu.PrefetchScalarGridSpec(
            num_scalar_prefetch=1,
            grid=grid,
            in_specs=specs(rmap, 1) + [
                pl.BlockSpec((step_rows, v_w), lambda b, h, n, e: (rmap(b, n), h)),
                pl.BlockSpec((step_rows, v_w), lambda b, h, n, e: (rmap(b, n), gate_off + h)),
                pl.BlockSpec((1, GLA_DV), lambda b, h, n, e: (0, 0))],
            out_specs=pl.BlockSpec((step_rows, v_w), lambda b, h, n, e: (rmap(b, n), h)),
            scratch_shapes=scratch),
        out_shape=jax.ShapeDtypeStruct((t, GLA_WIDTH), BF16),
        compiler_params=_params("parallel", "parallel", "arbitrary"),
        name="gla_bwd",
    )(exact, h_qkv, h_qkv, h_qkv, log_a, fwd, h_rest, gain.reshape(1, GLA_DV))


def _sgu_kernel(u_ref, v_ref, gate_ref, lng_ref, lnb_ref, w_ref, bt_ref, o_ref):
    for c in range(SGU_ROWS // SGU_CHUNK):
        rows = pl.ds(c * SGU_CHUNK, SGU_CHUNK)
        x = v_ref[rows, :].astype(F32)
        mu = jnp.mean(x, axis=-1, keepdims=True)
        xc = x - mu
        var = jnp.mean(xc * xc, axis=-1, keepdims=True)
        y = (xc * lax.rsqrt(var + LN_EPS) * lng_ref[...] + lnb_ref[...]).astype(BF16)
        for g in range(SGU_GROUPS):
            cols = pl.ds(g * LANES, LANES)
            mixed = jnp.dot(w_ref[g], y[:, g * LANES:(g + 1) * LANES], preferred_element_type=F32)
            mixed = mixed + bt_ref[:, g:g + 1]
            o_ref[rows, cols] = (u_ref[rows, cols].astype(F32) * mixed
                                 * gate_ref[rows, cols].astype(F32)).astype(BF16)


def _sgu(h_rest, ln_gain, ln_bias, w, b):
    t = h_rest.shape[0]
    blk = lambda c: pl.BlockSpec((SGU_ROWS, SGU_WIDTH), lambda i: (i, c))
    return pl.pallas_call(
        _sgu_kernel,
        grid=(t // SGU_ROWS,),
        in_specs=[blk(H_SGU_U), blk(H_SGU_V), blk(H_SGU_GATE),
                  pl.BlockSpec((1, SGU_WIDTH), lambda i: (0, 0)),
                  pl.BlockSpec((1, SGU_WIDTH), lambda i: (0, 0)),
                  pl.BlockSpec((SGU_GROUPS, SGU_CHUNK, SGU_CHUNK), lambda i: (0, 0, 0)),
                  pl.BlockSpec((SGU_CHUNK, SGU_GROUPS), lambda i: (0, 0))],
        out_specs=pl.BlockSpec((SGU_ROWS, SGU_WIDTH), lambda i: (i, 0)),
        out_shape=jax.ShapeDtypeStruct((t, SGU_WIDTH), BF16),
        compiler_params=_params("parallel"),
        name="sgu",
    )(h_rest, h_rest, h_rest, ln_gain.reshape(1, SGU_WIDTH), ln_bias.reshape(1, SGU_WIDTH),
      w.astype(BF16), jnp.transpose(b))


def _attn_kernel(sink_ref, q_ref, gate_ref, kp_ref, kc_ref, kn_ref, vp_ref, vc_ref, vn_ref, o_ref, *, n_steps):
    n = pl.program_id(1)
    log2e = math.log2(math.e)
    qpos = lax.broadcasted_iota(jnp.int32, (WINDOW, 3 * WINDOW), 0)
    kpos = lax.broadcasted_iota(jnp.int32, (WINDOW, 3 * WINDOW), 1) - WINDOW
    band = jnp.abs(kpos - qpos) <= WINDOW
    full = slice(None)
    block = lambda c: pl.ds(c * WINDOW, WINDOW)

    items = []
    for a in range(ATTN_BLOCKS):
        prev = (kp_ref, vp_ref, full) if a == 0 else (kc_ref, vc_ref, block(a - 1))
        nxt = (kn_ref, vn_ref, full) if a == ATTN_BLOCKS - 1 else (kc_ref, vc_ref, block(a + 1))
        keys = [(prev[0], prev[2]), (kc_ref, block(a)), (nxt[0], nxt[2])]
        vals = [(prev[1], prev[2]), (vc_ref, block(a)), (nxt[1], nxt[2])]
        valid = band
        if a == 0:
            valid = jnp.logical_and(valid, jnp.logical_or(kpos >= 0, n > 0))
        if a == ATTN_BLOCKS - 1:
            valid = jnp.logical_and(valid, jnp.logical_or(kpos < WINDOW, n < n_steps - 1))
        valid = jnp.concatenate([valid] * ATT_GROUP, axis=0)
        items += [(block(a), keys, vals, valid, hk) for hk in range(ATT_KV_HEADS)]

    def scores(item):
        rows, keys, _, _, hk = item
        kcols = pl.ds(hk * HEAD_DIM, HEAD_DIM)
        q = jnp.concatenate([q_ref[rows, pl.ds((hk * ATT_GROUP + g) * HEAD_DIM, HEAD_DIM)]
                             for g in range(ATT_GROUP)], axis=0)
        k = jnp.concatenate([ref[r, kcols] for ref, r in keys], axis=0)
        return lax.dot_general(q, k, (((1,), (1,)), ((), ())), preferred_element_type=F32)

    def finish(item, s):
        rows, _, vals, valid, hk = item
        kcols = pl.ds(hk * HEAD_DIM, HEAD_DIM)
        v = jnp.concatenate([ref[r, kcols] for ref, r in vals], axis=0)
        s = jnp.where(valid, s * (HEAD_DIM ** -0.5 * log2e), -jnp.inf)
        sink = jnp.concatenate([jnp.full((WINDOW, 1), sink_ref[hk * ATT_GROUP + g] * log2e, F32)
                                for g in range(ATT_GROUP)], axis=0)
        m = jnp.maximum(jnp.max(s, axis=-1, keepdims=True), sink)
        e = jnp.exp2(s - m)
        denom = jnp.sum(e, axis=-1, keepdims=True) + jnp.exp2(sink - m)
        o = jnp.dot(e.astype(BF16), v, preferred_element_type=F32) / denom
        for g in range(ATT_GROUP):
            cols = pl.ds((hk * ATT_GROUP + g) * HEAD_DIM, HEAD_DIM)
            o_ref[rows, cols] = (o[g * WINDOW:(g + 1) * WINDOW, :] * gate_ref[rows, cols].astype(F32)).astype(BF16)

    pending = [scores(item) for item in items[:ATTN_LOOKAHEAD]]
    for i, item in enumerate(items):
        if i + ATTN_LOOKAHEAD < len(items):
            pending.append(scores(items[i + ATTN_LOOKAHEAD]))
        finish(item, pending.pop(0))


def _attention(h_rest, sink, batch, seq):
    t = batch * seq
    rows = ATTN_BLOCKS * WINDOW
    assert seq % rows == 0, (seq, rows)
    n_steps = seq // rows
    nb = seq // WINDOW
    cur = lambda b, n: b * n_steps + n
    prv = lambda b, n: b * nb + jnp.maximum(n * ATTN_BLOCKS - 1, 0)
    nxt = lambda b, n: b * nb + jnp.minimum((n + 1) * ATTN_BLOCKS, nb - 1)
    wide = lambda blk: pl.BlockSpec((rows, ATT_WIDTH), lambda b, n: (cur(b, n), blk))
    kv_cur = lambda blk: pl.BlockSpec((rows, ATT_KV_WIDTH), lambda b, n: (cur(b, n), blk))
    kv_edge = lambda rowmap, blk: pl.BlockSpec((WINDOW, ATT_KV_WIDTH), lambda b, n: (rowmap(b, n), blk))
    return pl.pallas_call(
        functools.partial(_attn_kernel, n_steps=n_steps),
        grid=(batch, n_steps),
        in_specs=[pl.BlockSpec(memory_space=pltpu.SMEM),
                  wide(H_ATT_Q), wide(H_ATT_GATE),
                  kv_edge(prv, H_ATT_K), kv_cur(H_ATT_K), kv_edge(nxt, H_ATT_K),
                  kv_edge(prv, H_ATT_V), kv_cur(H_ATT_V), kv_edge(nxt, H_ATT_V)],
        out_specs=pl.BlockSpec((rows, ATT_WIDTH), lambda b, n: (cur(b, n), 0)),
        out_shape=jax.ShapeDtypeStruct((t, ATT_WIDTH), BF16),
        compiler_params=_params("parallel", "parallel"),
        name="window_attention",
    )(sink, h_rest, h_rest, h_rest, h_rest, h_rest, h_rest, h_rest, h_rest)


def _merge_kernel(x_ref, oa_ref, ob_ref, oc_ref, g0_ref, g1_ref, g2_ref, wbr_ref, bias_ref, o_ref):
    x = x_ref[...]
    acc = None
    row0 = 0
    for i, (g_ref, o_in) in enumerate(((g0_ref, oa_ref), (g1_ref, ob_ref), (g2_ref, oc_ref))):
        width = o_in.shape[1]
        gate = jax.nn.sigmoid(jnp.dot(x, g_ref[...], preferred_element_type=F32) + bias_ref[i:i + 1, :])
        term = gate * jnp.dot(o_in[...], wbr_ref[row0:row0 + width, :], preferred_element_type=F32)
        acc = term if acc is None else acc + term
        row0 += width
    o_ref[...] = acc.astype(BF16)


def _merge(xn, o_a, o_b, o_c, w_rest, w_br, gate_bias):
    t = xn.shape[0]
    tm = min(MERGE_TM, t)
    nj = D_MODEL // MERGE_TN
    g0 = REST_ACT_WIDTH // MERGE_TN
    row = lambda width: pl.BlockSpec((tm, width), lambda i, j: (i, 0))
    gcol = lambda br: pl.BlockSpec((D_MODEL, MERGE_TN), lambda i, j: (0, g0 + br * nj + j))
    return pl.pallas_call(
        _merge_kernel,
        grid=(t // tm, nj),
        in_specs=[row(D_MODEL), row(GLA_WIDTH), row(SGU_WIDTH), row(ATT_WIDTH),
                  gcol(0), gcol(1), gcol(2),
                  pl.BlockSpec((GLA_WIDTH + SGU_WIDTH + ATT_WIDTH, MERGE_TN), lambda i, j: (0, j)),
                  pl.BlockSpec((N_BRANCH, MERGE_TN), lambda i, j: (0, j))],
        out_specs=pl.BlockSpec((tm, MERGE_TN), lambda i, j: (i, j)),
        out_shape=jax.ShapeDtypeStruct((t, D_MODEL), BF16),
        compiler_params=_params("parallel", "arbitrary"),
        name="gated_merge",
    )(xn, o_a, o_b, o_c, w_rest, w_rest, w_rest, w_br, gate_bias)


def _out_kernel(m_ref, w_ref, x_ref, o_ref):
    o_ref[...] = x_ref[...] + jnp.dot(m_ref[...], w_ref[...], preferred_element_type=F32)


def _out_proj(merged, w_out, x2):
    t = x2.shape[0]
    tm = min(OUT_TM, t)
    return pl.pallas_call(
        _out_kernel,
        grid=(t // tm, D_MODEL // OUT_TN),
        in_specs=[pl.BlockSpec((tm, D_MODEL), lambda i, j: (i, 0)),
                  pl.BlockSpec((D_MODEL, OUT_TN), lambda i, j: (0, j)),
                  pl.BlockSpec((tm, OUT_TN), lambda i, j: (i, j))],
        out_specs=pl.BlockSpec((tm, OUT_TN), lambda i, j: (i, j)),
        out_shape=jax.ShapeDtypeStruct((t, D_MODEL), F32),
        compiler_params=_params("parallel", "arbitrary"),
        name="out_proj",
    )(merged, w_out, x2)


def _cast_kernel(x_ref, o_ref):
    o_ref[...] = x_ref[...].astype(BF16)


def _cast_bf16(w, layer, name):
    rows, cols = w.shape[1], w.shape[2]
    return pl.pallas_call(
        _cast_kernel,
        grid=(rows // CAST_ROWS, cols // CAST_TN),
        in_specs=[pl.BlockSpec((None, CAST_ROWS, CAST_TN), lambda r, j: (layer, r, j))],
        out_specs=pl.BlockSpec((CAST_ROWS, CAST_TN), lambda r, j: (r, j)),
        out_shape=jax.ShapeDtypeStruct((rows, cols), BF16),
        compiler_params=_params("parallel", "parallel"),
        name=name,
    )(w)


def _cast_t_kernel(x_ref, o_ref):
    o_ref[...] = jnp.transpose(x_ref[...]).astype(BF16)


def _cast_t_shift_kernel(a_ref, b_ref, o_ref):
    both = jnp.concatenate([a_ref[CAST_SHIFT:, :], b_ref[...]], axis=0)
    o_ref[...] = jnp.transpose(both).astype(BF16)


def _cast_t_head_kernel(x_ref, o_ref):
    xt = jnp.transpose(x_ref[...])
    lane = lax.broadcasted_iota(jnp.int32, xt.shape, 1)
    o_ref[...] = jnp.where(lane < CAST_SHIFT, xt, 0.0).astype(BF16)


def _cast_t_bf16(wt, layer, row0, n_rows, name):
    k = wt.shape[2]
    tile0 = row0 // CAST_TN
    return pl.pallas_call(
        _cast_t_kernel,
        grid=(k // CAST_ROWS, n_rows // CAST_TN),
        in_specs=[pl.BlockSpec((None, CAST_TN, CAST_ROWS), lambda r, j: (layer, tile0 + j, r))],
        out_specs=pl.BlockSpec((CAST_ROWS, CAST_TN), lambda r, j: (r, j)),
        out_shape=jax.ShapeDtypeStruct((k, n_rows), BF16),
        compiler_params=_params("parallel", "parallel"),
        name=name,
    )(wt)


def _cast_t_bf16_shifted(wt, layer, row0, name):
    n, k = wt.shape[1], wt.shape[2]
    n_rows = n - row0 - CAST_SHIFT
    tile0 = row0 // CAST_TN
    shifts_per_tile = CAST_TN // CAST_SHIFT
    return pl.pallas_call(
        _cast_t_shift_kernel,
        grid=(k // CAST_ROWS, n_rows // CAST_TN),
        in_specs=[pl.BlockSpec((None, CAST_TN, CAST_ROWS), lambda r, j: (layer, tile0 + j, r)),
                  pl.BlockSpec((None, CAST_SHIFT, CAST_ROWS),
                               lambda r, j: (layer, (tile0 + j + 1) * shifts_per_tile, r))],
        out_specs=pl.BlockSpec((CAST_ROWS, CAST_TN), lambda r, j: (r, j)),
        out_shape=jax.ShapeDtypeStruct((k, n_rows), BF16),
        compiler_params=_params("parallel", "parallel"),
        name=name,
    )(wt, wt)


def _cast_t_bf16_head(wt, layer, row0, name):
    k = wt.shape[2]
    return pl.pallas_call(
        _cast_t_head_kernel,
        grid=(k // CAST_ROWS,),
        in_specs=[pl.BlockSpec((None, LANES, CAST_ROWS), lambda r: (layer, row0 // LANES, r))],
        out_specs=pl.BlockSpec((CAST_ROWS, LANES), lambda r: (r, 0)),
        out_shape=jax.ShapeDtypeStruct((k, LANES), BF16),
        compiler_params=_params("parallel"),
        name=name,
    )(wt)


def _prepare_layer(layer, norm_gain, w_in, gla_gate_up, gla_gate_bias, gla_norm_gain, sgu_ln_gain, sgu_ln_bias,
                   sgu_w, sgu_b, q_norm_gain, k_norm_gain, sink, gate_bias, w_br, w_out):
    lr0 = 2 * GLA_QK_WIDTH + GLA_WIDTH
    lr1 = lr0 + 2 * GLA_GATE_RANK
    assert lr1 - lr0 == CAST_SHIFT and lr0 % CAST_TN == 0
    w_in_t = jnp.swapaxes(w_in, 1, 2)
    up2 = jnp.zeros((LANES, 2 * GLA_QK_WIDTH), F32)
    up2 = up2.at[:GLA_GATE_RANK, :GLA_QK_WIDTH].set(gla_gate_up[layer, 0])
    up2 = up2.at[GLA_GATE_RANK:2 * GLA_GATE_RANK, GLA_QK_WIDTH:].set(gla_gate_up[layer, 1])
    return dict(
        norm_gain=norm_gain[layer],
        w_qkv=_cast_t_bf16(w_in_t, layer, 0, lr0, "cast_w_qkv"),
        w_rest=_cast_t_bf16_shifted(w_in_t, layer, lr0, "cast_w_rest"),
        w_lr=_cast_t_bf16_head(w_in_t, layer, lr0, "cast_w_lr"), up2=up2.astype(BF16),
        bias2=gla_gate_bias[layer].reshape(1, 2 * GLA_QK_WIDTH), gla_norm_gain=gla_norm_gain[layer],
        sgu_ln_gain=sgu_ln_gain[layer], sgu_ln_bias=sgu_ln_bias[layer], sgu_w=sgu_w[layer], sgu_b=sgu_b[layer],
        q_gain=q_norm_gain[layer].reshape(1, HEAD_DIM), k_gain=k_norm_gain[layer].reshape(1, HEAD_DIM),
        sink=sink[layer], gate_bias=gate_bias[layer],
        w_br=_cast_bf16(w_br, layer, "cast_w_br"),
        w_out=_cast_bf16(w_out, layer, "cast_w_out"))


def _layer(x, p):
    batch, seq, _ = x.shape
    t = batch * seq
    x2 = x.reshape(t, D_MODEL)
    xn, log_a, gla_exact = _norm_decay(x2, p["norm_gain"], p["w_lr"], p["up2"], p["bias2"])
    h_qkv = _proj_raw(xn, p["w_qkv"], "proj_gla_qkv")
    h_rest = _proj_rest(xn, p["w_rest"], p["q_gain"], p["k_gain"], seq)

    o_a = _gla(h_qkv, h_rest, log_a, gla_exact, p["gla_norm_gain"], batch, seq)
    o_b = _sgu(h_rest, p["sgu_ln_gain"], p["sgu_ln_bias"], p["sgu_w"], p["sgu_b"])
    o_c = _attention(h_rest, p["sink"], batch, seq)

    merged = _merge(xn, o_a, o_b, o_c, p["w_rest"], p["w_br"], p["gate_bias"])
    return _out_proj(merged, p["w_out"], x2).reshape(batch, seq, D_MODEL)


def kernel(x_prompt, x_sample, norm_gain, w_in, gla_gate_up, gla_gate_bias, gla_norm_gain, sgu_ln_gain,
           sgu_ln_bias, sgu_w, sgu_b, q_norm_gain, k_norm_gain, sink, gate_bias, w_br, w_out):
    y_prompt, y_sample = x_prompt, x_sample
    for l in range(norm_gain.shape[0]):
        p = _prepare_layer(l, norm_gain, w_in, gla_gate_up, gla_gate_bias, gla_norm_gain, sgu_ln_gain,
                           sgu_ln_bias, sgu_w, sgu_b, q_norm_gain, k_norm_gain, sink, gate_bias, w_br, w_out)
        y_prompt = _layer(y_prompt, p)
        y_sample = _layer(y_sample, p)
    return (y_prompt, y_sample)
```
